```python
import jax, jax.numpy as jnp
from jax import lax
import numpy as np

D_MODEL = 2048
BATCH = 4
SEQ = 2048
DEPTH = 1
DEC_BATCH = 128
DEC_SEQ = 8
PAST_LEN = 16384
PAGE_SIZE = 128

N_META = 16
D_MIX = D_MODEL
W_A = D_MIX // 2
W_B = D_MIX - W_A
N_LRU_HEADS = 16
LRU_HEAD_DIM = W_A // N_LRU_HEADS
LRU_CONV = 4
LRU_C = 8.0
CONF_KERNEL = 31
N_EXPERTS = 64
N_EXPERT_GROUPS = 8
TOPK_GROUPS = 4
TOP_K = 8
D_EXPERT = D_MODEL // 4
D_SHARED = D_EXPERT
ROUTED_SCALE = 2.5
EPS = 1e-6

kernel_name = 'hymba_rglru_conformer_moe_step'


def rms_norm(x, g):
    xf = x.astype(jnp.float32)
    y = xf * lax.rsqrt(jnp.mean(xf * xf, axis=-1, keepdims=True) + EPS)
    return (y * g.astype(jnp.float32)).astype(x.dtype)


def layer_norm(x, g, b):
    xf = x.astype(jnp.float32)
    mu = jnp.mean(xf, axis=-1, keepdims=True)
    var = jnp.mean(jnp.square(xf - mu), axis=-1, keepdims=True)
    y = (xf - mu) * lax.rsqrt(var + EPS) * g.astype(jnp.float32) + b.astype(jnp.float32)
    return y.astype(x.dtype)


def causal_dwconv(x, buf, w, b):
    K, C = w.shape
    xp = jnp.concatenate([buf.astype(x.dtype), x], axis=1)
    y = lax.conv_general_dilated(xp, w.astype(x.dtype)[:, None, :], window_strides=(1,), padding='VALID',
                                 dimension_numbers=('NWC', 'WIO', 'NWC'), feature_group_count=C)
    return y + b.astype(x.dtype), xp[:, xp.shape[1] - (K - 1):]


def rg_lru(x, h0, wa, ba, wi, bi, lam):
    B, T, W = x.shape
    xf = x.astype(jnp.float32)
    xh = xf.reshape(B, T, N_LRU_HEADS, LRU_HEAD_DIM)
    r = jax.nn.sigmoid(jnp.einsum('bthd,hde->bthe', xh, wa.astype(jnp.float32)).reshape(B, T, W) + ba.astype(jnp.float32))
    i = jax.nn.sigmoid(jnp.einsum('bthd,hde->bthe', xh, wi.astype(jnp.float32)).reshape(B, T, W) + bi.astype(jnp.float32))
    log_a = -LRU_C * r * jax.nn.softplus(-lam.astype(jnp.float32))
    a = jnp.exp(log_a)
    u = jnp.sqrt(-jnp.expm1(2.0 * log_a)) * (i * xf)

    def step(h, au):
        a_t, u_t = au
        h = a_t * h + u_t
        return h, h

    h_last, hs = lax.scan(step, h0.astype(jnp.float32), (jnp.swapaxes(a, 0, 1), jnp.swapaxes(u, 0, 1)))
    return jnp.swapaxes(hs, 0, 1).astype(x.dtype), h_last.astype(h0.dtype)


def moe(x, router_w, router_bias, w_gate, w_up, w_down, sh_gate, sh_up, sh_down):
    B, T, D = x.shape
    xt = x.reshape(B * T, D)
    N = B * T
    scores = jax.nn.sigmoid((xt @ router_w).astype(jnp.float32))
    biased = scores + router_bias.astype(jnp.float32)
    per_group = N_EXPERTS // N_EXPERT_GROUPS
    group_scores = lax.top_k(biased.reshape(N, N_EXPERT_GROUPS, per_group), 2)[0].sum(-1)
    _, gidx = lax.top_k(group_scores, TOPK_GROUPS)
    gmask = jax.nn.one_hot(gidx, N_EXPERT_GROUPS, dtype=jnp.float32).sum(1)
    emask = jnp.repeat(gmask, per_group, axis=1)
    masked = jnp.where(emask > 0, biased, -jnp.inf)
    _, eidx = lax.top_k(masked, TOP_K)
    wsel = jnp.take_along_axis(scores, eidx, axis=1)
    wsel = wsel / (jnp.sum(wsel, axis=-1, keepdims=True) + 1e-20) * ROUTED_SCALE
    gates = jnp.sum(jax.nn.one_hot(eidx, N_EXPERTS, dtype=jnp.float32) * wsel[..., None], axis=1)
    g = jnp.einsum('nd,edf->nef', xt, w_gate)
    u = jnp.einsum('nd,edf->nef', xt, w_up)
    h = jax.nn.silu(g) * u * gates.astype(x.dtype)[:, :, None]
    routed = jnp.einsum('nef,efd->nd', h, w_down)
    shared = (jax.nn.silu(xt @ sh_gate) * (xt @ sh_up)) @ sh_down
    return (routed + shared).reshape(B, T, D)


def layer(x, lru_conv_s, lru_h_s, conf_conv_s, p):
    (norm1_g, w_in, lru_conv_w, lru_conv_b, lru_wa, lru_ba, lru_wi, lru_bi, lru_lambda,
     conf_conv_w, conf_conv_b, conf_ln_g, conf_ln_b, out_norm_a, out_norm_b, w_out,
     norm2_g, router_w, router_bias, exp_w_gate, exp_w_up, exp_w_down, sh_w_gate, sh_w_up, sh_w_down) = p
    xn = rms_norm(x, norm1_g)
    proj = xn @ w_in
    u_a, g_a, u_b, v_b = jnp.split(proj, [W_A, 2 * W_A, 2 * W_A + W_B], axis=-1)
    c_a, new_lru_conv = causal_dwconv(u_a, lru_conv_s, lru_conv_w, lru_conv_b)
    r_a, new_h = rg_lru(c_a, lru_h_s, lru_wa, lru_ba, lru_wi, lru_bi, lru_lambda)
    y_a = jax.nn.gelu(g_a) * r_a
    glu = u_b * jax.nn.sigmoid(v_b)
    c_b, new_conf = causal_dwconv(glu, conf_conv_s, conf_conv_w, conf_conv_b)
    y_b = jax.nn.silu(layer_norm(c_b, conf_ln_g, conf_ln_b))
    y = jnp.concatenate([rms_norm(y_a, out_norm_a), rms_norm(y_b, out_norm_b)], axis=-1) @ w_out
    x = x + y
    x = x + moe(rms_norm(x, norm2_g), router_w, router_bias, exp_w_gate, exp_w_up, exp_w_down,
                sh_w_gate, sh_w_up, sh_w_down)
    return x, new_lru_conv, new_h, new_conf


def trunk(x, lru_conv_all, lru_h_all, conf_all, layer_params, final_norm_g):
    lc, lh, cc = [], [], []
    for l in range(DEPTH):
        p = tuple(w[l] for w in layer_params)
        x, a, h, c = layer(x, lru_conv_all[l], lru_h_all[l], conf_all[l], p)
        lc.append(a)
        lh.append(h)
        cc.append(c)
    return rms_norm(x, final_norm_g), jnp.stack(lc), jnp.stack(lh), jnp.stack(cc)


def setup_inputs(seed: int = 0) -> dict:
    key = jax.random.key(seed)
    ks = jax.random.split(key, 32)
    L = DEPTH

    def nrm(k, shape, scale):
        return jax.random.normal(k, shape, jnp.float32) * scale

    u = jax.random.uniform(ks[14], (L, W_A), jnp.float32, minval=0.9, maxval=0.999)
    a = u ** (1.0 / LRU_C)
    return {
        'x_prompt': nrm(ks[0], (BATCH, SEQ, D_MODEL), 1.0),
        'x_sample': nrm(ks[1], (DEC_BATCH, DEC_SEQ, D_MODEL), 1.0),
        'state_lru_conv': nrm(ks[2], (L, DEC_BATCH, LRU_CONV - 1, W_A), 1.0),
        'state_lru_h': nrm(ks[3], (L, DEC_BATCH, W_A), 0.5),
        'state_conf_conv': nrm(ks[4], (L, DEC_BATCH, CONF_KERNEL - 1, W_B), 0.5),
        'meta_tokens': nrm(ks[5], (N_META, D_MODEL), 1.0),
        'norm1_g': 1.0 + nrm(ks[6], (L, D_MODEL), 0.02),
        'w_in': nrm(ks[7], (L, D_MODEL, 2 * W_A + 2 * W_B), D_MODEL ** -0.5),
        'lru_conv_w': nrm(ks[8], (L, LRU_CONV, W_A), LRU_CONV ** -0.5),
        'lru_conv_b': nrm(ks[9], (L, W_A), 0.01),
        'lru_wa': nrm(ks[10], (L, N_LRU_HEADS, LRU_HEAD_DIM, LRU_HEAD_DIM), LRU_HEAD_DIM ** -0.5),
        'lru_ba': nrm(ks[11], (L, W_A), 0.01),
        'lru_wi': nrm(ks[12], (L, N_LRU_HEADS, LRU_HEAD_DIM, LRU_HEAD_DIM), LRU_HEAD_DIM ** -0.5),
        'lru_bi': nrm(ks[13], (L, W_A), 0.01),
        'lru_lambda': jnp.log(a) - jnp.log1p(-a),
        'conf_conv_w': nrm(ks[15], (L, CONF_KERNEL, W_B), CONF_KERNEL ** -0.5),
        'conf_conv_b': nrm(ks[16], (L, W_B), 0.01),
        'conf_ln_g': 1.0 + nrm(ks[17], (L, W_B), 0.02),
        'conf_ln_b': nrm(ks[18], (L, W_B), 0.01),
        'out_norm_a': 1.0 + nrm(ks[19], (L, W_A), 0.02),
        'out_norm_b': 1.0 + nrm(ks[20], (L, W_B), 0.02),
        'w_out': nrm(ks[21], (L, D_MIX, D_MODEL), D_MIX ** -0.5),
        'norm2_g': 1.0 + nrm(ks[22], (L, D_MODEL), 0.02),
        'router_w': nrm(ks[23], (L, D_MODEL, N_EXPERTS), D_MODEL ** -0.5),
        'router_bias': nrm(ks[24], (L, N_EXPERTS), 0.01),
        'exp_w_gate': nrm(ks[25], (L, N_EXPERTS, D_MODEL, D_EXPERT), D_MODEL ** -0.5),
        'exp_w_up': nrm(ks[26], (L, N_EXPERTS, D_MODEL, D_EXPERT), D_MODEL ** -0.5),
        'exp_w_down': nrm(ks[27], (L, N_EXPERTS, D_EXPERT, D_MODEL), D_EXPERT ** -0.5),
        'sh_w_gate': nrm(ks[28], (L, D_MODEL, D_SHARED), D_MODEL ** -0.5),
        'sh_w_up': nrm(ks[29], (L, D_MODEL, D_SHARED), D_MODEL ** -0.5),
        'sh_w_down': nrm(ks[30], (L, D_SHARED, D_MODEL), D_SHARED ** -0.5),
        'final_norm_g': 1.0 + nrm(ks[31], (D_MODEL,), 0.02),
    }


def reference(x_prompt, x_sample, state_lru_conv, state_lru_h, state_conf_conv, meta_tokens,
              norm1_g, w_in, lru_conv_w, lru_conv_b, lru_wa, lru_ba, lru_wi, lru_bi, lru_lambda,
              conf_conv_w, conf_conv_b, conf_ln_g, conf_ln_b, out_norm_a, out_norm_b, w_out,
              norm2_g, router_w, router_bias, exp_w_gate, exp_w_up, exp_w_down,
              sh_w_gate, sh_w_up, sh_w_down, final_norm_g):
    layer_params = (norm1_g, w_in, lru_conv_w, lru_conv_b, lru_wa, lru_ba, lru_wi, lru_bi, lru_lambda,
                    conf_conv_w, conf_conv_b, conf_ln_g, conf_ln_b, out_norm_a, out_norm_b, w_out,
                    norm2_g, router_w, router_bias, exp_w_gate, exp_w_up, exp_w_down,
                    sh_w_gate, sh_w_up, sh_w_down)
    B = x_prompt.shape[0]
    dt = x_prompt.dtype
    meta = jnp.broadcast_to(meta_tokens.astype(dt)[None], (B, N_META, D_MODEL))
    xp = jnp.concatenate([meta, x_prompt], axis=1)
    zero_lc = jnp.zeros((DEPTH, B, LRU_CONV - 1, W_A), dt)
    zero_h = jnp.zeros((DEPTH, B, W_A), state_lru_h.dtype)
    zero_cc = jnp.zeros((DEPTH, B, CONF_KERNEL - 1, W_B), dt)
    yp, p_lc, p_h, p_cc = trunk(xp, zero_lc, zero_h, zero_cc, layer_params, final_norm_g)
    y_prompt = yp[:, N_META:]
    y_sample, s_lc, s_h, s_cc = trunk(x_sample, state_lru_conv, state_lru_h, state_conf_conv,
                                      layer_params, final_norm_g)
    return (y_prompt, y_sample, p_lc, p_h, p_cc, s_lc, s_h, s_cc)
```

```python
import functools

import jax
import jax.numpy as jnp
from jax import lax
from jax.experimental import pallas as pl
from jax.experimental.pallas import tpu as pltpu

D_MODEL = 2048
N_META = 16
W_A = 1024
W_B = 1024
N_LRU_HEADS = 16
LRU_HEAD_DIM = 64
LRU_CONV = 4
LRU_C = 8.0
CONF_KERNEL = 31
N_EXPERTS = 64
N_EXPERT_GROUPS = 8
EXPERTS_PER_GROUP = 8
TOPK_GROUPS = 4
TOP_K = 8
D_EXPERT = 512
ROUTED_SCALE = 2.5
EPS = 1e-6

SUBLANES = 8
LANES = 128
LRU_HIST = SUBLANES
CONF_HIST = 32
VMEM_LIMIT = 56 * 1024 * 1024

F32 = jnp.float32
BF16 = jnp.bfloat16


def _params(semantics):
    return pltpu.CompilerParams(dimension_semantics=semantics, vmem_limit_bytes=VMEM_LIMIT)


def _rms_norm(x, g):
    return x * lax.rsqrt(jnp.mean(x * x, axis=-1, keepdims=True) + EPS) * g


def _in_proj_kernel(x_ref, g_ref, w_ref, o_ref):
    xn = _rms_norm(x_ref[...], g_ref[...])
    o_ref[...] = jnp.dot(xn.astype(BF16), w_ref[...], preferred_element_type=F32)


def _in_proj(x, g, w, *, tm, tn):
    n, d = x.shape
    f = w.shape[1]
    return pl.pallas_call(
        _in_proj_kernel,
        grid=(f // tn, n // tm),
        in_specs=[
            pl.BlockSpec((tm, d), lambda j, i: (i, 0)),
            pl.BlockSpec((1, d), lambda j, i: (0, 0)),
            pl.BlockSpec((d, tn), lambda j, i: (0, j)),
        ],
        out_specs=pl.BlockSpec((tm, tn), lambda j, i: (i, j)),
        out_shape=jax.ShapeDtypeStruct((n, f), F32),
        compiler_params=_params(("arbitrary", "arbitrary")),
        name="in_proj",
    )(x, g, w)


def _mixer_kernel(
    proj_ref, lc_ref, h0_ref, cc_ref,
    wca_ref, bca_ref, wa_ref, ba_ref, wi_ref, bi_ref, lam_ref,
    wcb_ref, bcb_ref, lng_ref, lnb_ref, nag_ref, nbg_ref,
    y_ref, lc_out, h_out, cc_out,
    ua_ext, glu_ext, hst, a_scr, u_scr, h_scr,
    *, n_seq, n_t,
):
    t = pl.program_id(1)
    rows = n_seq * n_t
    c = W_A

    @pl.when(t == 0)
    def _():
        ua_ext[:, pl.ds(0, LRU_HIST), :] = jnp.zeros((n_seq, LRU_HIST, c), F32)
        ua_ext[:, pl.ds(LRU_HIST - (LRU_CONV - 1), LRU_CONV - 1), :] = lc_ref[...]
        glu_ext[:, pl.ds(0, CONF_HIST), :] = jnp.zeros((n_seq, CONF_HIST, c), F32)
        glu_ext[:, pl.ds(CONF_HIST - (CONF_KERNEL - 1), CONF_KERNEL - 1), :] = cc_ref[...]
        hst[...] = h0_ref[...]

    ua_ext[:, pl.ds(LRU_HIST, n_t), :] = proj_ref[:, pl.ds(0, c)].reshape(n_seq, n_t, c)
    c_a = jnp.zeros((n_seq, n_t, c), F32)
    for j in range(LRU_CONV):
        c_a = c_a + wca_ref[pl.ds(j, 1), :][None] * ua_ext[:, pl.ds(LRU_HIST - (LRU_CONV - 1) + j, n_t), :]
    c_a = (c_a + bca_ref[...][None]).reshape(rows, c)
    c_a16 = c_a.astype(BF16)
    r = jax.nn.sigmoid(jnp.dot(c_a16, wa_ref[...], preferred_element_type=F32) + ba_ref[...])
    i = jax.nn.sigmoid(jnp.dot(c_a16, wi_ref[...], preferred_element_type=F32) + bi_ref[...])
    neg_lam = -lam_ref[...]
    softplus = jnp.maximum(neg_lam, 0.0) + jnp.log1p(jnp.exp(-jnp.abs(neg_lam)))
    log_a = (-LRU_C * r) * softplus
    a = jnp.exp(log_a)
    a_scr[...] = a
    u_scr[...] = jnp.sqrt(-jnp.tanh(log_a) * (a * a + 1.0)) * (i * c_a)

    row8 = lax.broadcasted_iota(jnp.int32, (SUBLANES, c), 0)
    groups = n_t // SUBLANES
    for s in range(n_seq):
        def scan_group(g, carry, s=s):
            off = pl.multiple_of(s * n_t + g * SUBLANES, SUBLANES)
            a = a_scr[pl.ds(off, SUBLANES), :]
            u = u_scr[pl.ds(off, SUBLANES), :]
            for d in (1, 2, 4):
                keep = row8 >= d
                a_sh = jnp.where(keep, pltpu.roll(a, d, 0), 1.0)
                u_sh = jnp.where(keep, pltpu.roll(u, d, 0), 0.0)
                u = a * u_sh + u
                a = a * a_sh
            h = a * carry + u
            h_scr[pl.ds(off, SUBLANES), :] = h
            return jnp.broadcast_to(h[SUBLANES - 1:SUBLANES, :], (SUBLANES, c))

        carry = lax.fori_loop(0, groups, scan_group, jnp.broadcast_to(hst[s], (SUBLANES, c)))
        hst[s] = carry[0:1, :]

    y_a = jax.nn.gelu(proj_ref[:, pl.ds(c, c)]) * h_scr[...]
    y_ref[:, pl.ds(0, c)] = _rms_norm(y_a, nag_ref[...]).astype(y_ref.dtype)

    glu = proj_ref[:, pl.ds(2 * c, c)] * jax.nn.sigmoid(proj_ref[:, pl.ds(3 * c, c)])
    glu_ext[:, pl.ds(CONF_HIST, n_t), :] = glu.reshape(n_seq, n_t, c)
    c_b = jnp.zeros((n_seq, n_t, c), F32)
    for j in range(CONF_KERNEL):
        c_b = c_b + wcb_ref[pl.ds(j, 1), :][None] * glu_ext[:, pl.ds(CONF_HIST - (CONF_KERNEL - 1) + j, n_t), :]
    c_b = (c_b + bcb_ref[...][None]).reshape(rows, c)
    mu = jnp.mean(c_b, axis=-1, keepdims=True)
    cen = c_b - mu
    var = jnp.mean(cen * cen, axis=-1, keepdims=True)
    ln = cen * lax.rsqrt(var + EPS) * lng_ref[...] + lnb_ref[...]
    y_b = ln * jax.nn.sigmoid(ln)
    y_ref[:, pl.ds(c, c)] = _rms_norm(y_b, nbg_ref[...]).astype(y_ref.dtype)

    lc_out[...] = ua_ext[:, pl.ds(LRU_HIST + n_t - (LRU_CONV - 1), LRU_CONV - 1), :]
    cc_out[...] = glu_ext[:, pl.ds(CONF_HIST + n_t - (CONF_KERNEL - 1), CONF_KERNEL - 1), :]
    h_out[...] = hst[...]
    ua_ext[:, pl.ds(0, LRU_HIST), :] = ua_ext[:, pl.ds(n_t, LRU_HIST), :]
    glu_ext[:, pl.ds(0, CONF_HIST), :] = glu_ext[:, pl.ds(n_t, CONF_HIST), :]


def _mixer(proj, row_block0, lc, h0, cc, weights, *, n_seq, n_t, n_tiles):
    b = lc.shape[0]
    c = W_A
    rows = n_seq * n_t
    n_sb = b // n_seq

    def row_map(sb, t):
        return (row_block0 + sb * n_tiles + t, 0)

    def const2(sb, t):
        return (0, 0)

    def seq_map(sb, t):
        return (sb, 0, 0)

    w_specs = [pl.BlockSpec(w.shape, const2) for w in weights]
    return pl.pallas_call(
        functools.partial(_mixer_kernel, n_seq=n_seq, n_t=n_t),
        grid=(n_sb, n_tiles),
        in_specs=[
            pl.BlockSpec((rows, 4 * c), row_map),
            pl.BlockSpec((n_seq, LRU_CONV - 1, c), seq_map),
            pl.BlockSpec((n_seq, 1, c), seq_map),
            pl.BlockSpec((n_seq, CONF_KERNEL - 1, c), seq_map),
        ] + w_specs,
        out_specs=[
            pl.BlockSpec((rows, 2 * c), lambda sb, t: (sb * n_tiles + t, 0)),
            pl.BlockSpec((n_seq, LRU_CONV - 1, c), seq_map),
            pl.BlockSpec((n_seq, 1, c), seq_map),
            pl.BlockSpec((n_seq, CONF_KERNEL - 1, c), seq_map),
        ],
        out_shape=[
            jax.ShapeDtypeStruct((b * n_tiles * n_t, 2 * c), BF16),
            jax.ShapeDtypeStruct((b, LRU_CONV - 1, c), F32),
            jax.ShapeDtypeStruct((b, 1, c), F32),
            jax.ShapeDtypeStruct((b, CONF_KERNEL - 1, c), F32),
        ],
        scratch_shapes=[
            pltpu.VMEM((n_seq, LRU_HIST + n_t, c), F32),
            pltpu.VMEM((n_seq, CONF_HIST + n_t, c), F32),
            pltpu.VMEM((n_seq, 1, c), F32),
            pltpu.VMEM((rows, c), F32),
            pltpu.VMEM((rows, c), F32),
            pltpu.VMEM((rows, c), F32),
        ],
        compiler_params=_params(("arbitrary", "arbitrary")),
        name="mixer",
    )(proj, lc, h0, cc, *weights)


def _out_proj_kernel(y_ref, x_ref, w_ref, g_ref, rw_ref, x1_ref, xn_ref, logit_ref):
    x1 = x_ref[...] + jnp.dot(y_ref[...], w_ref[...], preferred_element_type=F32)
    x1_ref[...] = x1
    xn = _rms_norm(x1, g_ref[...])
    xn_ref[...] = xn
    logit_ref[...] = jnp.dot(xn, rw_ref[...], preferred_element_type=F32, precision=lax.Precision.HIGHEST)


def _out_proj(ycat, x, w_out, g2, router_w, *, tm):
    n, d = x.shape
    e = router_w.shape[1]
    const = lambda i: (0, 0)
    row = lambda i: (i, 0)
    return pl.pallas_call(
        _out_proj_kernel,
        grid=(n // tm,),
        in_specs=[
            pl.BlockSpec((tm, d), row),
            pl.BlockSpec((tm, d), row),
            pl.BlockSpec((d, d), const),
            pl.BlockSpec((1, d), const),
            pl.BlockSpec((d, e), const),
        ],
        out_specs=[pl.BlockSpec((tm, d), row), pl.BlockSpec((tm, d), row), pl.BlockSpec((tm, e), row)],
        out_shape=[
            jax.ShapeDtypeStruct((n, d), F32),
            jax.ShapeDtypeStruct((n, d), F32),
            jax.ShapeDtypeStruct((n, e), F32),
        ],
        compiler_params=_params(("arbitrary",)),
        name="out_proj",
    )(ycat, x, w_out, g2, router_w)


def _first_argmax(work, index, sentinel):
    m = jnp.max(work, axis=0, keepdims=True)
    first = jnp.min(jnp.where(work == m, index, sentinel), axis=0, keepdims=True)
    return m, first


def _route_kernel(logit_ref, bias_ref, eidx_ref, gate_ref):
    n_tok = logit_ref.shape[1]
    scores = jax.nn.sigmoid(logit_ref[...])
    biased = scores + bias_ref[...]
    grouped = biased.reshape(N_EXPERT_GROUPS, EXPERTS_PER_GROUP, n_tok)
    in_group = lax.broadcasted_iota(jnp.int32, grouped.shape, 1)
    top1 = jnp.max(grouped, axis=1, keepdims=True)
    first1 = jnp.min(jnp.where(grouped == top1, in_group, EXPERTS_PER_GROUP), axis=1, keepdims=True)
    top2 = jnp.max(jnp.where(in_group == first1, -jnp.inf, grouped), axis=1, keepdims=True)
    group_scores = (top1 + top2).reshape(N_EXPERT_GROUPS, n_tok)

    group_id = lax.broadcasted_iota(jnp.int32, group_scores.shape, 0)
    group_on = jnp.zeros(group_scores.shape, F32)
    work = group_scores
    for _ in range(TOPK_GROUPS):
        _, first = _first_argmax(work, group_id, N_EXPERT_GROUPS)
        pick = group_id == first
        group_on = jnp.where(pick, 1.0, group_on)
        work = jnp.where(pick, -jnp.inf, work)

    masked = jnp.where(group_on.reshape(N_EXPERT_GROUPS, 1, n_tok) > 0.0, grouped, -jnp.inf)
    work = masked.reshape(N_EXPERTS, n_tok)
    expert_id = lax.broadcasted_iota(jnp.int32, work.shape, 0)
    ids, sel = [], []
    for _ in range(TOP_K):
        _, first = _first_argmax(work, expert_id, N_EXPERTS)
        pick = expert_id == first
        ids.append(first)
        sel.append(jnp.sum(jnp.where(pick, scores, 0.0), axis=0, keepdims=True))
        work = jnp.where(pick, -jnp.inf, work)
    sel = jnp.concatenate(sel, axis=0)
    eidx_ref[...] = jnp.concatenate(ids, axis=0)
    gate_ref[...] = sel / (jnp.sum(sel, axis=0, keepdims=True) + 1e-20) * ROUTED_SCALE


def _route(logits_t, bias, *, tl):
    e, n = logits_t.shape
    return pl.pallas_call(
        _route_kernel,
        grid=(n // tl,),
        in_specs=[pl.BlockSpec((e, tl), lambda i: (0, i)), pl.BlockSpec((e, 1), lambda i: (0, 0))],
        out_specs=[pl.BlockSpec((TOP_K, tl), lambda i: (0, i)), pl.BlockSpec((TOP_K, tl), lambda i: (0, i))],
        out_shape=[jax.ShapeDtypeStruct((TOP_K, n), jnp.int32), jax.ShapeDtypeStruct((TOP_K, n), F32)],
        compiler_params=_params(("arbitrary",)),
        name="route",
    )(logits_t, bias)


def _dispatch_kernel(dest_ref, x_ref, xs_hbm, sem):
    tm = x_ref.shape[0]

    def issue(n, carry):
        for k in range(TOP_K):
            pltpu.make_async_copy(
                x_ref.at[pl.ds(n, 1), :], xs_hbm.at[pl.ds(dest_ref[0, 0, n * TOP_K + k], 1), :], sem
            ).start()
        return carry

    lax.fori_loop(0, tm, issue, 0)
    for _ in range(TOP_K):
        pltpu.make_async_copy(x_ref, xs_hbm.at[pl.ds(0, tm), :], sem).wait()


def _dispatch(dest_flat, xn, *, tm):
    n, d = xn.shape
    return pl.pallas_call(
        _dispatch_kernel,
        grid=(n // tm,),
        in_specs=[
            pl.BlockSpec((1, 1, tm * TOP_K), lambda i: (i, 0, 0), memory_space=pltpu.SMEM),
            pl.BlockSpec((tm, d), lambda i: (i, 0)),
        ],
        out_specs=pl.BlockSpec(memory_space=pl.ANY),
        out_shape=jax.ShapeDtypeStruct((n * TOP_K, d), F32),
        scratch_shapes=[pltpu.SemaphoreType.DMA],
        compiler_params=_params(("arbitrary",)),
        name="dispatch",
    )(dest_flat, xn)


def _experts_kernel(tile_ref, expert_ref, lo_ref, hi_ref, first_ref, xs_ref, wg_ref, wu_ref, wd_ref, ys_ref):
    it = pl.program_id(0)
    tm = xs_ref.shape[0]

    @pl.when(first_ref[it] == 1)
    def _():
        ys_ref[...] = jnp.zeros(ys_ref.shape, F32)

    lo = lo_ref[it]
    hi = hi_ref[it]

    @pl.when(lo < hi)
    def _():
        x = xs_ref[...].astype(BF16)
        g = jnp.dot(x, wg_ref[0].astype(BF16), preferred_element_type=F32)
        u = jnp.dot(x, wu_ref[0].astype(BF16), preferred_element_type=F32)
        row = tile_ref[it] * tm + lax.broadcasted_iota(jnp.int32, (tm, 1), 0)
        h = jnp.where((row >= lo) & (row < hi), g * jax.nn.sigmoid(g) * u, 0.0)
        ys_ref[...] += jnp.dot(h.astype(BF16), wd_ref[0].astype(BF16), preferred_element_type=F32)


def _experts(items, xs, w_gate, w_up, w_down, *, tm):
    p, d = xs.shape
    f = w_gate.shape[2]
    n_items = items[0].shape[0]
    row_map = lambda i, tile, expert, lo, hi, first: (tile[i], 0)
    w_map = lambda i, tile, expert, lo, hi, first: (expert[i], 0, 0)
    return pl.pallas_call(
        _experts_kernel,
        grid_spec=pltpu.PrefetchScalarGridSpec(
            num_scalar_prefetch=5,
            grid=(n_items,),
            in_specs=[
                pl.BlockSpec((tm, d), row_map),
                pl.BlockSpec((1, d, f), w_map),
                pl.BlockSpec((1, d, f), w_map),
                pl.BlockSpec((1, f, d), w_map),
            ],
            out_specs=pl.BlockSpec((tm, d), row_map),
        ),
        out_shape=jax.ShapeDtypeStruct((p, d), F32),
        compiler_params=_params(("arbitrary",)),
        name="experts",
    )(*items, xs, w_gate, w_up, w_down)


def _combine_kernel(dest_ref, gate_ref, x1_ref, xn_ref, sg_ref, su_ref, sd_ref, fg_ref, ys_hbm, o_ref, buf, sem):
    tm = x1_ref.shape[0]

    def issue(n, carry):
        for k in range(TOP_K):
            pltpu.make_async_copy(
                ys_hbm.at[pl.ds(dest_ref[0, 0, n * TOP_K + k], 1), :], buf.at[k, pl.ds(n, 1), :], sem
            ).start()
        return carry

    lax.fori_loop(0, tm, issue, 0)

    xn = xn_ref[...].astype(BF16)
    sg = jnp.dot(xn, sg_ref[...], preferred_element_type=F32)
    su = jnp.dot(xn, su_ref[...], preferred_element_type=F32)
    acc = x1_ref[...] + jnp.dot(
        (sg * jax.nn.sigmoid(sg) * su).astype(BF16), sd_ref[...], preferred_element_type=F32
    )

    for k in range(TOP_K):
        pltpu.make_async_copy(ys_hbm.at[pl.ds(0, tm), :], buf.at[k], sem).wait()
    gates = gate_ref[...]
    for k in range(TOP_K):
        acc = acc + gates[:, k:k + 1] * buf[k]
    o_ref[...] = _rms_norm(acc, fg_ref[...])


def _combine(dest_flat, gates, x1, xn, sh_gate, sh_up, sh_down, final_g, ys, *, tm):
    n, d = x1.shape
    f = sh_gate.shape[1]
    const = lambda i: (0, 0)
    row = lambda i: (i, 0)
    return pl.pallas_call(
        _combine_kernel,
        grid=(n // tm,),
        in_specs=[
            pl.BlockSpec((1, 1, tm * TOP_K), lambda i: (i, 0, 0), memory_space=pltpu.SMEM),
            pl.BlockSpec((tm, TOP_K), row),
            pl.BlockSpec((tm, d), row),
            pl.BlockSpec((tm, d), row),
            pl.BlockSpec((d, f), const),
            pl.BlockSpec((d, f), const),
            pl.BlockSpec((f, d), const),
            pl.BlockSpec((1, d), const),
            pl.BlockSpec(memory_space=pl.ANY),
        ],
        out_specs=pl.BlockSpec((tm, d), row),
        out_shape=jax.ShapeDtypeStruct((n, d), F32),
        scratch_shapes=[pltpu.VMEM((TOP_K, tm, d), F32), pltpu.SemaphoreType.DMA],
        compiler_params=_params(("arbitrary",)),
        name="combine",
    )(dest_flat, gates, x1, xn, sh_gate, sh_up, sh_down, final_g, ys)


def _work_items(eidx, *, tm):
    n = eidx.shape[0]
    total = n * TOP_K
    n_tiles = total // tm
    chosen = jnp.sum(jax.nn.one_hot(eidx, N_EXPERTS, dtype=jnp.int32), axis=1)
    before = jnp.cumsum(chosen, axis=0) - chosen
    counts = jnp.sum(chosen, axis=0)
    offsets = jnp.concatenate([jnp.zeros((1,), jnp.int32), jnp.cumsum(counts).astype(jnp.int32)])
    dest = offsets[eidx] + jnp.take_along_axis(before, eidx, axis=1)
    tile_bounds = jnp.arange(1, n_tiles, dtype=jnp.int32) * tm
    offset_rank = jnp.arange(N_EXPERTS + 1, dtype=jnp.int32) + jnp.clip((offsets + tm - 1) // tm - 1, 0, n_tiles - 1)
    tile_rank = jnp.arange(n_tiles - 1, dtype=jnp.int32) + jnp.sum(
        (offsets[None, :] <= tile_bounds[:, None]).astype(jnp.int32), axis=1
    )
    values = jnp.concatenate([offsets, tile_bounds])
    ranks = jnp.concatenate([offset_rank, tile_rank])
    slots = jnp.arange(values.shape[0], dtype=jnp.int32)
    bounds = jnp.sum(jnp.where(ranks[None, :] == slots[:, None], values[None, :], 0), axis=1)
    lo, hi = bounds[:-1], bounds[1:]
    tile = jnp.minimum(lo // tm, n_tiles - 1)
    expert = jnp.clip(
        jnp.sum((offsets[None, :] <= lo[:, None]).astype(jnp.int32), axis=1) - 1, 0, N_EXPERTS - 1
    )
    first = jnp.concatenate([jnp.ones((1,), jnp.int32), (tile[1:] != tile[:-1]).astype(jnp.int32)])
    return dest.astype(jnp.int32), (tile, expert, lo, hi, first)


def _block_diag(w):
    h, dh, _ = w.shape
    eye = jnp.eye(h, dtype=w.dtype)
    return (eye[:, None, :, None] * w[:, :, None, :]).reshape(h * dh, h * dh)


IN_PROJ_TM = 464
IN_PROJ_TN = 1024
PROMPT_T = 344
SAMPLE_SEQS = 8
OUT_PROJ_TM = 464
ROUTE_TL = 256
DISPATCH_TM = 464
EXPERT_TM = 256
COMBINE_TM = 160


def kernel(x_prompt, x_sample, state_lru_conv, state_lru_h, state_conf_conv, meta_tokens, norm1_g, w_in, lru_conv_w, lru_conv_b, lru_wa, lru_ba, lru_wi, lru_bi, lru_lambda, conf_conv_w, conf_conv_b, conf_ln_g, conf_ln_b, out_norm_a, out_norm_b, w_out, norm2_g, router_w, router_bias, exp_w_gate, exp_w_up, exp_w_down, sh_w_gate, sh_w_up, sh_w_down, final_norm_g):
    b_p, seq, d = x_prompt.shape
    b_s, t_s, _ = x_sample.shape
    t_p = N_META + seq
    n_p = b_p * t_p
    n_s = b_s * t_s
    n = n_p + n_s

    meta = jnp.broadcast_to(meta_tokens[None], (b_p, N_META, d))
    x_all = jnp.concatenate(
        [jnp.concatenate([meta, x_prompt], axis=1).reshape(n_p, d), x_sample.reshape(n_s, d)], axis=0
    )

    row = lambda v: v.reshape(1, -1)
    proj = _in_proj(x_all, row(norm1_g[0]), w_in[0].astype(BF16), tm=IN_PROJ_TM, tn=IN_PROJ_TN)

    mixer_w = (
        lru_conv_w[0], row(lru_conv_b[0]),
        _block_diag(lru_wa[0]).astype(BF16), row(lru_ba[0]),
        _block_diag(lru_wi[0]).astype(BF16), row(lru_bi[0]),
        row(lru_lambda[0]),
        conf_conv_w[0], row(conf_conv_b[0]), row(conf_ln_g[0]), row(conf_ln_b[0]),
        row(out_norm_a[0]), row(out_norm_b[0]),
    )
    y_p, p_lc, p_h, p_cc = _mixer(
        proj, 0,
        jnp.zeros((b_p, LRU_CONV - 1, W_A), F32), jnp.zeros((b_p, 1, W_A), F32),
        jnp.zeros((b_p, CONF_KERNEL - 1, W_B), F32),
        mixer_w, n_seq=1, n_t=PROMPT_T, n_tiles=t_p // PROMPT_T,
    )
    y_s, s_lc, s_h, s_cc = _mixer(
        proj, n_p // (SAMPLE_SEQS * t_s),
        state_lru_conv[0], state_lru_h[0].reshape(b_s, 1, W_A), state_conf_conv[0],
        mixer_w, n_seq=SAMPLE_SEQS, n_t=t_s, n_tiles=1,
    )
    ycat = jnp.concatenate([y_p, y_s], axis=0)

    x1, xn2, logits = _out_proj(ycat, x_all, w_out[0].astype(BF16), row(norm2_g[0]), router_w[0], tm=OUT_PROJ_TM)

    n_pad = -n % ROUTE_TL
    logits_t = jnp.pad(logits.T, ((0, 0), (0, n_pad)))
    eidx_t, gate_t = _route(logits_t, router_bias[0].reshape(N_EXPERTS, 1), tl=ROUTE_TL)
    eidx = eidx_t[:, :n].T
    gates = gate_t[:, :n].T

    dest, items = _work_items(eidx, tm=EXPERT_TM)
    xs = _dispatch(dest.reshape(n // DISPATCH_TM, 1, DISPATCH_TM * TOP_K), xn2, tm=DISPATCH_TM)
    ys = _experts(items, xs, exp_w_gate[0], exp_w_up[0], exp_w_down[0], tm=EXPERT_TM)
    out = _combine(
        dest.reshape(n // COMBINE_TM, 1, COMBINE_TM * TOP_K), gates, x1, xn2,
        sh_w_gate[0].astype(BF16), sh_w_up[0].astype(BF16), sh_w_down[0].astype(BF16),
        row(final_norm_g), ys, tm=COMBINE_TM,
    )

    y_prompt = out[:n_p].reshape(b_p, t_p, d)[:, N_META:]
    y_sample = out[n_p:].reshape(b_s, t_s, d)
    return (
        y_prompt, y_sample,
        p_lc[None], p_h.reshape(1, b_p, W_A), p_cc[None],
        s_lc[None], s_h.reshape(1, b_s, W_A), s_cc[None],
    )
```

```python
import functools

import jax
import jax.numpy as jnp
from jax import lax
from jax.experimental import pallas as pl
from jax.experimental.pallas import tpu as pltpu

D_MODEL = 2048
N_META = 16
W_A = 1024
W_B = 1024
LRU_CONV = 4
LRU_C = 8.0
CONF_KERNEL = 31
N_EXPERTS = 64
N_EXPERT_GROUPS = 8
EXPERTS_PER_GROUP = 8
TOPK_GROUPS = 4
TOP_K = 8
ROUTED_SCALE = 2.5
EPS = 1e-6

SUBLANES = 8
LRU_HIST = SUBLANES
CONF_HIST = 32
VMEM_LIMIT = 56 * 1024 * 1024

F32 = jnp.float32
BF16 = jnp.bfloat16


def _params(semantics):
    return pltpu.CompilerParams(dimension_semantics=semantics, vmem_limit_bytes=VMEM_LIMIT)


def _rms_norm(x, g):
    return x * lax.rsqrt(jnp.mean(x * x, axis=-1, keepdims=True) + EPS) * g


def _part_starts(n_blocks):
    starts, s = [], 0
    for nb in n_blocks:
        starts.append(s)
        s += nb
    return starts


def _part_map(start, nb, grid_axis):
    def index_map(*ids):
        return (jnp.clip(ids[grid_axis] - start, 0, nb - 1), 0)

    return index_map


def _for_part(i, n_blocks, body):
    for p, (start, nb) in enumerate(zip(_part_starts(n_blocks), n_blocks)):
        pl.when((i >= start) & (i < start + nb))(functools.partial(body, p))


def _in_proj_kernel(*refs, n_blocks):
    x_refs = refs[:len(n_blocks)]
    g_ref, w_ref, o_ref = refs[len(n_blocks):]

    def body(p):
        xn = _rms_norm(x_refs[p][...], g_ref[...])
        o_ref[...] = jnp.dot(xn.astype(BF16), w_ref[...], preferred_element_type=F32)

    _for_part(pl.program_id(1), n_blocks, body)


def _in_proj(x_parts, g, w, *, tm, tn):
    d, f = w.shape
    n_blocks = tuple(x.shape[0] // tm for x in x_parts)
    x_specs = [
        pl.BlockSpec((tm, d), _part_map(start, nb, 1)) for start, nb in zip(_part_starts(n_blocks), n_blocks)
    ]
    return pl.pallas_call(
        functools.partial(_in_proj_kernel, n_blocks=n_blocks),
        grid=(f // tn, sum(n_blocks)),
        in_specs=x_specs + [
            pl.BlockSpec((1, d), lambda j, i: (0, 0)),
            pl.BlockSpec((d, tn), lambda j, i: (0, j)),
        ],
        out_specs=pl.BlockSpec((tm, tn), lambda j, i: (i, j)),
        out_shape=jax.ShapeDtypeStruct((sum(n_blocks) * tm, f), F32),
        compiler_params=_params(("arbitrary", "arbitrary")),
        name="in_proj",
    )(*x_parts, g, w)


def _causal_conv(ext, w_ref, n_t, hist):
    taps = w_ref.shape[0]
    length = ext.shape[1]
    shifted = {0: ext}
    acc = None
    for j in range(taps):
        q, r = divmod(hist - (taps - 1) + j, SUBLANES)
        if r not in shifted:
            shifted[r] = pltpu.roll(ext, length - r, 1)
        term = w_ref[pl.ds(j, 1), :][None] * shifted[r][:, q * SUBLANES:q * SUBLANES + n_t, :]
        acc = term if acc is None else acc + term
    return acc


def _mixer_kernel(
    proj_ref, lc_ref, h0_ref, cc_ref,
    wca_ref, bca_ref, wa_ref, ba_ref, wi_ref, bi_ref, lam_ref,
    wcb_ref, bcb_ref, lng_ref, lnb_ref, nag_ref, nbg_ref,
    y_ref, lc_out, h_out, cc_out,
    ua_ext, glu_ext, hst, a_scr, u_scr, h_scr,
    *, n_seq, n_t,
):
    t = pl.program_id(1)
    rows = n_seq * n_t
    c = W_A

    @pl.when(t == 0)
    def _():
        ua_ext[:, pl.ds(0, LRU_HIST), :] = jnp.zeros((n_seq, LRU_HIST, c), F32)
        ua_ext[:, pl.ds(LRU_HIST - (LRU_CONV - 1), LRU_CONV - 1), :] = lc_ref[...]
        glu_ext[:, pl.ds(0, CONF_HIST), :] = jnp.zeros((n_seq, CONF_HIST, c), F32)
        glu_ext[:, pl.ds(CONF_HIST - (CONF_KERNEL - 1), CONF_KERNEL - 1), :] = cc_ref[...]
        hst[...] = h0_ref[...]

    ua_ext[:, pl.ds(LRU_HIST, n_t), :] = proj_ref[:, pl.ds(0, c)].reshape(n_seq, n_t, c)
    c_a = _causal_conv(ua_ext[...], wca_ref, n_t, LRU_HIST)
    c_a = (c_a + bca_ref[...][None]).reshape(rows, c)
    c_a16 = c_a.astype(BF16)
    r = jax.nn.sigmoid(jnp.dot(c_a16, wa_ref[...], preferred_element_type=F32) + ba_ref[...])
    i = jax.nn.sigmoid(jnp.dot(c_a16, wi_ref[...], preferred_element_type=F32) + bi_ref[...])
    neg_lam = -lam_ref[...]
    softplus = jnp.maximum(neg_lam, 0.0) + jnp.log1p(jnp.exp(-jnp.abs(neg_lam)))
    log_a = (-LRU_C * r) * softplus
    a = jnp.exp(log_a)
    a_scr[...] = a
    u_scr[...] = jnp.sqrt(-jnp.tanh(log_a) * (a * a + 1.0)) * (i * c_a)

    row8 = lax.broadcasted_iota(jnp.int32, (SUBLANES, c), 0)
    groups = n_t // SUBLANES
    for s in range(n_seq):
        def scan_group(g, carry, s=s):
            off = pl.multiple_of(s * n_t + g * SUBLANES, SUBLANES)
            a = a_scr[pl.ds(off, SUBLANES), :]
            u = u_scr[pl.ds(off, SUBLANES), :]
            for d in (1, 2, 4):
                keep = row8 >= d
                a_sh = jnp.where(keep, pltpu.roll(a, d, 0), 1.0)
                u_sh = jnp.where(keep, pltpu.roll(u, d, 0), 0.0)
                u = a * u_sh + u
                a = a * a_sh
            h = a * carry + u
            h_scr[pl.ds(off, SUBLANES), :] = h
            return jnp.broadcast_to(h[SUBLANES - 1:SUBLANES, :], (SUBLANES, c))

        carry = lax.fori_loop(0, groups, scan_group, jnp.broadcast_to(hst[s], (SUBLANES, c)))
        hst[s] = carry[0:1, :]

    y_a = jax.nn.gelu(proj_ref[:, pl.ds(c, c)]) * h_scr[...]
    y_ref[:, pl.ds(0, c)] = _rms_norm(y_a, nag_ref[...]).astype(y_ref.dtype)

    glu = proj_ref[:, pl.ds(2 * c, c)] * jax.nn.sigmoid(proj_ref[:, pl.ds(3 * c, c)])
    glu_ext[:, pl.ds(CONF_HIST, n_t), :] = glu.reshape(n_seq, n_t, c)
    c_b = _causal_conv(glu_ext[...], wcb_ref, n_t, CONF_HIST)
    c_b = (c_b + bcb_ref[...][None]).reshape(rows, c)
    mu = jnp.mean(c_b, axis=-1, keepdims=True)
    cen = c_b - mu
    var = jnp.mean(cen * cen, axis=-1, keepdims=True)
    ln = cen * lax.rsqrt(var + EPS) * lng_ref[...] + lnb_ref[...]
    y_b = ln * jax.nn.sigmoid(ln)
    y_ref[:, pl.ds(c, c)] = _rms_norm(y_b, nbg_ref[...]).astype(y_ref.dtype)

    lc_out[...] = ua_ext[:, pl.ds(LRU_HIST + n_t - (LRU_CONV - 1), LRU_CONV - 1), :]
    cc_out[...] = glu_ext[:, pl.ds(CONF_HIST + n_t - (CONF_KERNEL - 1), CONF_KERNEL - 1), :]
    h_out[...] = hst[...]
    ua_ext[:, pl.ds(0, LRU_HIST), :] = ua_ext[:, pl.ds(n_t, LRU_HIST), :]
    glu_ext[:, pl.ds(0, CONF_HIST), :] = glu_ext[:, pl.ds(n_t, CONF_HIST), :]


def _mixer(proj, row_block0, lc, h0, cc, weights, *, n_seq, n_t, n_tiles):
    b = lc.shape[0]
    c = W_A
    rows = n_seq * n_t
    n_sb = b // n_seq

    def row_map(sb, t):
        return (row_block0 + sb * n_tiles + t, 0)

    def const2(sb, t):
        return (0, 0)

    def seq_map(sb, t):
        return (sb, 0, 0)

    w_specs = [pl.BlockSpec(w.shape, const2) for w in weights]
    return pl.pallas_call(
        functools.partial(_mixer_kernel, n_seq=n_seq, n_t=n_t),
        grid=(n_sb, n_tiles),
        in_specs=[
            pl.BlockSpec((rows, 4 * c), row_map),
            pl.BlockSpec((n_seq, LRU_CONV - 1, c), seq_map),
            pl.BlockSpec((n_seq, 1, c), seq_map),
            pl.BlockSpec((n_seq, CONF_KERNEL - 1, c), seq_map),
        ] + w_specs,
        out_specs=[
            pl.BlockSpec((rows, 2 * c), lambda sb, t: (sb * n_tiles + t, 0)),
            pl.BlockSpec((n_seq, LRU_CONV - 1, c), seq_map),
            pl.BlockSpec((n_seq, 1, c), seq_map),
            pl.BlockSpec((n_seq, CONF_KERNEL - 1, c), seq_map),
        ],
        out_shape=[
            jax.ShapeDtypeStruct((b * n_tiles * n_t, 2 * c), BF16),
            jax.ShapeDtypeStruct((b, LRU_CONV - 1, c), F32),
            jax.ShapeDtypeStruct((b, 1, c), F32),
            jax.ShapeDtypeStruct((b, CONF_KERNEL - 1, c), F32),
        ],
        scratch_shapes=[
            pltpu.VMEM((n_seq, LRU_HIST + n_t, c), F32),
            pltpu.VMEM((n_seq, CONF_HIST + n_t, c), F32),
            pltpu.VMEM((n_seq, 1, c), F32),
            pltpu.VMEM((rows, c), F32),
            pltpu.VMEM((rows, c), F32),
            pltpu.VMEM((rows, c), F32),
        ],
        compiler_params=_params(("arbitrary", "arbitrary")),
        name="mixer",
    )(proj, lc, h0, cc, *weights)


def _out_proj_kernel(*refs, n_blocks):
    k = len(n_blocks)
    y_refs, x_refs = refs[:k], refs[k:2 * k]
    w_ref, g_ref, rw_ref, x1_ref, xn_ref, logit_ref = refs[2 * k:]

    def body(p):
        x1 = x_refs[p][...] + jnp.dot(y_refs[p][...], w_ref[...], preferred_element_type=F32)
        x1_ref[...] = x1
        xn = _rms_norm(x1, g_ref[...])
        xn_ref[...] = xn
        logit_ref[...] = jnp.dot(xn.astype(BF16), rw_ref[...], preferred_element_type=F32)

    _for_part(pl.program_id(0), n_blocks, body)


def _out_proj(y_parts, x_parts, w_out, g2, router_w, *, tm):
    d = w_out.shape[0]
    e = router_w.shape[1]
    n_blocks = tuple(x.shape[0] // tm for x in x_parts)
    part_specs = [
        pl.BlockSpec((tm, d), _part_map(start, nb, 0)) for start, nb in zip(_part_starts(n_blocks), n_blocks)
    ]
    n = sum(n_blocks) * tm
    const = lambda i: (0, 0)
    row = lambda i: (i, 0)
    return pl.pallas_call(
        functools.partial(_out_proj_kernel, n_blocks=n_blocks),
        grid=(sum(n_blocks),),
        in_specs=part_specs + part_specs + [
            pl.BlockSpec((d, d), const),
            pl.BlockSpec((1, d), const),
            pl.BlockSpec((d, e), const),
        ],
        out_specs=[pl.BlockSpec((tm, d), row), pl.BlockSpec((tm, d), row), pl.BlockSpec((tm, e), row)],
        out_shape=[
            jax.ShapeDtypeStruct((n, d), F32),
            jax.ShapeDtypeStruct((n, d), F32),
            jax.ShapeDtypeStruct((n, e), F32),
        ],
        compiler_params=_params(("arbitrary",)),
        name="out_proj",
    )(*y_parts, *x_parts, w_out, g2, router_w)


def _first_argmax(work, index, sentinel):
    m = jnp.max(work, axis=0, keepdims=True)
    return jnp.min(jnp.where(work == m, index, sentinel), axis=0, keepdims=True)


def _route_kernel(logit_ref, bias_ref, eidx_ref, gate_ref, rank_ref, count_ref, seen):
    n_tok = logit_ref.shape[1]

    @pl.when(pl.program_id(0) == 0)
    def _():
        seen[...] = jnp.zeros(seen.shape, F32)

    scores = jax.nn.sigmoid(logit_ref[...])
    biased = scores + bias_ref[...]
    grouped = biased.reshape(N_EXPERT_GROUPS, EXPERTS_PER_GROUP, n_tok)
    in_group = lax.broadcasted_iota(jnp.int32, grouped.shape, 1)
    top1 = jnp.max(grouped, axis=1, keepdims=True)
    first1 = jnp.min(jnp.where(grouped == top1, in_group, EXPERTS_PER_GROUP), axis=1, keepdims=True)
    top2 = jnp.max(jnp.where(in_group == first1, -jnp.inf, grouped), axis=1, keepdims=True)
    group_scores = (top1 + top2).reshape(N_EXPERT_GROUPS, n_tok)

    group_id = lax.broadcasted_iota(jnp.int32, group_scores.shape, 0)
    group_on = jnp.zeros(group_scores.shape, F32)
    work = group_scores
    for _ in range(TOPK_GROUPS):
        pick = group_id == _first_argmax(work, group_id, N_EXPERT_GROUPS)
        group_on = jnp.where(pick, 1.0, group_on)
        work = jnp.where(pick, -jnp.inf, work)

    masked = jnp.where(group_on.reshape(N_EXPERT_GROUPS, 1, n_tok) > 0.0, grouped, -jnp.inf)
    work = masked.reshape(N_EXPERTS, n_tok)
    expert_id = lax.broadcasted_iota(jnp.int32, work.shape, 0)
    ids, sel = [], []
    chosen = jnp.zeros(work.shape, F32)
    for _ in range(TOP_K):
        first = _first_argmax(work, expert_id, N_EXPERTS)
        pick = expert_id == first
        ids.append(first)
        sel.append(jnp.sum(jnp.where(pick, scores, 0.0), axis=0, keepdims=True))
        chosen = jnp.where(pick, 1.0, chosen)
        work = jnp.where(pick, -jnp.inf, work)
    sel = jnp.concatenate(sel, axis=0)
    eidx_ref[...] = jnp.concatenate(ids, axis=0)
    gate_ref[...] = sel / (jnp.sum(sel, axis=0, keepdims=True) + 1e-20) * ROUTED_SCALE

    earlier = lax.broadcasted_iota(jnp.int32, (n_tok, n_tok), 0) < lax.broadcasted_iota(jnp.int32, (n_tok, n_tok), 1)
    before = seen[...] + jnp.dot(chosen.astype(BF16), earlier.astype(BF16), preferred_element_type=F32)
    rank_ref[...] = jnp.concatenate(
        [jnp.sum(jnp.where(expert_id == ids[k], before, 0.0), axis=0, keepdims=True) for k in range(TOP_K)], axis=0
    ).astype(jnp.int32)
    seen[...] += jnp.sum(chosen, axis=1, keepdims=True)
    count_ref[...] = seen[...].astype(jnp.int32)


def _route(logits_t, bias, *, tl):
    e, n = logits_t.shape
    pick_spec = pl.BlockSpec((TOP_K, tl), lambda i: (0, i))
    return pl.pallas_call(
        _route_kernel,
        grid=(n // tl,),
        in_specs=[pl.BlockSpec((e, tl), lambda i: (0, i)), pl.BlockSpec((e, 1), lambda i: (0, 0))],
        out_specs=[pick_spec, pick_spec, pick_spec, pl.BlockSpec((e, 1), lambda i: (0, 0))],
        out_shape=[
            jax.ShapeDtypeStruct((TOP_K, n), jnp.int32),
            jax.ShapeDtypeStruct((TOP_K, n), F32),
            jax.ShapeDtypeStruct((TOP_K, n), jnp.int32),
            jax.ShapeDtypeStruct((e, 1), jnp.int32),
        ],
        scratch_shapes=[pltpu.VMEM((e, 1), F32)],
        compiler_params=_params(("arbitrary",)),
        name="route",
    )(logits_t, bias)


def _dispatch_kernel(dest_ref, x_ref, xs_hbm, sem):
    tm = x_ref.shape[0]

    def issue(n, carry):
        for k in range(TOP_K):
            pltpu.make_async_copy(
                x_ref.at[pl.ds(n, 1), :], xs_hbm.at[pl.ds(dest_ref[0, 0, n * TOP_K + k], 1), :], sem
            ).start()
        return carry

    lax.fori_loop(0, tm, issue, 0)
    for _ in range(TOP_K):
        pltpu.make_async_copy(x_ref, xs_hbm.at[pl.ds(0, tm), :], sem).wait()


def _dispatch(dest, xn, *, tm):
    n, d = xn.shape
    return pl.pallas_call(
        _dispatch_kernel,
        grid=(n // tm,),
        in_specs=[
            pl.BlockSpec((1, 1, tm * TOP_K), lambda i: (i, 0, 0), memory_space=pltpu.SMEM),
            pl.BlockSpec((tm, d), lambda i: (i, 0)),
        ],
        out_specs=pl.BlockSpec(memory_space=pl.ANY),
        out_shape=jax.ShapeDtypeStruct((n * TOP_K, d), F32),
        scratch_shapes=[pltpu.SemaphoreType.DMA],
        compiler_params=_params(("arbitrary",)),
        name="dispatch",
    )(dest.reshape(n // tm, 1, tm * TOP_K), xn)


def _experts_kernel(tile_ref, expert_ref, lo_ref, hi_ref, first_ref, xs_ref, wg_ref, wu_ref, wd_ref, ys_ref):
    it = pl.program_id(0)
    tm = xs_ref.shape[0]

    @pl.when(first_ref[it] == 1)
    def _():
        ys_ref[...] = jnp.zeros(ys_ref.shape, F32)

    lo = lo_ref[it]
    hi = hi_ref[it]

    @pl.when(lo < hi)
    def _():
        x = xs_ref[...].astype(BF16)
        g = jnp.dot(x, wg_ref[0].astype(BF16), preferred_element_type=F32)
        u = jnp.dot(x, wu_ref[0].astype(BF16), preferred_element_type=F32)
        row = tile_ref[it] * tm + lax.broadcasted_iota(jnp.int32, (tm, 1), 0)
        h = jnp.where((row >= lo) & (row < hi), g * jax.nn.sigmoid(g) * u, 0.0)
        ys_ref[...] += jnp.dot(h.astype(BF16), wd_ref[0].astype(BF16), preferred_element_type=F32)


def _experts(items, xs, w_gate, w_up, w_down, *, tm):
    p, d = xs.shape
    f = w_gate.shape[2]
    n_items = items[0].shape[0]
    row_map = lambda i, tile, expert, lo, hi, first: (tile[i], 0)
    w_map = lambda i, tile, expert, lo, hi, first: (expert[i], 0, 0)
    return pl.pallas_call(
        _experts_kernel,
        grid_spec=pltpu.PrefetchScalarGridSpec(
            num_scalar_prefetch=5,
            grid=(n_items,),
            in_specs=[
                pl.BlockSpec((tm, d), row_map),
                pl.BlockSpec((1, d, f), w_map),
                pl.BlockSpec((1, d, f), w_map),
                pl.BlockSpec((1, f, d), w_map),
            ],
            out_specs=pl.BlockSpec((tm, d), row_map),
        ),
        out_shape=jax.ShapeDtypeStruct((p, d), F32),
        compiler_params=_params(("arbitrary",)),
        name="experts",
    )(*items, xs, w_gate, w_up, w_down)


def _combine_kernel(dest_ref, gate_ref, x1_ref, xn_ref, sg_ref, su_ref, sd_ref, fg_ref, ys_hbm, *rest, n_blocks):
    o_refs = rest[:len(n_blocks)]
    buf, sem = rest[len(n_blocks):]
    tm = x1_ref.shape[0]

    def issue(n, carry):
        for k in range(TOP_K):
            pltpu.make_async_copy(
                ys_hbm.at[pl.ds(dest_ref[0, 0, n * TOP_K + k], 1), :], buf.at[k, pl.ds(n, 1), :], sem
            ).start()
        return carry

    lax.fori_loop(0, tm, issue, 0)

    xn = xn_ref[...].astype(BF16)
    sg = jnp.dot(xn, sg_ref[...], preferred_element_type=F32)
    su = jnp.dot(xn, su_ref[...], preferred_element_type=F32)
    acc = x1_ref[...] + jnp.dot(
        (sg * jax.nn.sigmoid(sg) * su).astype(BF16), sd_ref[...], preferred_element_type=F32
    )

    for k in range(TOP_K):
        pltpu.make_async_copy(ys_hbm.at[pl.ds(0, tm), :], buf.at[k], sem).wait()
    gates = gate_ref[...]
    for k in range(TOP_K):
        acc = acc + gates[:, k:k + 1] * buf[k]
    out = _rms_norm(acc, fg_ref[...])

    def body(p):
        o_refs[p][...] = out

    _for_part(pl.program_id(0), n_blocks, body)


def _combine(dest, gates, x1, xn, sh_gate, sh_up, sh_down, final_g, ys, part_rows, *, tm):
    n, d = x1.shape
    f = sh_gate.shape[1]
    n_blocks = tuple(r // tm for r in part_rows)
    const = lambda i: (0, 0)
    row = lambda i: (i, 0)
    return pl.pallas_call(
        functools.partial(_combine_kernel, n_blocks=n_blocks),
        grid=(n // tm,),
        in_specs=[
            pl.BlockSpec((1, 1, tm * TOP_K), lambda i: (i, 0, 0), memory_space=pltpu.SMEM),
            pl.BlockSpec((tm, TOP_K), row),
            pl.BlockSpec((tm, d), row),
            pl.BlockSpec((tm, d), row),
            pl.BlockSpec((d, f), const),
            pl.BlockSpec((d, f), const),
            pl.BlockSpec((f, d), const),
            pl.BlockSpec((1, d), const),
            pl.BlockSpec(memory_space=pl.ANY),
        ],
        out_specs=[
            pl.BlockSpec((tm, d), _part_map(start, nb, 0)) for start, nb in zip(_part_starts(n_blocks), n_blocks)
        ],
        out_shape=[jax.ShapeDtypeStruct((r, d), F32) for r in part_rows],
        scratch_shapes=[pltpu.VMEM((TOP_K, tm, d), F32), pltpu.SemaphoreType.DMA],
        compiler_params=_params(("arbitrary",)),
        name="combine",
    )(dest.reshape(n // tm, 1, tm * TOP_K), gates, x1, xn, sh_gate, sh_up, sh_down, final_g, ys)


def _work_items(counts, n_rows, *, tm):
    n_tiles = n_rows // tm
    offsets = jnp.concatenate([jnp.zeros((1,), jnp.int32), jnp.cumsum(counts).astype(jnp.int32)])
    tile_bounds = jnp.arange(1, n_tiles, dtype=jnp.int32) * tm
    offset_rank = jnp.arange(N_EXPERTS + 1, dtype=jnp.int32) + jnp.clip((offsets + tm - 1) // tm - 1, 0, n_tiles - 1)
    tile_rank = jnp.arange(n_tiles - 1, dtype=jnp.int32) + jnp.sum(
        (offsets[None, :] <= tile_bounds[:, None]).astype(jnp.int32), axis=1
    )
    values = jnp.concatenate([offsets, tile_bounds])
    ranks = jnp.concatenate([offset_rank, tile_rank])
    slots = jnp.arange(values.shape[0], dtype=jnp.int32)
    bounds = jnp.sum(jnp.where(ranks[None, :] == slots[:, None], values[None, :], 0), axis=1)
    lo, hi = bounds[:-1], bounds[1:]
    tile = jnp.minimum(lo // tm, n_tiles - 1)
    expert = jnp.clip(
        jnp.sum((offsets[None, :] <= lo[:, None]).astype(jnp.int32), axis=1) - 1, 0, N_EXPERTS - 1
    )
    first = jnp.concatenate([jnp.ones((1,), jnp.int32), (tile[1:] != tile[:-1]).astype(jnp.int32)])
    return offsets, (tile, expert, lo, hi, first)


def _block_diag(w):
    h, dh, _ = w.shape
    eye = jnp.eye(h, dtype=w.dtype)
    return (eye[:, None, :, None] * w[:, :, None, :]).reshape(h * dh, h * dh)


ROW_TM = 512
OUT_PROJ_TM = 256
IN_PROJ_TN = 1024
PROMPT_T = 256
SAMPLE_SEQS = 16
ROUTE_TL = 256
EXPERT_TM = 256
COMBINE_TM = 128


def kernel(x_prompt, x_sample, state_lru_conv, state_lru_h, state_conf_conv, meta_tokens, norm1_g, w_in, lru_conv_w, lru_conv_b, lru_wa, lru_ba, lru_wi, lru_bi, lru_lambda, conf_conv_w, conf_conv_b, conf_ln_g, conf_ln_b, out_norm_a, out_norm_b, w_out, norm2_g, router_w, router_bias, exp_w_gate, exp_w_up, exp_w_down, sh_w_gate, sh_w_up, sh_w_down, final_norm_g):
    b_p, seq, d = x_prompt.shape
    b_s, t_s, _ = x_sample.shape
    n_p = b_p * seq
    n_s = b_s * t_s
    n = n_p + n_s
    x_parts = (x_prompt.reshape(n_p, d), x_sample.reshape(n_s, d))

    row = lambda v: v.reshape(1, -1)
    w_in16 = w_in[0].astype(BF16)
    mixer_w = (
        lru_conv_w[0], row(lru_conv_b[0]),
        _block_diag(lru_wa[0]).astype(BF16), row(lru_ba[0]),
        _block_diag(lru_wi[0]).astype(BF16), row(lru_bi[0]),
        row(lru_lambda[0]),
        conf_conv_w[0], row(conf_conv_b[0]), row(conf_ln_g[0]), row(conf_ln_b[0]),
        row(out_norm_a[0]), row(out_norm_b[0]),
    )

    proj_m = _in_proj((meta_tokens,), row(norm1_g[0]), w_in16, tm=N_META, tn=IN_PROJ_TN)
    _, m_lc, m_h, m_cc = _mixer(
        proj_m, 0,
        jnp.zeros((1, LRU_CONV - 1, W_A), F32), jnp.zeros((1, 1, W_A), F32), jnp.zeros((1, CONF_KERNEL - 1, W_B), F32),
        mixer_w, n_seq=1, n_t=N_META, n_tiles=1,
    )

    proj = _in_proj(x_parts, row(norm1_g[0]), w_in16, tm=ROW_TM, tn=IN_PROJ_TN)
    y_p, p_lc, p_h, p_cc = _mixer(
        proj, 0,
        jnp.broadcast_to(m_lc, (b_p,) + m_lc.shape[1:]), jnp.broadcast_to(m_h, (b_p,) + m_h.shape[1:]),
        jnp.broadcast_to(m_cc, (b_p,) + m_cc.shape[1:]),
        mixer_w, n_seq=1, n_t=PROMPT_T, n_tiles=seq // PROMPT_T,
    )
    y_s, s_lc, s_h, s_cc = _mixer(
        proj, n_p // (SAMPLE_SEQS * t_s),
        state_lru_conv[0], state_lru_h[0].reshape(b_s, 1, W_A), state_conf_conv[0],
        mixer_w, n_seq=SAMPLE_SEQS, n_t=t_s, n_tiles=1,
    )

    x1, xn2, logits = _out_proj(
        (y_p, y_s), x_parts, w_out[0].astype(BF16), row(norm2_g[0]), router_w[0].astype(BF16), tm=OUT_PROJ_TM
    )

    eidx_t, gate_t, rank_t, counts = _route(logits.T, router_bias[0].reshape(N_EXPERTS, 1), tl=ROUTE_TL)
    offsets, items = _work_items(counts[:, 0], n * TOP_K, tm=EXPERT_TM)
    expert_one_hot = eidx_t[:, :, None] == jnp.arange(N_EXPERTS, dtype=jnp.int32)
    dest = (jnp.sum(jnp.where(expert_one_hot, offsets[:N_EXPERTS], 0), axis=2) + rank_t).T

    xs = _dispatch(dest, xn2, tm=ROW_TM)
    ys = _experts(items, xs, exp_w_gate[0], exp_w_up[0], exp_w_down[0], tm=EXPERT_TM)
    out_p, out_s = _combine(
        dest, gate_t.T, x1, xn2,
        sh_w_gate[0].astype(BF16), sh_w_up[0].astype(BF16), sh_w_down[0].astype(BF16),
        row(final_norm_g), ys, (n_p, n_s), tm=COMBINE_TM,
    )

    return (
        out_p.reshape(b_p, seq, d), out_s.reshape(b_s, t_s, d),
        p_lc[None], p_h.reshape(1, b_p, W_A), p_cc[None],
        s_lc[None], s_h.reshape(1, b_s, W_A), s_cc[None],
    )
```

```python
import functools

import jax
import jax.numpy as jnp
from jax import lax
from jax.experimental import pallas as pl
from jax.experimental.pallas import tpu as pltpu

D_MODEL = 2048
N_META = 16
W_A = 1024
W_B = 1024
LRU_CONV = 4
LRU_C = 8.0
CONF_KERNEL = 31
N_EXPERTS = 64
N_EXPERT_GROUPS = 8
EXPERTS_PER_GROUP = 8
TOPK_GROUPS = 4
TOP_K = 8
ROUTED_SCALE = 2.5
EPS = 1e-6

SUBLANES = 8
LANES = 128
TOKEN_BLOCK = 256
SEG_ROWS = SUBLANES
LOCAL_ROWS = 2560
SORT_CHUNK = 512
HOT_PIECE = 16
TAB_WIDTH = 3 * N_EXPERTS
assert LOCAL_ROWS >= TOKEN_BLOCK * TOP_K + N_EXPERTS * (SEG_ROWS - 1) and LOCAL_ROWS % SORT_CHUNK == 0
LRU_HIST = SUBLANES
CONF_HIST = 32
VMEM_LIMIT = 56 * 1024 * 1024

F32 = jnp.float32
BF16 = jnp.bfloat16


def _params(semantics):
    return pltpu.CompilerParams(dimension_semantics=semantics, vmem_limit_bytes=VMEM_LIMIT)


def _rms_norm(x, g):
    return x * lax.rsqrt(jnp.mean(x * x, axis=-1, keepdims=True) + EPS) * g


def _part_starts(n_blocks):
    starts, s = [], 0
    for nb in n_blocks:
        starts.append(s)
        s += nb
    return starts


def _part_map(start, nb, grid_axis):
    def index_map(*ids):
        return (jnp.clip(ids[grid_axis] - start, 0, nb - 1), 0)

    return index_map


def _for_part(i, n_blocks, body):
    for p, (start, nb) in enumerate(zip(_part_starts(n_blocks), n_blocks)):
        pl.when((i >= start) & (i < start + nb))(functools.partial(body, p))


def _in_proj_kernel(*refs, n_blocks):
    x_refs = refs[:len(n_blocks)]
    g_ref, w_ref, o_ref = refs[len(n_blocks):]

    def body(p):
        xn = _rms_norm(x_refs[p][...], g_ref[...])
        o_ref[...] = jnp.dot(xn.astype(BF16), w_ref[...], preferred_element_type=F32)

    _for_part(pl.program_id(1), n_blocks, body)


def _in_proj(x_parts, g, w, *, tm, tn):
    d, f = w.shape
    n_blocks = tuple(x.shape[0] // tm for x in x_parts)
    x_specs = [
        pl.BlockSpec((tm, d), _part_map(start, nb, 1)) for start, nb in zip(_part_starts(n_blocks), n_blocks)
    ]
    return pl.pallas_call(
        functools.partial(_in_proj_kernel, n_blocks=n_blocks),
        grid=(f // tn, sum(n_blocks)),
        in_specs=x_specs + [
            pl.BlockSpec((1, d), lambda j, i: (0, 0)),
            pl.BlockSpec((d, tn), lambda j, i: (0, j)),
        ],
        out_specs=pl.BlockSpec((tm, tn), lambda j, i: (i, j)),
        out_shape=jax.ShapeDtypeStruct((sum(n_blocks) * tm, f), F32),
        compiler_params=_params(("arbitrary", "arbitrary")),
        name="in_proj",
    )(*x_parts, g, w)


def _causal_conv(ext, w_ref, n_t, hist):
    taps = w_ref.shape[0]
    length = ext.shape[1]
    shifted = {0: ext}
    acc = None
    for j in range(taps):
        q, r = divmod(hist - (taps - 1) + j, SUBLANES)
        if r not in shifted:
            shifted[r] = pltpu.roll(ext, length - r, 1)
        term = w_ref[pl.ds(j, 1), :][None] * shifted[r][:, q * SUBLANES:q * SUBLANES + n_t, :]
        acc = term if acc is None else acc + term
    return acc


def _mixer_kernel(
    proj_ref, lc_ref, h0_ref, cc_ref,
    wca_ref, bca_ref, wa_ref, ba_ref, wi_ref, bi_ref, lam_ref,
    wcb_ref, bcb_ref, lng_ref, lnb_ref, nag_ref, nbg_ref,
    y_ref, lc_out, h_out, cc_out,
    ua_ext, glu_ext, hst, a_scr, u_scr, h_scr,
    *, n_seq, n_t,
):
    t = pl.program_id(1)
    rows = n_seq * n_t
    c = W_A

    @pl.when(t == 0)
    def _():
        ua_ext[:, pl.ds(0, LRU_HIST), :] = jnp.zeros((n_seq, LRU_HIST, c), F32)
        ua_ext[:, pl.ds(LRU_HIST - (LRU_CONV - 1), LRU_CONV - 1), :] = lc_ref[...]
        glu_ext[:, pl.ds(0, CONF_HIST), :] = jnp.zeros((n_seq, CONF_HIST, c), F32)
        glu_ext[:, pl.ds(CONF_HIST - (CONF_KERNEL - 1), CONF_KERNEL - 1), :] = cc_ref[...]
        hst[...] = h0_ref[...]

    ua_ext[:, pl.ds(LRU_HIST, n_t), :] = proj_ref[:, pl.ds(0, c)].reshape(n_seq, n_t, c)
    c_a = _causal_conv(ua_ext[...], wca_ref, n_t, LRU_HIST)
    c_a = (c_a + bca_ref[...][None]).reshape(rows, c)
    c_a16 = c_a.astype(BF16)
    r = jax.nn.sigmoid(jnp.dot(c_a16, wa_ref[...], preferred_element_type=F32) + ba_ref[...])
    i = jax.nn.sigmoid(jnp.dot(c_a16, wi_ref[...], preferred_element_type=F32) + bi_ref[...])
    neg_lam = -lam_ref[...]
    softplus = jnp.maximum(neg_lam, 0.0) + jnp.log1p(jnp.exp(-jnp.abs(neg_lam)))
    log_a = (-LRU_C * r) * softplus
    a = jnp.exp(log_a)
    a_scr[...] = a
    u_scr[...] = jnp.sqrt(-jnp.tanh(log_a) * (a * a + 1.0)) * (i * c_a)

    row8 = lax.broadcasted_iota(jnp.int32, (SUBLANES, c), 0)
    groups = n_t // SUBLANES
    for s in range(n_seq):
        def scan_group(g, carry, s=s):
            off = pl.multiple_of(s * n_t + g * SUBLANES, SUBLANES)
            a = a_scr[pl.ds(off, SUBLANES), :]
            u = u_scr[pl.ds(off, SUBLANES), :]
            for d in (1, 2, 4):
                keep = row8 >= d
                a_sh = jnp.where(keep, pltpu.roll(a, d, 0), 1.0)
                u_sh = jnp.where(keep, pltpu.roll(u, d, 0), 0.0)
                u = a * u_sh + u
                a = a * a_sh
            h = a * carry + u
            h_scr[pl.ds(off, SUBLANES), :] = h
            return jnp.broadcast_to(h[SUBLANES - 1:SUBLANES, :], (SUBLANES, c))

        carry = lax.fori_loop(0, groups, scan_group, jnp.broadcast_to(hst[s], (SUBLANES, c)))
        hst[s] = carry[0:1, :]

    y_a = jax.nn.gelu(proj_ref[:, pl.ds(c, c)]) * h_scr[...]
    y_ref[:, pl.ds(0, c)] = _rms_norm(y_a, nag_ref[...]).astype(y_ref.dtype)

    glu = proj_ref[:, pl.ds(2 * c, c)] * jax.nn.sigmoid(proj_ref[:, pl.ds(3 * c, c)])
    glu_ext[:, pl.ds(CONF_HIST, n_t), :] = glu.reshape(n_seq, n_t, c)
    c_b = _causal_conv(glu_ext[...], wcb_ref, n_t, CONF_HIST)
    c_b = (c_b + bcb_ref[...][None]).reshape(rows, c)
    mu = jnp.mean(c_b, axis=-1, keepdims=True)
    cen = c_b - mu
    var = jnp.mean(cen * cen, axis=-1, keepdims=True)
    ln = cen * lax.rsqrt(var + EPS) * lng_ref[...] + lnb_ref[...]
    y_b = ln * jax.nn.sigmoid(ln)
    y_ref[:, pl.ds(c, c)] = _rms_norm(y_b, nbg_ref[...]).astype(y_ref.dtype)

    lc_out[...] = ua_ext[:, pl.ds(LRU_HIST + n_t - (LRU_CONV - 1), LRU_CONV - 1), :]
    cc_out[...] = glu_ext[:, pl.ds(CONF_HIST + n_t - (CONF_KERNEL - 1), CONF_KERNEL - 1), :]
    h_out[...] = hst[...]
    ua_ext[:, pl.ds(0, LRU_HIST), :] = ua_ext[:, pl.ds(n_t, LRU_HIST), :]
    glu_ext[:, pl.ds(0, CONF_HIST), :] = glu_ext[:, pl.ds(n_t, CONF_HIST), :]


def _mixer(proj, row_block0, lc, h0, cc, weights, *, n_seq, n_t, n_tiles):
    b = lc.shape[0]
    c = W_A
    rows = n_seq * n_t
    n_sb = b // n_seq

    def row_map(sb, t):
        return (row_block0 + sb * n_tiles + t, 0)

    def const2(sb, t):
        return (0, 0)

    def seq_map(sb, t):
        return (sb, 0, 0)

    w_specs = [pl.BlockSpec(w.shape, const2) for w in weights]
    return pl.pallas_call(
        functools.partial(_mixer_kernel, n_seq=n_seq, n_t=n_t),
        grid=(n_sb, n_tiles),
        in_specs=[
            pl.BlockSpec((rows, 4 * c), row_map),
            pl.BlockSpec((n_seq, LRU_CONV - 1, c), seq_map),
            pl.BlockSpec((n_seq, 1, c), seq_map),
            pl.BlockSpec((n_seq, CONF_KERNEL - 1, c), seq_map),
        ] + w_specs,
        out_specs=[
            pl.BlockSpec((rows, 2 * c), lambda sb, t: (sb * n_tiles + t, 0)),
            pl.BlockSpec((n_seq, LRU_CONV - 1, c), seq_map),
            pl.BlockSpec((n_seq, 1, c), seq_map),
            pl.BlockSpec((n_seq, CONF_KERNEL - 1, c), seq_map),
        ],
        out_shape=[
            jax.ShapeDtypeStruct((b * n_tiles * n_t, 2 * c), BF16),
            jax.ShapeDtypeStruct((b, LRU_CONV - 1, c), F32),
            jax.ShapeDtypeStruct((b, 1, c), F32),
            jax.ShapeDtypeStruct((b, CONF_KERNEL - 1, c), F32),
        ],
        scratch_shapes=[
            pltpu.VMEM((n_seq, LRU_HIST + n_t, c), F32),
            pltpu.VMEM((n_seq, CONF_HIST + n_t, c), F32),
            pltpu.VMEM((n_seq, 1, c), F32),
            pltpu.VMEM((rows, c), F32),
            pltpu.VMEM((rows, c), F32),
            pltpu.VMEM((rows, c), F32),
        ],
        compiler_params=_params(("arbitrary", "arbitrary")),
        name="mixer",
    )(proj, lc, h0, cc, *weights)


def _out_proj_kernel(*refs, n_blocks):
    k = len(n_blocks)
    y_refs, x_refs = refs[:k], refs[k:2 * k]
    w_ref, g_ref, rw_ref, x1_ref, xn_ref, logit_ref = refs[2 * k:]

    def body(p):
        x1 = x_refs[p][...] + jnp.dot(y_refs[p][...], w_ref[...], preferred_element_type=F32)
        x1_ref[...] = x1
        xn = _rms_norm(x1, g_ref[...]).astype(BF16)
        xn_ref[...] = xn
        logit_ref[...] = jnp.dot(xn, rw_ref[...], preferred_element_type=F32)

    _for_part(pl.program_id(0), n_blocks, body)


def _out_proj(y_parts, x_parts, w_out, g2, router_w, *, tm):
    d = w_out.shape[0]
    e = router_w.shape[1]
    n_blocks = tuple(x.shape[0] // tm for x in x_parts)
    part_specs = [
        pl.BlockSpec((tm, d), _part_map(start, nb, 0)) for start, nb in zip(_part_starts(n_blocks), n_blocks)
    ]
    n = sum(n_blocks) * tm
    const = lambda i: (0, 0)
    row = lambda i: (i, 0)
    return pl.pallas_call(
        functools.partial(_out_proj_kernel, n_blocks=n_blocks),
        grid=(sum(n_blocks),),
        in_specs=part_specs + part_specs + [
            pl.BlockSpec((d, d), const),
            pl.BlockSpec((1, d), const),
            pl.BlockSpec((d, e), const),
        ],
        out_specs=[pl.BlockSpec((tm, d), row), pl.BlockSpec((tm, d), row), pl.BlockSpec((tm, e), row)],
        out_shape=[
            jax.ShapeDtypeStruct((n, d), F32),
            jax.ShapeDtypeStruct((n, d), BF16),
            jax.ShapeDtypeStruct((n, e), F32),
        ],
        compiler_params=_params(("arbitrary",)),
        name="out_proj",
    )(*y_parts, *x_parts, w_out, g2, router_w)


def _first_argmax(work, index, sentinel):
    m = jnp.max(work, axis=0, keepdims=True)
    return jnp.min(jnp.where(work == m, index, sentinel), axis=0, keepdims=True)


def _route_kernel(logit_ref, bias_ref, eidx_ref, gate_ref, rank_ref, count_ref):
    n_tok = logit_ref.shape[1]

    scores = jax.nn.sigmoid(logit_ref[...])
    biased = scores + bias_ref[...]
    grouped = biased.reshape(N_EXPERT_GROUPS, EXPERTS_PER_GROUP, n_tok)
    in_group = lax.broadcasted_iota(jnp.int32, grouped.shape, 1)
    top1 = jnp.max(grouped, axis=1, keepdims=True)
    first1 = jnp.min(jnp.where(grouped == top1, in_group, EXPERTS_PER_GROUP), axis=1, keepdims=True)
    top2 = jnp.max(jnp.where(in_group == first1, -jnp.inf, grouped), axis=1, keepdims=True)
    group_scores = (top1 + top2).reshape(N_EXPERT_GROUPS, n_tok)

    group_id = lax.broadcasted_iota(jnp.int32, group_scores.shape, 0)
    group_on = jnp.zeros(group_scores.shape, F32)
    work = group_scores
    for _ in range(TOPK_GROUPS):
        pick = group_id == _first_argmax(work, group_id, N_EXPERT_GROUPS)
        group_on = jnp.where(pick, 1.0, group_on)
        work = jnp.where(pick, -jnp.inf, work)

    masked = jnp.where(group_on.reshape(N_EXPERT_GROUPS, 1, n_tok) > 0.0, grouped, -jnp.inf)
    work = masked.reshape(N_EXPERTS, n_tok)
    expert_id = lax.broadcasted_iota(jnp.int32, work.shape, 0)
    ids, sel = [], []
    chosen = jnp.zeros(work.shape, F32)
    for _ in range(TOP_K):
        first = _first_argmax(work, expert_id, N_EXPERTS)
        pick = expert_id == first
        ids.append(first)
        sel.append(jnp.sum(jnp.where(pick, scores, 0.0), axis=0, keepdims=True))
        chosen = jnp.where(pick, 1.0, chosen)
        work = jnp.where(pick, -jnp.inf, work)
    sel = jnp.concatenate(sel, axis=0)
    eidx_ref[...] = jnp.concatenate(ids, axis=0)
    gate_ref[...] = sel / (jnp.sum(sel, axis=0, keepdims=True) + 1e-20) * ROUTED_SCALE

    earlier = lax.broadcasted_iota(jnp.int32, (n_tok, n_tok), 0) < lax.broadcasted_iota(jnp.int32, (n_tok, n_tok), 1)
    before = jnp.dot(chosen.astype(BF16), earlier.astype(BF16), preferred_element_type=F32)
    rank_ref[...] = jnp.concatenate(
        [jnp.sum(jnp.where(expert_id == ids[k], before, 0.0), axis=0, keepdims=True) for k in range(TOP_K)], axis=0
    ).astype(jnp.int32)
    count_ref[0] = jnp.sum(chosen, axis=1, keepdims=True).astype(jnp.int32)


def _route(logits_t, bias, *, tl):
    e, n = logits_t.shape
    pick_spec = pl.BlockSpec((TOP_K, tl), lambda i: (0, i))
    return pl.pallas_call(
        _route_kernel,
        grid=(n // tl,),
        in_specs=[pl.BlockSpec((e, tl), lambda i: (0, i)), pl.BlockSpec((e, 1), lambda i: (0, 0))],
        out_specs=[pick_spec, pick_spec, pick_spec, pl.BlockSpec((1, e, 1), lambda i: (i, 0, 0))],
        out_shape=[
            jax.ShapeDtypeStruct((TOP_K, n), jnp.int32),
            jax.ShapeDtypeStruct((TOP_K, n), F32),
            jax.ShapeDtypeStruct((TOP_K, n), jnp.int32),
            jax.ShapeDtypeStruct((n // tl, e, 1), jnp.int32),
        ],
        compiler_params=_params(("arbitrary",)),
        name="route",
    )(logits_t, bias)


def _gate_lanes(gate):
    g1 = gate.astype(BF16).astype(F32)
    rest = gate - g1
    g2 = rest.astype(BF16).astype(F32)
    g3 = (rest - g2).astype(BF16).astype(F32)
    lane = lax.broadcasted_iota(jnp.int32, (gate.shape[0], LANES), 1)
    return jnp.where(lane == 0, g1, jnp.where(lane == 1, g2, jnp.where(lane == 2, g3, 0.0))).astype(BF16)


def _segment_copies(tab_ref, block, copy_chunk):
    def per_expert(e, total):
        local0 = tab_ref[block, e]
        global0 = tab_ref[block, N_EXPERTS + e]
        chunks = tab_ref[block, 2 * N_EXPERTS + e]

        def per_chunk(c, carry):
            copy_chunk(
                pl.multiple_of(local0 + c * SEG_ROWS, SEG_ROWS), pl.multiple_of(global0 + c * SEG_ROWS, SEG_ROWS)
            ).start()
            return carry

        lax.fori_loop(0, chunks, per_chunk, 0)
        return total + chunks

    return lax.fori_loop(0, N_EXPERTS, per_expert, 0)


def _wait_copies(count, copy_chunk):
    def wait_one(c, carry):
        copy_chunk(0, 0).wait()
        return carry

    lax.fori_loop(0, count, wait_one, 0)


def _dispatch_kernel(tab_ref, gap_ref, pos_ref, gate_ref, x_ref, xs_hbm, sorted_scr, zero_scr, pending, sem):
    step = pl.program_id(0)
    slot = step % 2
    d = x_ref.shape[1]
    n_tok = x_ref.shape[0]
    pos = pos_ref[...]
    gate = gate_ref[...]
    x = x_ref[...]
    piece_id = lax.broadcasted_iota(jnp.int32, (HOT_PIECE, n_tok), 0)
    pos_k = [jnp.broadcast_to(pos[k:k + 1, :], (HOT_PIECE, n_tok)) for k in range(TOP_K)]
    gate_k = [jnp.broadcast_to(gate[k:k + 1, :], (HOT_PIECE, n_tok)) for k in range(TOP_K)]
    for c in range(LOCAL_ROWS // SORT_CHUNK):
        one_hot, row_gate = [], []
        for p in range(SORT_CHUNK // HOT_PIECE):
            row_id = piece_id + (c * SORT_CHUNK + p * HOT_PIECE)
            hot = jnp.zeros((HOT_PIECE, n_tok), F32)
            gate_hit = jnp.zeros((HOT_PIECE, n_tok), F32)
            for k in range(TOP_K):
                hit = row_id == pos_k[k]
                hot = jnp.where(hit, 1.0, hot)
                gate_hit = jnp.where(hit, gate_k[k], gate_hit)
            one_hot.append(hot.astype(BF16))
            row_gate.append(jnp.sum(gate_hit, axis=1, keepdims=True))
        rows = jnp.dot(jnp.concatenate(one_hot, axis=0), x, preferred_element_type=F32)
        out_rows = pl.ds(c * SORT_CHUNK, SORT_CHUNK)
        sorted_scr[slot, out_rows, pl.ds(0, d)] = rows.astype(BF16)
        sorted_scr[slot, out_rows, pl.ds(d, LANES)] = _gate_lanes(jnp.concatenate(row_gate, axis=0))

    def copy_chunk(local_row, global_row):
        return pltpu.make_async_copy(
            sorted_scr.at[slot, pl.ds(local_row, SEG_ROWS), :], xs_hbm.at[pl.ds(global_row, SEG_ROWS), :], sem
        )

    @pl.when(step > 0)
    def _():
        _wait_copies(pending[0], copy_chunk)

    pending[0] = _segment_copies(tab_ref, step, copy_chunk)

    @pl.when(step == pl.num_programs(0) - 1)
    def _():
        zero_scr[...] = jnp.zeros(zero_scr.shape, BF16)

        def zero_chunk(row):
            return pltpu.make_async_copy(
                zero_scr, xs_hbm.at[pl.ds(pl.multiple_of(row, SEG_ROWS), SEG_ROWS), :], sem
            )

        def per_gap(g, total):
            start = gap_ref[g]
            chunks = gap_ref[N_EXPERTS + 1 + g]

            def per_chunk(c, carry):
                zero_chunk(start + c * SEG_ROWS).start()
                return carry

            lax.fori_loop(0, chunks, per_chunk, 0)
            return total + chunks

        n_fill = lax.fori_loop(0, N_EXPERTS + 1, per_gap, 0)
        _wait_copies(pending[0] + n_fill, copy_chunk)


def _dispatch(tab, gaps, pos_t, gate_t, xn, n_sorted, *, tb):
    n, d = xn.shape
    width = d + LANES
    pick_spec = pl.BlockSpec((TOP_K, tb), lambda i, tab, gaps: (0, i))
    return pl.pallas_call(
        _dispatch_kernel,
        grid_spec=pltpu.PrefetchScalarGridSpec(
            num_scalar_prefetch=2,
            grid=(n // tb,),
            in_specs=[pick_spec, pick_spec, pl.BlockSpec((tb, d), lambda i, tab, gaps: (i, 0))],
            out_specs=pl.BlockSpec(memory_space=pl.ANY),
            scratch_shapes=[
                pltpu.VMEM((2, LOCAL_ROWS, width), BF16),
                pltpu.VMEM((SEG_ROWS, width), BF16),
                pltpu.SMEM((1,), jnp.int32),
                pltpu.SemaphoreType.DMA,
            ],
        ),
        out_shape=jax.ShapeDtypeStruct((n_sorted, width), BF16),
        compiler_params=_params(("arbitrary",)),
        name="dispatch",
    )(tab, gaps, pos_t, gate_t, xn)


def _experts_kernel(expert_ref, used_ref, fresh_ref, xs_ref, wg_ref, wu_ref, wd_ref, ys_ref, wg16, wu16, wd16):
    it = pl.program_id(0)
    d = ys_ref.shape[1]

    @pl.when(used_ref[it] == 0)
    def _():
        ys_ref[...] = jnp.zeros(ys_ref.shape, ys_ref.dtype)

    @pl.when(used_ref[it] == 1)
    def _():
        @pl.when(fresh_ref[it] == 1)
        def _():
            wg16[...] = wg_ref[0].astype(BF16)
            wu16[...] = wu_ref[0].astype(BF16)
            wd16[...] = wd_ref[0].astype(BF16)

        x = xs_ref[:, pl.ds(0, d)]
        gate = jnp.sum(xs_ref[:, pl.ds(d, LANES)].astype(F32), axis=1, keepdims=True)
        g = jnp.dot(x, wg16[...], preferred_element_type=F32)
        u = jnp.dot(x, wu16[...], preferred_element_type=F32)
        h = g * jax.nn.sigmoid(g) * u * gate
        ys_ref[...] = jnp.dot(h.astype(BF16), wd16[...], preferred_element_type=F32).astype(ys_ref.dtype)


def _experts(tiles, xs, w_gate, w_up, w_down, *, tm):
    p = xs.shape[0]
    d, f = w_gate.shape[1:]
    row_map = lambda i, expert, used, fresh: (i, 0)
    w_map = lambda i, expert, used, fresh: (expert[i], 0, 0)
    return pl.pallas_call(
        _experts_kernel,
        grid_spec=pltpu.PrefetchScalarGridSpec(
            num_scalar_prefetch=3,
            grid=(p // tm,),
            in_specs=[
                pl.BlockSpec((tm, xs.shape[1]), row_map),
                pl.BlockSpec((1, d, f), w_map),
                pl.BlockSpec((1, d, f), w_map),
                pl.BlockSpec((1, f, d), w_map),
            ],
            out_specs=pl.BlockSpec((tm, d), row_map),
            scratch_shapes=[pltpu.VMEM((d, f), BF16), pltpu.VMEM((d, f), BF16), pltpu.VMEM((f, d), BF16)],
        ),
        out_shape=jax.ShapeDtypeStruct((p, d), BF16),
        compiler_params=_params(("arbitrary",)),
        name="experts",
    )(*tiles, xs, w_gate, w_up, w_down)


def _combine_kernel(tab_ref, pos_ref, x1_ref, xn_ref, sg_ref, su_ref, sd_ref, fg_ref, ys_hbm, *rest, n_blocks):
    o_refs = rest[:len(n_blocks)]
    local_scr, pending, sems = rest[len(n_blocks):]
    n_tok = x1_ref.shape[0]
    step = pl.program_id(0)
    slot = step % 2

    def copy_chunk_into(dst_slot):
        def copy_chunk(local_row, global_row):
            return pltpu.make_async_copy(
                ys_hbm.at[pl.ds(global_row, SEG_ROWS), :],
                local_scr.at[dst_slot, pl.ds(local_row, SEG_ROWS), :],
                sems.at[dst_slot],
            )

        return copy_chunk

    @pl.when(step == 0)
    def _():
        local_scr[...] = jnp.zeros(local_scr.shape, local_scr.dtype)
        pending[0] = _segment_copies(tab_ref, 0, copy_chunk_into(0))

    @pl.when(step + 1 < pl.num_programs(0))
    def _():
        pending[1 - slot] = _segment_copies(tab_ref, step + 1, copy_chunk_into(1 - slot))

    xn = xn_ref[...]
    sg = jnp.dot(xn, sg_ref[...], preferred_element_type=F32)
    su = jnp.dot(xn, su_ref[...], preferred_element_type=F32)
    acc = x1_ref[...] + jnp.dot(
        (sg * jax.nn.sigmoid(sg) * su).astype(BF16), sd_ref[...], preferred_element_type=F32
    )

    piece_id = lax.broadcasted_iota(jnp.int32, (HOT_PIECE, LANES), 1)
    one_hot = []
    for r in range(n_tok // HOT_PIECE):
        pos = pos_ref[pl.ds(r * HOT_PIECE, HOT_PIECE), :]
        pos_k = [jnp.broadcast_to(pos[:, k:k + 1], (HOT_PIECE, LANES)) for k in range(TOP_K)]
        cols = []
        for c in range(LOCAL_ROWS // LANES):
            col_id = piece_id + c * LANES
            hot = jnp.zeros((HOT_PIECE, LANES), F32)
            for k in range(TOP_K):
                hot = jnp.where(col_id == pos_k[k], 1.0, hot)
            cols.append(hot.astype(BF16))
        one_hot.append(jnp.concatenate(cols, axis=1))
    one_hot = jnp.concatenate(one_hot, axis=0)

    _wait_copies(pending[slot], copy_chunk_into(slot))
    routed = jnp.dot(one_hot, local_scr[slot], preferred_element_type=F32)
    out = _rms_norm(acc + routed, fg_ref[...])

    def body(p):
        o_refs[p][...] = out

    _for_part(pl.program_id(0), n_blocks, body)


def _combine(tab, pos, x1, xn, sh_gate, sh_up, sh_down, final_g, ys, part_rows, *, tb):
    n, d = x1.shape
    f = sh_gate.shape[1]
    n_blocks = tuple(r // tb for r in part_rows)
    const = lambda i, tab: (0, 0)
    row = lambda i, tab: (i, 0)
    return pl.pallas_call(
        functools.partial(_combine_kernel, n_blocks=n_blocks),
        grid_spec=pltpu.PrefetchScalarGridSpec(
            num_scalar_prefetch=1,
            grid=(n // tb,),
            in_specs=[
                pl.BlockSpec((tb, TOP_K), row),
                pl.BlockSpec((tb, d), row),
                pl.BlockSpec((tb, d), row),
                pl.BlockSpec((d, f), const),
                pl.BlockSpec((d, f), const),
                pl.BlockSpec((f, d), const),
                pl.BlockSpec((1, d), const),
                pl.BlockSpec(memory_space=pl.ANY),
            ],
            out_specs=[
                pl.BlockSpec((tb, d), _part_map(start, nb, 0))
                for start, nb in zip(_part_starts(n_blocks), n_blocks)
            ],
            scratch_shapes=[
                pltpu.VMEM((2, LOCAL_ROWS, d), BF16),
                pltpu.SMEM((2,), jnp.int32),
                pltpu.SemaphoreType.DMA((2,)),
            ],
        ),
        out_shape=[jax.ShapeDtypeStruct((r, d), F32) for r in part_rows],
        compiler_params=_params(("arbitrary",)),
        name="combine",
    )(tab, pos, x1, xn, sh_gate, sh_up, sh_down, final_g, ys)


def _segment_tables(block_counts, n_rows, *, tm):
    padded = (block_counts + SEG_ROWS - 1) // SEG_ROWS * SEG_ROWS
    local_start = jnp.cumsum(padded, axis=1) - padded
    per_expert = jnp.sum(padded, axis=0)
    region = (per_expert + tm - 1) // tm * tm
    region_start = jnp.concatenate([jnp.zeros((1,), jnp.int32), jnp.cumsum(region).astype(jnp.int32)])
    global_start = region_start[None, :N_EXPERTS] + jnp.cumsum(padded, axis=0) - padded
    tab = jnp.concatenate([local_start, global_start, padded // SEG_ROWS], axis=1).astype(jnp.int32)

    region_end = region_start[:N_EXPERTS] + per_expert
    gap_start = jnp.concatenate([region_end, region_start[N_EXPERTS:]])
    gap_end = jnp.concatenate([region_start[1:], jnp.full((1,), n_rows, jnp.int32)])
    gaps = jnp.concatenate([gap_start, (gap_end - gap_start) // SEG_ROWS]).astype(jnp.int32)

    tile_start = jnp.arange(n_rows // tm, dtype=jnp.int32) * tm
    expert = jnp.clip(
        jnp.sum((region_start[None, :] <= tile_start[:, None]).astype(jnp.int32), axis=1) - 1, 0, N_EXPERTS - 1
    )
    tile_end = jnp.sum(jnp.where(expert[:, None] == jnp.arange(N_EXPERTS), region_end[None, :], 0), axis=1)
    used = (tile_start < tile_end).astype(jnp.int32)
    fresh = jnp.concatenate([jnp.ones((1,), jnp.int32), (expert[1:] != expert[:-1]).astype(jnp.int32)])
    return tab, gaps, (expert, used, fresh), local_start.astype(jnp.int32)


def _block_diag(w):
    h, dh, _ = w.shape
    eye = jnp.eye(h, dtype=w.dtype)
    return (eye[:, None, :, None] * w[:, :, None, :]).reshape(h * dh, h * dh)


ROW_TM = 512
OUT_PROJ_TM = 256
IN_PROJ_TN = 1024
PROMPT_T = 256
SAMPLE_SEQS = 16
EXPERT_TM = 256


def kernel(x_prompt, x_sample, state_lru_conv, state_lru_h, state_conf_conv, meta_tokens, norm1_g, w_in, lru_conv_w, lru_conv_b, lru_wa, lru_ba, lru_wi, lru_bi, lru_lambda, conf_conv_w, conf_conv_b, conf_ln_g, conf_ln_b, out_norm_a, out_norm_b, w_out, norm2_g, router_w, router_bias, exp_w_gate, exp_w_up, exp_w_down, sh_w_gate, sh_w_up, sh_w_down, final_norm_g):
    b_p, seq, d = x_prompt.shape
    b_s, t_s, _ = x_sample.shape
    n_p = b_p * seq
    n_s = b_s * t_s
    n = n_p + n_s
    x_parts = (x_prompt.reshape(n_p, d), x_sample.reshape(n_s, d))

    row = lambda v: v.reshape(1, -1)
    w_in16 = w_in[0].astype(BF16)
    mixer_w = (
        lru_conv_w[0], row(lru_conv_b[0]),
        _block_diag(lru_wa[0]).astype(BF16), row(lru_ba[0]),
        _block_diag(lru_wi[0]).astype(BF16), row(lru_bi[0]),
        row(lru_lambda[0]),
        conf_conv_w[0], row(conf_conv_b[0]), row(conf_ln_g[0]), row(conf_ln_b[0]),
        row(out_norm_a[0]), row(out_norm_b[0]),
    )

    proj_m = _in_proj((meta_tokens,), row(norm1_g[0]), w_in16, tm=N_META, tn=IN_PROJ_TN)
    _, m_lc, m_h, m_cc = _mixer(
        proj_m, 0,
        jnp.zeros((1, LRU_CONV - 1, W_A), F32), jnp.zeros((1, 1, W_A), F32), jnp.zeros((1, CONF_KERNEL - 1, W_B), F32),
        mixer_w, n_seq=1, n_t=N_META, n_tiles=1,
    )

    proj = _in_proj(x_parts, row(norm1_g[0]), w_in16, tm=ROW_TM, tn=IN_PROJ_TN)
    y_p, p_lc, p_h, p_cc = _mixer(
        proj, 0,
        jnp.broadcast_to(m_lc, (b_p,) + m_lc.shape[1:]), jnp.broadcast_to(m_h, (b_p,) + m_h.shape[1:]),
        jnp.broadcast_to(m_cc, (b_p,) + m_cc.shape[1:]),
        mixer_w, n_seq=1, n_t=PROMPT_T, n_tiles=seq // PROMPT_T,
    )
    y_s, s_lc, s_h, s_cc = _mixer(
        proj, n_p // (SAMPLE_SEQS * t_s),
        state_lru_conv[0], state_lru_h[0].reshape(b_s, 1, W_A), state_conf_conv[0],
        mixer_w, n_seq=SAMPLE_SEQS, n_t=t_s, n_tiles=1,
    )

    x1, xn2, logits = _out_proj(
        (y_p, y_s), x_parts, w_out[0].astype(BF16), row(norm2_g[0]), router_w[0].astype(BF16), tm=OUT_PROJ_TM
    )

    eidx_t, gate_t, rank_t, block_counts = _route(logits.T, router_bias[0].reshape(N_EXPERTS, 1), tl=TOKEN_BLOCK)
    n_blocks = n // TOKEN_BLOCK
    n_sorted = n_blocks * (TOKEN_BLOCK * TOP_K + N_EXPERTS * (SEG_ROWS - 1)) + N_EXPERTS * (EXPERT_TM - 1)
    n_sorted = -(-n_sorted // EXPERT_TM) * EXPERT_TM
    tab, gaps, tiles, local_start = _segment_tables(block_counts[:, :, 0], n_sorted, tm=EXPERT_TM)
    expert_one_hot = eidx_t[:, :, None] == jnp.arange(N_EXPERTS, dtype=jnp.int32)
    token_start = jnp.repeat(local_start, TOKEN_BLOCK, axis=0)
    pos_t = jnp.sum(jnp.where(expert_one_hot, token_start[None], 0), axis=2) + rank_t

    xs = _dispatch(tab, gaps, pos_t, gate_t, xn2, n_sorted, tb=TOKEN_BLOCK)
    ys = _experts(tiles, xs, exp_w_gate[0], exp_w_up[0], exp_w_down[0], tm=EXPERT_TM)
    out_p, out_s = _combine(
        tab, pos_t.T, x1, xn2,
        sh_w_gate[0].astype(BF16), sh_w_up[0].astype(BF16), sh_w_down[0].astype(BF16),
        row(final_norm_g), ys, (n_p, n_s), tb=TOKEN_BLOCK,
    )

    return (
        out_p.reshape(b_p, seq, d), out_s.reshape(b_s, t_s, d),
        p_lc[None], p_h.reshape(1, b_p, W_A), p_cc[None],
        s_lc[None], s_h.reshape(1, b_s, W_A), s_cc[None],
    )
```

```python
import functools

import jax
import jax.numpy as jnp
from jax import lax
from jax.experimental import pallas as pl
from jax.experimental.pallas import tpu as pltpu

D_MODEL = 2048
N_META = 16
W_A = 1024
W_B = 1024
LRU_CONV = 4
LRU_C = 8.0
CONF_KERNEL = 31
N_EXPERTS = 64
N_EXPERT_GROUPS = 8
EXPERTS_PER_GROUP = 8
TOPK_GROUPS = 4
TOP_K = 8
ROUTED_SCALE = 2.5
EPS = 1e-6

SUBLANES = 8
LANES = 128
TOKEN_BLOCK = 256
SEG_ROWS = SUBLANES
LOCAL_ROWS = 2560
SORT_CHUNK = 512
HOT_PIECE = 16
TAB_WIDTH = 1 + LOCAL_ROWS // SEG_ROWS
assert LOCAL_ROWS >= TOKEN_BLOCK * TOP_K + N_EXPERTS * (SEG_ROWS - 1) and LOCAL_ROWS % SORT_CHUNK == 0
LRU_HIST = SUBLANES
CONF_HIST = 32
VMEM_LIMIT = 56 * 1024 * 1024

F32 = jnp.float32
BF16 = jnp.bfloat16


def _params(semantics):
    return pltpu.CompilerParams(dimension_semantics=semantics, vmem_limit_bytes=VMEM_LIMIT)


def _rms_norm(x, g):
    return x * lax.rsqrt(jnp.mean(x * x, axis=-1, keepdims=True) + EPS) * g


def _part_starts(n_blocks):
    starts, s = [], 0
    for nb in n_blocks:
        starts.append(s)
        s += nb
    return starts


def _part_map(start, nb, grid_axis):
    def index_map(*ids):
        return (jnp.clip(ids[grid_axis] - start, 0, nb - 1), 0)

    return index_map


def _for_part(i, n_blocks, body):
    for p, (start, nb) in enumerate(zip(_part_starts(n_blocks), n_blocks)):
        pl.when((i >= start) & (i < start + nb))(functools.partial(body, p))


def _in_proj_kernel(*refs, n_blocks):
    x_refs = refs[:len(n_blocks)]
    g_ref, w_ref, o_ref = refs[len(n_blocks):]

    def body(p):
        xn = _rms_norm(x_refs[p][...], g_ref[...])
        o_ref[...] = jnp.dot(xn.astype(BF16), w_ref[...], preferred_element_type=F32)

    _for_part(pl.program_id(1), n_blocks, body)


def _in_proj(x_parts, g, w, *, tm, tn):
    d, f = w.shape
    n_blocks = tuple(x.shape[0] // tm for x in x_parts)
    x_specs = [
        pl.BlockSpec((tm, d), _part_map(start, nb, 1)) for start, nb in zip(_part_starts(n_blocks), n_blocks)
    ]
    return pl.pallas_call(
        functools.partial(_in_proj_kernel, n_blocks=n_blocks),
        grid=(f // tn, sum(n_blocks)),
        in_specs=x_specs + [
            pl.BlockSpec((1, d), lambda j, i: (0, 0)),
            pl.BlockSpec((d, tn), lambda j, i: (0, j)),
        ],
        out_specs=pl.BlockSpec((tm, tn), lambda j, i: (i, j)),
        out_shape=jax.ShapeDtypeStruct((sum(n_blocks) * tm, f), F32),
        compiler_params=_params(("arbitrary", "arbitrary")),
        name="in_proj",
    )(*x_parts, g, w)


def _causal_conv(ext, w_ref, n_t, hist):
    taps = w_ref.shape[0]
    length = ext.shape[1]
    shifted = {0: ext}
    acc = None
    for j in range(taps):
        q, r = divmod(hist - (taps - 1) + j, SUBLANES)
        if r not in shifted:
            shifted[r] = pltpu.roll(ext, length - r, 1)
        term = w_ref[pl.ds(j, 1), :][None] * shifted[r][:, q * SUBLANES:q * SUBLANES + n_t, :]
        acc = term if acc is None else acc + term
    return acc


def _mixer_kernel(
    proj_ref, lc_ref, h0_ref, cc_ref,
    wca_ref, bca_ref, wa_ref, ba_ref, wi_ref, bi_ref, lam_ref,
    wcb_ref, bcb_ref, lng_ref, lnb_ref, nag_ref, nbg_ref,
    y_ref, lc_out, h_out, cc_out,
    ua_ext, glu_ext, hst, a_scr, u_scr, h_scr,
    *, n_seq, n_t,
):
    t = pl.program_id(1)
    rows = n_seq * n_t
    c = W_A

    @pl.when(t == 0)
    def _():
        ua_ext[:, pl.ds(0, LRU_HIST), :] = jnp.zeros((n_seq, LRU_HIST, c), F32)
        ua_ext[:, pl.ds(LRU_HIST - (LRU_CONV - 1), LRU_CONV - 1), :] = lc_ref[...]
        glu_ext[:, pl.ds(0, CONF_HIST), :] = jnp.zeros((n_seq, CONF_HIST, c), F32)
        glu_ext[:, pl.ds(CONF_HIST - (CONF_KERNEL - 1), CONF_KERNEL - 1), :] = cc_ref[...]
        hst[...] = h0_ref[...]

    ua_ext[:, pl.ds(LRU_HIST, n_t), :] = proj_ref[:, pl.ds(0, c)].reshape(n_seq, n_t, c)
    c_a = _causal_conv(ua_ext[...], wca_ref, n_t, LRU_HIST)
    c_a = (c_a + bca_ref[...][None]).reshape(rows, c)
    c_a16 = c_a.astype(BF16)
    r = jax.nn.sigmoid(jnp.dot(c_a16, wa_ref[...], preferred_element_type=F32) + ba_ref[...])
    i = jax.nn.sigmoid(jnp.dot(c_a16, wi_ref[...], preferred_element_type=F32) + bi_ref[...])
    neg_lam = -lam_ref[...]
    softplus = jnp.maximum(neg_lam, 0.0) + jnp.log1p(jnp.exp(-jnp.abs(neg_lam)))
    log_a = (-LRU_C * r) * softplus
    a = jnp.exp(log_a)
    a_scr[...] = a
    u_scr[...] = jnp.sqrt(-jnp.tanh(log_a) * (a * a + 1.0)) * (i * c_a)

    row8 = lax.broadcasted_iota(jnp.int32, (SUBLANES, c), 0)
    groups = n_t // SUBLANES
    for s in range(n_seq):
        def scan_group(g, carry, s=s):
            off = pl.multiple_of(s * n_t + g * SUBLANES, SUBLANES)
            a = a_scr[pl.ds(off, SUBLANES), :]
            u = u_scr[pl.ds(off, SUBLANES), :]
            for d in (1, 2, 4):
                keep = row8 >= d
                a_sh = jnp.where(keep, pltpu.roll(a, d, 0), 1.0)
                u_sh = jnp.where(keep, pltpu.roll(u, d, 0), 0.0)
                u = a * u_sh + u
                a = a * a_sh
            h = a * carry + u
            h_scr[pl.ds(off, SUBLANES), :] = h
            return jnp.broadcast_to(h[SUBLANES - 1:SUBLANES, :], (SUBLANES, c))

        carry = lax.fori_loop(0, groups, scan_group, jnp.broadcast_to(hst[s], (SUBLANES, c)))
        hst[s] = carry[0:1, :]

    y_a = jax.nn.gelu(proj_ref[:, pl.ds(c, c)]) * h_scr[...]
    y_ref[:, pl.ds(0, c)] = _rms_norm(y_a, nag_ref[...]).astype(y_ref.dtype)

    glu = proj_ref[:, pl.ds(2 * c, c)] * jax.nn.sigmoid(proj_ref[:, pl.ds(3 * c, c)])
    glu_ext[:, pl.ds(CONF_HIST, n_t), :] = glu.reshape(n_seq, n_t, c)
    c_b = _causal_conv(glu_ext[...], wcb_ref, n_t, CONF_HIST)
    c_b = (c_b + bcb_ref[...][None]).reshape(rows, c)
    mu = jnp.mean(c_b, axis=-1, keepdims=True)
    cen = c_b - mu
    var = jnp.mean(cen * cen, axis=-1, keepdims=True)
    ln = cen * lax.rsqrt(var + EPS) * lng_ref[...] + lnb_ref[...]
    y_b = ln * jax.nn.sigmoid(ln)
    y_ref[:, pl.ds(c, c)] = _rms_norm(y_b, nbg_ref[...]).astype(y_ref.dtype)

    lc_out[...] = ua_ext[:, pl.ds(LRU_HIST + n_t - (LRU_CONV - 1), LRU_CONV - 1), :]
    cc_out[...] = glu_ext[:, pl.ds(CONF_HIST + n_t - (CONF_KERNEL - 1), CONF_KERNEL - 1), :]
    h_out[...] = hst[...]
    ua_ext[:, pl.ds(0, LRU_HIST), :] = ua_ext[:, pl.ds(n_t, LRU_HIST), :]
    glu_ext[:, pl.ds(0, CONF_HIST), :] = glu_ext[:, pl.ds(n_t, CONF_HIST), :]


def _mixer(proj, row_block0, lc, h0, cc, weights, *, n_seq, n_t, n_tiles):
    b = lc.shape[0]
    c = W_A
    rows = n_seq * n_t
    n_sb = b // n_seq

    def row_map(sb, t):
        return (row_block0 + sb * n_tiles + t, 0)

    def const2(sb, t):
        return (0, 0)

    def seq_map(sb, t):
        return (sb, 0, 0)

    w_specs = [pl.BlockSpec(w.shape, const2) for w in weights]
    return pl.pallas_call(
        functools.partial(_mixer_kernel, n_seq=n_seq, n_t=n_t),
        grid=(n_sb, n_tiles),
        in_specs=[
            pl.BlockSpec((rows, 4 * c), row_map),
            pl.BlockSpec((n_seq, LRU_CONV - 1, c), seq_map),
            pl.BlockSpec((n_seq, 1, c), seq_map),
            pl.BlockSpec((n_seq, CONF_KERNEL - 1, c), seq_map),
        ] + w_specs,
        out_specs=[
            pl.BlockSpec((rows, 2 * c), lambda sb, t: (sb * n_tiles + t, 0)),
            pl.BlockSpec((n_seq, LRU_CONV - 1, c), seq_map),
            pl.BlockSpec((n_seq, 1, c), seq_map),
            pl.BlockSpec((n_seq, CONF_KERNEL - 1, c), seq_map),
        ],
        out_shape=[
            jax.ShapeDtypeStruct((b * n_tiles * n_t, 2 * c), BF16),
            jax.ShapeDtypeStruct((b, LRU_CONV - 1, c), F32),
            jax.ShapeDtypeStruct((b, 1, c), F32),
            jax.ShapeDtypeStruct((b, CONF_KERNEL - 1, c), F32),
        ],
        scratch_shapes=[
            pltpu.VMEM((n_seq, LRU_HIST + n_t, c), F32),
            pltpu.VMEM((n_seq, CONF_HIST + n_t, c), F32),
            pltpu.VMEM((n_seq, 1, c), F32),
            pltpu.VMEM((rows, c), F32),
            pltpu.VMEM((rows, c), F32),
            pltpu.VMEM((rows, c), F32),
        ],
        compiler_params=_params(("arbitrary", "arbitrary")),
        name="mixer",
    )(proj, lc, h0, cc, *weights)


def _out_proj_kernel(*refs, n_blocks):
    k = len(n_blocks)
    y_refs, x_refs = refs[:k], refs[k:2 * k]
    w_ref, g_ref, rw_ref, x1_ref, xn_ref, logit_ref = refs[2 * k:]

    def body(p):
        x1 = x_refs[p][...] + jnp.dot(y_refs[p][...], w_ref[...], preferred_element_type=F32)
        x1_ref[...] = x1
        xn = _rms_norm(x1, g_ref[...]).astype(BF16)
        xn_ref[...] = xn
        logit_ref[...] = jnp.dot(xn, rw_ref[...], preferred_element_type=F32)

    _for_part(pl.program_id(0), n_blocks, body)


def _out_proj(y_parts, x_parts, w_out, g2, router_w, *, tm):
    d = w_out.shape[0]
    e = router_w.shape[1]
    n_blocks = tuple(x.shape[0] // tm for x in x_parts)
    part_specs = [
        pl.BlockSpec((tm, d), _part_map(start, nb, 0)) for start, nb in zip(_part_starts(n_blocks), n_blocks)
    ]
    n = sum(n_blocks) * tm
    const = lambda i: (0, 0)
    row = lambda i: (i, 0)
    return pl.pallas_call(
        functools.partial(_out_proj_kernel, n_blocks=n_blocks),
        grid=(sum(n_blocks),),
        in_specs=part_specs + part_specs + [
            pl.BlockSpec((d, d), const),
            pl.BlockSpec((1, d), const),
            pl.BlockSpec((d, e), const),
        ],
        out_specs=[pl.BlockSpec((tm, d), row), pl.BlockSpec((tm, d), row), pl.BlockSpec((tm, e), row)],
        out_shape=[
            jax.ShapeDtypeStruct((n, d), F32),
            jax.ShapeDtypeStruct((n, d), BF16),
            jax.ShapeDtypeStruct((n, e), F32),
        ],
        compiler_params=_params(("arbitrary",)),
        name="out_proj",
    )(*y_parts, *x_parts, w_out, g2, router_w)


def _first_argmax(work, index, sentinel):
    m = jnp.max(work, axis=0, keepdims=True)
    return jnp.min(jnp.where(work == m, index, sentinel), axis=0, keepdims=True)


def _route_kernel(logit_ref, bias_ref, eidx_ref, gate_ref, rank_ref, count_ref):
    n_tok = logit_ref.shape[1]

    scores = jax.nn.sigmoid(logit_ref[...])
    biased = scores + bias_ref[...]
    grouped = biased.reshape(N_EXPERT_GROUPS, EXPERTS_PER_GROUP, n_tok)
    in_group = lax.broadcasted_iota(jnp.int32, grouped.shape, 1)
    top1 = jnp.max(grouped, axis=1, keepdims=True)
    first1 = jnp.min(jnp.where(grouped == top1, in_group, EXPERTS_PER_GROUP), axis=1, keepdims=True)
    top2 = jnp.max(jnp.where(in_group == first1, -jnp.inf, grouped), axis=1, keepdims=True)
    group_scores = (top1 + top2).reshape(N_EXPERT_GROUPS, n_tok)

    group_id = lax.broadcasted_iota(jnp.int32, group_scores.shape, 0)
    group_on = jnp.zeros(group_scores.shape, F32)
    work = group_scores
    for _ in range(TOPK_GROUPS):
        pick = group_id == _first_argmax(work, group_id, N_EXPERT_GROUPS)
        group_on = jnp.where(pick, 1.0, group_on)
        work = jnp.where(pick, -jnp.inf, work)

    masked = jnp.where(group_on.reshape(N_EXPERT_GROUPS, 1, n_tok) > 0.0, grouped, -jnp.inf)
    work = masked.reshape(N_EXPERTS, n_tok)
    expert_id = lax.broadcasted_iota(jnp.int32, work.shape, 0)
    ids, sel = [], []
    chosen = jnp.zeros(work.shape, F32)
    for _ in range(TOP_K):
        first = _first_argmax(work, expert_id, N_EXPERTS)
        pick = expert_id == first
        ids.append(first)
        sel.append(jnp.sum(jnp.where(pick, scores, 0.0), axis=0, keepdims=True))
        chosen = jnp.where(pick, 1.0, chosen)
        work = jnp.where(pick, -jnp.inf, work)
    sel = jnp.concatenate(sel, axis=0)
    eidx_ref[...] = jnp.concatenate(ids, axis=0)
    gate_ref[...] = sel / (jnp.sum(sel, axis=0, keepdims=True) + 1e-20) * ROUTED_SCALE

    earlier = lax.broadcasted_iota(jnp.int32, (n_tok, n_tok), 0) < lax.broadcasted_iota(jnp.int32, (n_tok, n_tok), 1)
    before = jnp.dot(chosen.astype(BF16), earlier.astype(BF16), preferred_element_type=F32)
    rank_ref[...] = jnp.concatenate(
        [jnp.sum(jnp.where(expert_id == ids[k], before, 0.0), axis=0, keepdims=True) for k in range(TOP_K)], axis=0
    ).astype(jnp.int32)
    count_ref[0] = jnp.sum(chosen, axis=1, keepdims=True).astype(jnp.int32)


def _route(logits_t, bias, *, tl):
    e, n = logits_t.shape
    pick_spec = pl.BlockSpec((TOP_K, tl), lambda i: (0, i))
    return pl.pallas_call(
        _route_kernel,
        grid=(n // tl,),
        in_specs=[pl.BlockSpec((e, tl), lambda i: (0, i)), pl.BlockSpec((e, 1), lambda i: (0, 0))],
        out_specs=[pick_spec, pick_spec, pick_spec, pl.BlockSpec((1, e, 1), lambda i: (i, 0, 0))],
        out_shape=[
            jax.ShapeDtypeStruct((TOP_K, n), jnp.int32),
            jax.ShapeDtypeStruct((TOP_K, n), F32),
            jax.ShapeDtypeStruct((TOP_K, n), jnp.int32),
            jax.ShapeDtypeStruct((n // tl, e, 1), jnp.int32),
        ],
        compiler_params=_params(("arbitrary",)),
        name="route",
    )(logits_t, bias)


def _gate_lanes(gate):
    g1 = gate.astype(BF16).astype(F32)
    rest = gate - g1
    g2 = rest.astype(BF16).astype(F32)
    g3 = (rest - g2).astype(BF16).astype(F32)
    lane = lax.broadcasted_iota(jnp.int32, (gate.shape[0], LANES), 1)
    return jnp.where(lane == 0, g1, jnp.where(lane == 1, g2, jnp.where(lane == 2, g3, 0.0))).astype(BF16)


def _segment_copies(tab_ref, block, copy_chunk):
    n_chunks = tab_ref[block, 0]

    def per_chunk(j, carry):
        copy_chunk(
            pl.multiple_of(j * SEG_ROWS, SEG_ROWS), pl.multiple_of(tab_ref[block, 1 + j], SEG_ROWS)
        ).start()
        return carry

    lax.fori_loop(0, n_chunks, per_chunk, 0)
    return n_chunks


def _wait_copies(count, copy_chunk):
    def wait_one(c, carry):
        copy_chunk(0, 0).wait()
        return carry

    lax.fori_loop(0, count, wait_one, 0)


def _dispatch_kernel(tab_ref, gap_ref, pos_ref, gate_ref, x_ref, xs_hbm, sorted_scr, zero_scr, pending, sem):
    step = pl.program_id(0)
    slot = step % 2
    d = x_ref.shape[1]
    n_tok = x_ref.shape[0]
    pos = pos_ref[...]
    gate = gate_ref[...]
    x = x_ref[...]
    piece_id = lax.broadcasted_iota(jnp.int32, (HOT_PIECE, n_tok), 0)
    pos_k = [jnp.broadcast_to(pos[k:k + 1, :], (HOT_PIECE, n_tok)) for k in range(TOP_K)]
    gate_k = [jnp.broadcast_to(gate[k:k + 1, :], (HOT_PIECE, n_tok)) for k in range(TOP_K)]
    for c in range(LOCAL_ROWS // SORT_CHUNK):
        one_hot, row_gate = [], []
        for p in range(SORT_CHUNK // HOT_PIECE):
            row_id = piece_id + (c * SORT_CHUNK + p * HOT_PIECE)
            hot = jnp.zeros((HOT_PIECE, n_tok), F32)
            gate_hit = jnp.zeros((HOT_PIECE, n_tok), F32)
            for k in range(TOP_K):
                hit = row_id == pos_k[k]
                hot = jnp.where(hit, 1.0, hot)
                gate_hit = jnp.where(hit, gate_k[k], gate_hit)
            one_hot.append(hot.astype(BF16))
            row_gate.append(jnp.sum(gate_hit, axis=1, keepdims=True))
        rows = jnp.dot(jnp.concatenate(one_hot, axis=0), x, preferred_element_type=F32)
        out_rows = pl.ds(c * SORT_CHUNK, SORT_CHUNK)
        sorted_scr[slot, out_rows, pl.ds(0, d)] = rows.astype(BF16)
        sorted_scr[slot, out_rows, pl.ds(d, LANES)] = _gate_lanes(jnp.concatenate(row_gate, axis=0))

    def copy_chunk(local_row, global_row):
        return pltpu.make_async_copy(
            sorted_scr.at[slot, pl.ds(local_row, SEG_ROWS), :], xs_hbm.at[pl.ds(global_row, SEG_ROWS), :], sem
        )

    @pl.when(step > 0)
    def _():
        _wait_copies(pending[0], copy_chunk)

    pending[0] = _segment_copies(tab_ref, step, copy_chunk)

    @pl.when(step == pl.num_programs(0) - 1)
    def _():
        zero_scr[...] = jnp.zeros(zero_scr.shape, BF16)
        tile_rows = zero_scr.shape[0]

        def zero_chunk(row):
            return pltpu.make_async_copy(
                zero_scr.at[pl.ds(0, SEG_ROWS), :],
                xs_hbm.at[pl.ds(pl.multiple_of(row, SEG_ROWS), SEG_ROWS), :],
                sem,
            )

        def zero_tile(row):
            return pltpu.make_async_copy(
                zero_scr, xs_hbm.at[pl.ds(pl.multiple_of(row, tile_rows), tile_rows), :], sem
            )

        def per_gap(g, total):
            start = gap_ref[g]
            chunks = gap_ref[N_EXPERTS + g]

            def per_chunk(c, carry):
                zero_chunk(start + c * SEG_ROWS).start()
                return carry

            lax.fori_loop(0, chunks, per_chunk, 0)
            return total + chunks

        n_fill = lax.fori_loop(0, N_EXPERTS, per_gap, 0)
        tail = gap_ref[2 * N_EXPERTS]
        n_tail = gap_ref[2 * N_EXPERTS + 1]

        def start_tile(c, carry):
            zero_tile(tail + c * tile_rows).start()
            return carry

        def wait_tile(c, carry):
            zero_tile(0).wait()
            return carry

        lax.fori_loop(0, n_tail, start_tile, 0)
        _wait_copies(pending[0] + n_fill, copy_chunk)
        lax.fori_loop(0, n_tail, wait_tile, 0)


def _dispatch(tab, gaps, pos_t, gate_t, xn, n_sorted, *, tb, zero_rows):
    n, d = xn.shape
    width = d + LANES
    pick_spec = pl.BlockSpec((TOP_K, tb), lambda i, tab, gaps: (0, i))
    return pl.pallas_call(
        _dispatch_kernel,
        grid_spec=pltpu.PrefetchScalarGridSpec(
            num_scalar_prefetch=2,
            grid=(n // tb,),
            in_specs=[pick_spec, pick_spec, pl.BlockSpec((tb, d), lambda i, tab, gaps: (i, 0))],
            out_specs=pl.BlockSpec(memory_space=pl.ANY),
            scratch_shapes=[
                pltpu.VMEM((2, LOCAL_ROWS, width), BF16),
                pltpu.VMEM((zero_rows, width), BF16),
                pltpu.SMEM((1,), jnp.int32),
                pltpu.SemaphoreType.DMA,
            ],
        ),
        out_shape=jax.ShapeDtypeStruct((n_sorted, width), BF16),
        compiler_params=_params(("arbitrary",)),
        name="dispatch",
    )(tab, gaps, pos_t, gate_t, xn)


def _experts_kernel(
    expert_ref, used_ref, fresh_ref, next_ref, xs_ref, wg_hbm, wu_hbm, wd_hbm, ys_ref,
    wg32, wu32, wd32, wg16, wu16, wd16, sems,
):
    it = pl.program_id(0)
    d = ys_ref.shape[1]

    def weight_copies(e):
        return (
            pltpu.make_async_copy(wg_hbm.at[e], wg32, sems.at[0]),
            pltpu.make_async_copy(wu_hbm.at[e], wu32, sems.at[1]),
            pltpu.make_async_copy(wd_hbm.at[e], wd32, sems.at[2]),
        )

    @pl.when(it == 0)
    def _():
        for copy in weight_copies(expert_ref[0]):
            copy.start()

    @pl.when(used_ref[it] == 0)
    def _():
        ys_ref[...] = jnp.zeros(ys_ref.shape, ys_ref.dtype)

    @pl.when(used_ref[it] == 1)
    def _():
        @pl.when(fresh_ref[it] == 1)
        def _():
            for copy in weight_copies(expert_ref[it]):
                copy.wait()
            wg16[...] = wg32[...].astype(BF16)
            wu16[...] = wu32[...].astype(BF16)
            wd16[...] = wd32[...].astype(BF16)

            @pl.when(next_ref[it] >= 0)
            def _():
                for copy in weight_copies(next_ref[it]):
                    copy.start()

        x = xs_ref[:, pl.ds(0, d)]
        gate = jnp.sum(xs_ref[:, pl.ds(d, LANES)].astype(F32), axis=1, keepdims=True)
        g = jnp.dot(x, wg16[...], preferred_element_type=F32)
        u = jnp.dot(x, wu16[...], preferred_element_type=F32)
        h = g * jax.nn.sigmoid(g) * u * gate
        ys_ref[...] = jnp.dot(h.astype(BF16), wd16[...], preferred_element_type=F32).astype(ys_ref.dtype)


def _experts(tiles, xs, w_gate, w_up, w_down, *, tm):
    p = xs.shape[0]
    d, f = w_gate.shape[1:]
    row_map = lambda i, expert, used, fresh, nxt: (i, 0)
    hbm = pl.BlockSpec(memory_space=pl.ANY)
    return pl.pallas_call(
        _experts_kernel,
        grid_spec=pltpu.PrefetchScalarGridSpec(
            num_scalar_prefetch=4,
            grid=(p // tm,),
            in_specs=[pl.BlockSpec((tm, xs.shape[1]), row_map), hbm, hbm, hbm],
            out_specs=pl.BlockSpec((tm, d), row_map),
            scratch_shapes=[
                pltpu.VMEM((d, f), F32), pltpu.VMEM((d, f), F32), pltpu.VMEM((f, d), F32),
                pltpu.VMEM((d, f), BF16), pltpu.VMEM((d, f), BF16), pltpu.VMEM((f, d), BF16),
                pltpu.SemaphoreType.DMA((3,)),
            ],
        ),
        out_shape=jax.ShapeDtypeStruct((p, d), BF16),
        compiler_params=_params(("arbitrary",)),
        name="experts",
    )(*tiles, xs, w_gate, w_up, w_down)


def _combine_kernel(tab_ref, pos_ref, x1_ref, xn_ref, sg_ref, su_ref, sd_ref, fg_ref, ys_hbm, *rest, n_blocks):
    o_refs = rest[:len(n_blocks)]
    local_scr, pending, sems = rest[len(n_blocks):]
    n_tok = x1_ref.shape[0]
    step = pl.program_id(0)
    slot = step % 2

    def copy_chunk_into(dst_slot):
        def copy_chunk(local_row, global_row):
            return pltpu.make_async_copy(
                ys_hbm.at[pl.ds(global_row, SEG_ROWS), :],
                local_scr.at[dst_slot, pl.ds(local_row, SEG_ROWS), :],
                sems.at[dst_slot],
            )

        return copy_chunk

    @pl.when(step == 0)
    def _():
        local_scr[...] = jnp.zeros(local_scr.shape, local_scr.dtype)
        pending[0] = _segment_copies(tab_ref, 0, copy_chunk_into(0))

    @pl.when(step + 1 < pl.num_programs(0))
    def _():
        pending[1 - slot] = _segment_copies(tab_ref, step + 1, copy_chunk_into(1 - slot))

    xn = xn_ref[...]
    sg = jnp.dot(xn, sg_ref[...], preferred_element_type=F32)
    su = jnp.dot(xn, su_ref[...], preferred_element_type=F32)
    acc = x1_ref[...] + jnp.dot(
        (sg * jax.nn.sigmoid(sg) * su).astype(BF16), sd_ref[...], preferred_element_type=F32
    )

    piece_id = lax.broadcasted_iota(jnp.int32, (HOT_PIECE, LANES), 1)
    one_hot = []
    for r in range(n_tok // HOT_PIECE):
        pos = pos_ref[pl.ds(r * HOT_PIECE, HOT_PIECE), :]
        pos_k = [jnp.broadcast_to(pos[:, k:k + 1], (HOT_PIECE, LANES)) for k in range(TOP_K)]
        cols = []
        for c in range(LOCAL_ROWS // LANES):
            col_id = piece_id + c * LANES
            hot = jnp.zeros((HOT_PIECE, LANES), F32)
            for k in range(TOP_K):
                hot = jnp.where(col_id == pos_k[k], 1.0, hot)
            cols.append(hot.astype(BF16))
        one_hot.append(jnp.concatenate(cols, axis=1))
    one_hot = jnp.concatenate(one_hot, axis=0)

    _wait_copies(pending[slot], copy_chunk_into(slot))
    routed = jnp.dot(one_hot, local_scr[slot], preferred_element_type=F32)
    out = _rms_norm(acc + routed, fg_ref[...])

    def body(p):
        o_refs[p][...] = out

    _for_part(pl.program_id(0), n_blocks, body)


def _combine(tab, pos, x1, xn, sh_gate, sh_up, sh_down, final_g, ys, part_rows, *, tb):
    n, d = x1.shape
    f = sh_gate.shape[1]
    n_blocks = tuple(r // tb for r in part_rows)
    const = lambda i, tab: (0, 0)
    row = lambda i, tab: (i, 0)
    return pl.pallas_call(
        functools.partial(_combine_kernel, n_blocks=n_blocks),
        grid_spec=pltpu.PrefetchScalarGridSpec(
            num_scalar_prefetch=1,
            grid=(n // tb,),
            in_specs=[
                pl.BlockSpec((tb, TOP_K), row),
                pl.BlockSpec((tb, d), row),
                pl.BlockSpec((tb, d), row),
                pl.BlockSpec((d, f), const),
                pl.BlockSpec((d, f), const),
                pl.BlockSpec((f, d), const),
                pl.BlockSpec((1, d), const),
                pl.BlockSpec(memory_space=pl.ANY),
            ],
            out_specs=[
                pl.BlockSpec((tb, d), _part_map(start, nb, 0))
                for start, nb in zip(_part_starts(n_blocks), n_blocks)
            ],
            scratch_shapes=[
                pltpu.VMEM((2, LOCAL_ROWS, d), BF16),
                pltpu.SMEM((2,), jnp.int32),
                pltpu.SemaphoreType.DMA((2,)),
            ],
        ),
        out_shape=[jax.ShapeDtypeStruct((r, d), F32) for r in part_rows],
        compiler_params=_params(("arbitrary",)),
        name="combine",
    )(tab, pos, x1, xn, sh_gate, sh_up, sh_down, final_g, ys)


def _segment_tables(block_counts, n_rows, *, tm):
    padded = (block_counts + SEG_ROWS - 1) // SEG_ROWS * SEG_ROWS
    local_start = jnp.cumsum(padded, axis=1) - padded
    per_expert = jnp.sum(padded, axis=0)
    region = (per_expert + tm - 1) // tm * tm
    region_start = jnp.concatenate([jnp.zeros((1,), jnp.int32), jnp.cumsum(region).astype(jnp.int32)])
    global_start = region_start[None, :N_EXPERTS] + jnp.cumsum(padded, axis=0) - padded
    chunks = padded // SEG_ROWS
    chunk_end = jnp.cumsum(chunks, axis=1)
    j = jnp.arange(LOCAL_ROWS // SEG_ROWS, dtype=jnp.int32)
    seg_of_chunk = jnp.sum((chunk_end[:, None, :] <= j[None, :, None]).astype(jnp.int32), axis=2)
    in_seg = seg_of_chunk[:, :, None] == jnp.arange(N_EXPERTS, dtype=jnp.int32)
    chunk_row = jnp.sum(
        jnp.where(in_seg, (global_start - (chunk_end - chunks) * SEG_ROWS)[:, None, :], 0), axis=2
    ) + j[None, :] * SEG_ROWS
    tab = jnp.concatenate([chunk_end[:, N_EXPERTS - 1:], chunk_row], axis=1).astype(jnp.int32)

    region_end = region_start[:N_EXPERTS] + per_expert
    tail = region_start[N_EXPERTS:]
    gaps = jnp.concatenate(
        [region_end, (region_start[1:] - region_end) // SEG_ROWS, tail, (n_rows - tail) // tm]
    ).astype(jnp.int32)

    tile_start = jnp.arange(n_rows // tm, dtype=jnp.int32) * tm
    expert = jnp.clip(
        jnp.sum((region_start[None, :] <= tile_start[:, None]).astype(jnp.int32), axis=1) - 1, 0, N_EXPERTS - 1
    )
    tile_end = jnp.sum(jnp.where(expert[:, None] == jnp.arange(N_EXPERTS), region_end[None, :], 0), axis=1)
    used = (tile_start < tile_end).astype(jnp.int32)
    fresh = jnp.concatenate([jnp.ones((1,), jnp.int32), (expert[1:] != expert[:-1]).astype(jnp.int32)])
    experts = jnp.arange(N_EXPERTS, dtype=jnp.int32)
    later = (experts[None, :] > expert[:, None]) & (per_expert[None, :] > 0)
    nxt = jnp.min(jnp.where(later, experts[None, :], N_EXPERTS), axis=1)
    nxt = jnp.where(nxt < N_EXPERTS, nxt, -1).astype(jnp.int32)
    return tab, gaps, (expert, used, fresh, nxt), local_start.astype(jnp.int32)


def _block_diag(w):
    h, dh, _ = w.shape
    eye = jnp.eye(h, dtype=w.dtype)
    return (eye[:, None, :, None] * w[:, :, None, :]).reshape(h * dh, h * dh)


ROW_TM = 512
OUT_PROJ_TM = 256
IN_PROJ_TN = 1024
PROMPT_T = 256
SAMPLE_SEQS = 16
EXPERT_TM = 256


def kernel(x_prompt, x_sample, state_lru_conv, state_lru_h, state_conf_conv, meta_tokens, norm1_g, w_in, lru_conv_w, lru_conv_b, lru_wa, lru_ba, lru_wi, lru_bi, lru_lambda, conf_conv_w, conf_conv_b, conf_ln_g, conf_ln_b, out_norm_a, out_norm_b, w_out, norm2_g, router_w, router_bias, exp_w_gate, exp_w_up, exp_w_down, sh_w_gate, sh_w_up, sh_w_down, final_norm_g):
    b_p, seq, d = x_prompt.shape
    b_s, t_s, _ = x_sample.shape
    n_p = b_p * seq
    n_s = b_s * t_s
    n = n_p + n_s
    x_parts = (x_prompt.reshape(n_p, d), x_sample.reshape(n_s, d))

    row = lambda v: v.reshape(1, -1)
    w_in16 = w_in[0].astype(BF16)
    mixer_w = (
        lru_conv_w[0], row(lru_conv_b[0]),
        _block_diag(lru_wa[0]).astype(BF16), row(lru_ba[0]),
        _block_diag(lru_wi[0]).astype(BF16), row(lru_bi[0]),
        row(lru_lambda[0]),
        conf_conv_w[0], row(conf_conv_b[0]), row(conf_ln_g[0]), row(conf_ln_b[0]),
        row(out_norm_a[0]), row(out_norm_b[0]),
    )

    proj_m = _in_proj((meta_tokens,), row(norm1_g[0]), w_in16, tm=N_META, tn=IN_PROJ_TN)
    _, m_lc, m_h, m_cc = _mixer(
        proj_m, 0,
        jnp.zeros((1, LRU_CONV - 1, W_A), F32), jnp.zeros((1, 1, W_A), F32), jnp.zeros((1, CONF_KERNEL - 1, W_B), F32),
        mixer_w, n_seq=1, n_t=N_META, n_tiles=1,
    )

    proj = _in_proj(x_parts, row(norm1_g[0]), w_in16, tm=ROW_TM, tn=IN_PROJ_TN)
    y_p, p_lc, p_h, p_cc = _mixer(
        proj, 0,
        jnp.broadcast_to(m_lc, (b_p,) + m_lc.shape[1:]), jnp.broadcast_to(m_h, (b_p,) + m_h.shape[1:]),
        jnp.broadcast_to(m_cc, (b_p,) + m_cc.shape[1:]),
        mixer_w, n_seq=1, n_t=PROMPT_T, n_tiles=seq // PROMPT_T,
    )
    y_s, s_lc, s_h, s_cc = _mixer(
        proj, n_p // (SAMPLE_SEQS * t_s),
        state_lru_conv[0], state_lru_h[0].reshape(b_s, 1, W_A), state_conf_conv[0],
        mixer_w, n_seq=SAMPLE_SEQS, n_t=t_s, n_tiles=1,
    )

    x1, xn2, logits = _out_proj(
        (y_p, y_s), x_parts, w_out[0].astype(BF16), row(norm2_g[0]), router_w[0].astype(BF16), tm=OUT_PROJ_TM
    )

    eidx_t, gate_t, rank_t, block_counts = _route(logits.T, router_bias[0].reshape(N_EXPERTS, 1), tl=TOKEN_BLOCK)
    n_blocks = n // TOKEN_BLOCK
    n_sorted = n_blocks * (TOKEN_BLOCK * TOP_K + N_EXPERTS * (SEG_ROWS - 1)) + N_EXPERTS * (EXPERT_TM - 1)
    n_sorted = -(-n_sorted // EXPERT_TM) * EXPERT_TM
    tab, gaps, tiles, local_start = _segment_tables(block_counts[:, :, 0], n_sorted, tm=EXPERT_TM)
    expert_one_hot = eidx_t[:, :, None] == jnp.arange(N_EXPERTS, dtype=jnp.int32)
    token_start = jnp.repeat(local_start, TOKEN_BLOCK, axis=0)
    pos_t = jnp.sum(jnp.where(expert_one_hot, token_start[None], 0), axis=2) + rank_t

    xs = _dispatch(tab, gaps, pos_t, gate_t, xn2, n_sorted, tb=TOKEN_BLOCK, zero_rows=EXPERT_TM)
    ys = _experts(tiles, xs, exp_w_gate[0], exp_w_up[0], exp_w_down[0], tm=EXPERT_TM)
    out_p, out_s = _combine(
        tab, pos_t.T, x1, xn2,
        sh_w_gate[0].astype(BF16), sh_w_up[0].astype(BF16), sh_w_down[0].astype(BF16),
        row(final_norm_g), ys, (n_p, n_s), tb=TOKEN_BLOCK,
    )

    return (
        out_p.reshape(b_p, seq, d), out_s.reshape(b_s, t_s, d),
        p_lc[None], p_h.reshape(1, b_p, W_A), p_cc[None],
        s_lc[None], s_h.reshape(1, b_s, W_A), s_cc[None],
    )
```

```python
import functools

import jax
import jax.numpy as jnp
from jax import lax
from jax.experimental import pallas as pl
from jax.experimental.pallas import tpu as pltpu

D_MODEL = 2048
N_META = 16
W_A = 1024
W_B = 1024
LRU_CONV = 4
LRU_C = 8.0
CONF_KERNEL = 31
N_EXPERTS = 64
N_EXPERT_GROUPS = 8
EXPERTS_PER_GROUP = 8
TOPK_GROUPS = 4
TOP_K = 8
ROUTED_SCALE = 2.5
EPS = 1e-6

SUBLANES = 8
LANES = 128
TOKEN_BLOCK = 256
SEG_ROWS = SUBLANES
LOCAL_ROWS = 2560
SORT_CHUNK = 512
HOT_PIECE = 16
CHUNK_ROWS = (SEG_ROWS,)
CHUNK_MAX = (LOCAL_ROWS // SEG_ROWS,)
ISSUE_UNROLL = 4
WAIT_GROUP = 32
assert LOCAL_ROWS >= TOKEN_BLOCK * TOP_K + N_EXPERTS * (SEG_ROWS - 1) and LOCAL_ROWS % SORT_CHUNK == 0
LRU_HIST = SUBLANES
CONF_HIST = 32
VMEM_LIMIT = 56 * 1024 * 1024

F32 = jnp.float32
BF16 = jnp.bfloat16


def _params(semantics):
    return pltpu.CompilerParams(dimension_semantics=semantics, vmem_limit_bytes=VMEM_LIMIT)


def _rms_norm(x, g):
    return x * lax.rsqrt(jnp.mean(x * x, axis=-1, keepdims=True) + EPS) * g


def _part_starts(n_blocks):
    starts, s = [], 0
    for nb in n_blocks:
        starts.append(s)
        s += nb
    return starts


def _part_map(start, nb, grid_axis):
    def index_map(*ids):
        return (jnp.clip(ids[grid_axis] - start, 0, nb - 1), 0)

    return index_map


def _for_part(i, n_blocks, body):
    for p, (start, nb) in enumerate(zip(_part_starts(n_blocks), n_blocks)):
        pl.when((i >= start) & (i < start + nb))(functools.partial(body, p))


def _in_proj_kernel(*refs, n_blocks):
    x_refs = refs[:len(n_blocks)]
    g_ref, w_ref, o_ref = refs[len(n_blocks):]

    def body(p):
        xn = _rms_norm(x_refs[p][...], g_ref[...])
        o_ref[...] = jnp.dot(xn.astype(BF16), w_ref[...], preferred_element_type=F32)

    _for_part(pl.program_id(1), n_blocks, body)


def _in_proj(x_parts, g, w, *, tm, tn):
    d, f = w.shape
    n_blocks = tuple(x.shape[0] // tm for x in x_parts)
    x_specs = [
        pl.BlockSpec((tm, d), _part_map(start, nb, 1)) for start, nb in zip(_part_starts(n_blocks), n_blocks)
    ]
    return pl.pallas_call(
        functools.partial(_in_proj_kernel, n_blocks=n_blocks),
        grid=(f // tn, sum(n_blocks)),
        in_specs=x_specs + [
            pl.BlockSpec((1, d), lambda j, i: (0, 0)),
            pl.BlockSpec((d, tn), lambda j, i: (0, j)),
        ],
        out_specs=pl.BlockSpec((tm, tn), lambda j, i: (i, j)),
        out_shape=jax.ShapeDtypeStruct((sum(n_blocks) * tm, f), F32),
        compiler_params=_params(("arbitrary", "arbitrary")),
        name="in_proj",
    )(*x_parts, g, w)


def _causal_conv(ext, w_ref, n_t, hist):
    taps = w_ref.shape[0]
    length = ext.shape[1]
    shifted = {0: ext}
    acc = None
    for j in range(taps):
        q, r = divmod(hist - (taps - 1) + j, SUBLANES)
        if r not in shifted:
            shifted[r] = pltpu.roll(ext, length - r, 1)
        term = w_ref[pl.ds(j, 1), :][None] * shifted[r][:, q * SUBLANES:q * SUBLANES + n_t, :]
        acc = term if acc is None else acc + term
    return acc


def _mixer_kernel(
    proj_ref, lc_ref, h0_ref, cc_ref,
    wca_ref, bca_ref, wa_ref, ba_ref, wi_ref, bi_ref, lam_ref,
    wcb_ref, bcb_ref, lng_ref, lnb_ref, nag_ref, nbg_ref,
    y_ref, lc_out, h_out, cc_out,
    ua_ext, glu_ext, hst, a_scr, u_scr, h_scr,
    *, n_seq, n_t,
):
    t = pl.program_id(1)
    rows = n_seq * n_t
    c = W_A

    @pl.when(t == 0)
    def _():
        ua_ext[:, pl.ds(0, LRU_HIST), :] = jnp.zeros((n_seq, LRU_HIST, c), F32)
        ua_ext[:, pl.ds(LRU_HIST - (LRU_CONV - 1), LRU_CONV - 1), :] = lc_ref[...]
        glu_ext[:, pl.ds(0, CONF_HIST), :] = jnp.zeros((n_seq, CONF_HIST, c), F32)
        glu_ext[:, pl.ds(CONF_HIST - (CONF_KERNEL - 1), CONF_KERNEL - 1), :] = cc_ref[...]
        hst[...] = h0_ref[...]

    ua_ext[:, pl.ds(LRU_HIST, n_t), :] = proj_ref[:, pl.ds(0, c)].reshape(n_seq, n_t, c)
    c_a = _causal_conv(ua_ext[...], wca_ref, n_t, LRU_HIST)
    c_a = (c_a + bca_ref[...][None]).reshape(rows, c)
    c_a16 = c_a.astype(BF16)
    r = jax.nn.sigmoid(jnp.dot(c_a16, wa_ref[...], preferred_element_type=F32) + ba_ref[...])
    i = jax.nn.sigmoid(jnp.dot(c_a16, wi_ref[...], preferred_element_type=F32) + bi_ref[...])
    neg_lam = -lam_ref[...]
    softplus = jnp.maximum(neg_lam, 0.0) + jnp.log1p(jnp.exp(-jnp.abs(neg_lam)))
    log_a = (-LRU_C * r) * softplus
    a = jnp.exp(log_a)
    a_scr[...] = a
    u_scr[...] = jnp.sqrt(-jnp.tanh(log_a) * (a * a + 1.0)) * (i * c_a)

    row8 = lax.broadcasted_iota(jnp.int32, (SUBLANES, c), 0)
    groups = n_t // SUBLANES
    for s in range(n_seq):
        def scan_group(g, carry, s=s):
            off = pl.multiple_of(s * n_t + g * SUBLANES, SUBLANES)
            a = a_scr[pl.ds(off, SUBLANES), :]
            u = u_scr[pl.ds(off, SUBLANES), :]
            for d in (1, 2, 4):
                keep = row8 >= d
                a_sh = jnp.where(keep, pltpu.roll(a, d, 0), 1.0)
                u_sh = jnp.where(keep, pltpu.roll(u, d, 0), 0.0)
                u = a * u_sh + u
                a = a * a_sh
            h = a * carry + u
            h_scr[pl.ds(off, SUBLANES), :] = h
            return jnp.broadcast_to(h[SUBLANES - 1:SUBLANES, :], (SUBLANES, c))

        carry = lax.fori_loop(0, groups, scan_group, jnp.broadcast_to(hst[s], (SUBLANES, c)))
        hst[s] = carry[0:1, :]

    y_a = jax.nn.gelu(proj_ref[:, pl.ds(c, c)]) * h_scr[...]
    y_ref[:, pl.ds(0, c)] = _rms_norm(y_a, nag_ref[...]).astype(y_ref.dtype)

    glu = proj_ref[:, pl.ds(2 * c, c)] * jax.nn.sigmoid(proj_ref[:, pl.ds(3 * c, c)])
    glu_ext[:, pl.ds(CONF_HIST, n_t), :] = glu.reshape(n_seq, n_t, c)
    c_b = _causal_conv(glu_ext[...], wcb_ref, n_t, CONF_HIST)
    c_b = (c_b + bcb_ref[...][None]).reshape(rows, c)
    mu = jnp.mean(c_b, axis=-1, keepdims=True)
    cen = c_b - mu
    var = jnp.mean(cen * cen, axis=-1, keepdims=True)
    ln = cen * lax.rsqrt(var + EPS) * lng_ref[...] + lnb_ref[...]
    y_b = ln * jax.nn.sigmoid(ln)
    y_ref[:, pl.ds(c, c)] = _rms_norm(y_b, nbg_ref[...]).astype(y_ref.dtype)

    lc_out[...] = ua_ext[:, pl.ds(LRU_HIST + n_t - (LRU_CONV - 1), LRU_CONV - 1), :]
    cc_out[...] = glu_ext[:, pl.ds(CONF_HIST + n_t - (CONF_KERNEL - 1), CONF_KERNEL - 1), :]
    h_out[...] = hst[...]
    ua_ext[:, pl.ds(0, LRU_HIST), :] = ua_ext[:, pl.ds(n_t, LRU_HIST), :]
    glu_ext[:, pl.ds(0, CONF_HIST), :] = glu_ext[:, pl.ds(n_t, CONF_HIST), :]


def _mixer(proj, row_block0, lc, h0, cc, weights, *, n_seq, n_t, n_tiles):
    b = lc.shape[0]
    c = W_A
    rows = n_seq * n_t
    n_sb = b // n_seq

    def row_map(sb, t):
        return (row_block0 + sb * n_tiles + t, 0)

    def const2(sb, t):
        return (0, 0)

    def seq_map(sb, t):
        return (sb, 0, 0)

    w_specs = [pl.BlockSpec(w.shape, const2) for w in weights]
    return pl.pallas_call(
        functools.partial(_mixer_kernel, n_seq=n_seq, n_t=n_t),
        grid=(n_sb, n_tiles),
        in_specs=[
            pl.BlockSpec((rows, 4 * c), row_map),
            pl.BlockSpec((n_seq, LRU_CONV - 1, c), seq_map),
            pl.BlockSpec((n_seq, 1, c), seq_map),
            pl.BlockSpec((n_seq, CONF_KERNEL - 1, c), seq_map),
        ] + w_specs,
        out_specs=[
            pl.BlockSpec((rows, 2 * c), lambda sb, t: (sb * n_tiles + t, 0)),
            pl.BlockSpec((n_seq, LRU_CONV - 1, c), seq_map),
            pl.BlockSpec((n_seq, 1, c), seq_map),
            pl.BlockSpec((n_seq, CONF_KERNEL - 1, c), seq_map),
        ],
        out_shape=[
            jax.ShapeDtypeStruct((b * n_tiles * n_t, 2 * c), BF16),
            jax.ShapeDtypeStruct((b, LRU_CONV - 1, c), F32),
            jax.ShapeDtypeStruct((b, 1, c), F32),
            jax.ShapeDtypeStruct((b, CONF_KERNEL - 1, c), F32),
        ],
        scratch_shapes=[
            pltpu.VMEM((n_seq, LRU_HIST + n_t, c), F32),
            pltpu.VMEM((n_seq, CONF_HIST + n_t, c), F32),
            pltpu.VMEM((n_seq, 1, c), F32),
            pltpu.VMEM((rows, c), F32),
            pltpu.VMEM((rows, c), F32),
            pltpu.VMEM((rows, c), F32),
        ],
        compiler_params=_params(("arbitrary", "arbitrary")),
        name="mixer",
    )(proj, lc, h0, cc, *weights)


def _out_proj_kernel(*refs, n_blocks):
    k = len(n_blocks)
    y_refs, x_refs = refs[:k], refs[k:2 * k]
    w_ref, g_ref, rw_ref, x1_ref, xn_ref, logit_ref = refs[2 * k:]

    def body(p):
        x1 = x_refs[p][...] + jnp.dot(y_refs[p][...], w_ref[...], preferred_element_type=F32)
        x1_ref[...] = x1
        xn = _rms_norm(x1, g_ref[...]).astype(BF16)
        xn_ref[...] = xn
        logit_ref[...] = jnp.dot(xn, rw_ref[...], preferred_element_type=F32)

    _for_part(pl.program_id(0), n_blocks, body)


def _out_proj(y_parts, x_parts, w_out, g2, router_w, *, tm):
    d = w_out.shape[0]
    e = router_w.shape[1]
    n_blocks = tuple(x.shape[0] // tm for x in x_parts)
    part_specs = [
        pl.BlockSpec((tm, d), _part_map(start, nb, 0)) for start, nb in zip(_part_starts(n_blocks), n_blocks)
    ]
    n = sum(n_blocks) * tm
    const = lambda i: (0, 0)
    row = lambda i: (i, 0)
    return pl.pallas_call(
        functools.partial(_out_proj_kernel, n_blocks=n_blocks),
        grid=(sum(n_blocks),),
        in_specs=part_specs + part_specs + [
            pl.BlockSpec((d, d), const),
            pl.BlockSpec((1, d), const),
            pl.BlockSpec((d, e), const),
        ],
        out_specs=[pl.BlockSpec((tm, d), row), pl.BlockSpec((tm, d), row), pl.BlockSpec((tm, e), row)],
        out_shape=[
            jax.ShapeDtypeStruct((n, d), F32),
            jax.ShapeDtypeStruct((n, d), BF16),
            jax.ShapeDtypeStruct((n, e), F32),
        ],
        compiler_params=_params(("arbitrary",)),
        name="out_proj",
    )(*y_parts, *x_parts, w_out, g2, router_w)


def _first_argmax(work, index, sentinel):
    m = jnp.max(work, axis=0, keepdims=True)
    return jnp.min(jnp.where(work == m, index, sentinel), axis=0, keepdims=True)


def _route_kernel(logit_ref, bias_ref, eidx_ref, gate_ref, rank_ref, count_ref):
    n_tok = logit_ref.shape[1]

    scores = jax.nn.sigmoid(logit_ref[...])
    biased = scores + bias_ref[...]
    grouped = biased.reshape(N_EXPERT_GROUPS, EXPERTS_PER_GROUP, n_tok)
    in_group = lax.broadcasted_iota(jnp.int32, grouped.shape, 1)
    top1 = jnp.max(grouped, axis=1, keepdims=True)
    first1 = jnp.min(jnp.where(grouped == top1, in_group, EXPERTS_PER_GROUP), axis=1, keepdims=True)
    top2 = jnp.max(jnp.where(in_group == first1, -jnp.inf, grouped), axis=1, keepdims=True)
    group_scores = (top1 + top2).reshape(N_EXPERT_GROUPS, n_tok)

    group_id = lax.broadcasted_iota(jnp.int32, group_scores.shape, 0)
    group_on = jnp.zeros(group_scores.shape, F32)
    work = group_scores
    for _ in range(TOPK_GROUPS):
        pick = group_id == _first_argmax(work, group_id, N_EXPERT_GROUPS)
        group_on = jnp.where(pick, 1.0, group_on)
        work = jnp.where(pick, -jnp.inf, work)

    masked = jnp.where(group_on.reshape(N_EXPERT_GROUPS, 1, n_tok) > 0.0, grouped, -jnp.inf)
    work = masked.reshape(N_EXPERTS, n_tok)
    expert_id = lax.broadcasted_iota(jnp.int32, work.shape, 0)
    ids, sel = [], []
    chosen = jnp.zeros(work.shape, F32)
    for _ in range(TOP_K):
        first = _first_argmax(work, expert_id, N_EXPERTS)
        pick = expert_id == first
        ids.append(first)
        sel.append(jnp.sum(jnp.where(pick, scores, 0.0), axis=0, keepdims=True))
        chosen = jnp.where(pick, 1.0, chosen)
        work = jnp.where(pick, -jnp.inf, work)
    sel = jnp.concatenate(sel, axis=0)
    eidx_ref[...] = jnp.concatenate(ids, axis=0)
    gate_ref[...] = sel / (jnp.sum(sel, axis=0, keepdims=True) + 1e-20) * ROUTED_SCALE

    earlier = lax.broadcasted_iota(jnp.int32, (n_tok, n_tok), 0) < lax.broadcasted_iota(jnp.int32, (n_tok, n_tok), 1)
    before = jnp.dot(chosen.astype(BF16), earlier.astype(BF16), preferred_element_type=F32)
    rank_ref[...] = jnp.concatenate(
        [jnp.sum(jnp.where(expert_id == ids[k], before, 0.0), axis=0, keepdims=True) for k in range(TOP_K)], axis=0
    ).astype(jnp.int32)
    count_ref[0] = jnp.sum(chosen, axis=1, keepdims=True).astype(jnp.int32)


def _route(logits_t, bias, *, tl):
    e, n = logits_t.shape
    pick_spec = pl.BlockSpec((TOP_K, tl), lambda i: (0, i))
    return pl.pallas_call(
        _route_kernel,
        grid=(n // tl,),
        in_specs=[pl.BlockSpec((e, tl), lambda i: (0, i)), pl.BlockSpec((e, 1), lambda i: (0, 0))],
        out_specs=[pick_spec, pick_spec, pick_spec, pl.BlockSpec((1, e, 1), lambda i: (i, 0, 0))],
        out_shape=[
            jax.ShapeDtypeStruct((TOP_K, n), jnp.int32),
            jax.ShapeDtypeStruct((TOP_K, n), F32),
            jax.ShapeDtypeStruct((TOP_K, n), jnp.int32),
            jax.ShapeDtypeStruct((n // tl, e, 1), jnp.int32),
        ],
        compiler_params=_params(("arbitrary",)),
        name="route",
    )(logits_t, bias)


def _gate_lanes(gate):
    g1 = gate.astype(BF16).astype(F32)
    rest = gate - g1
    g2 = rest.astype(BF16).astype(F32)
    g3 = (rest - g2).astype(BF16).astype(F32)
    lane = lax.broadcasted_iota(jnp.int32, (gate.shape[0], LANES), 1)
    return jnp.where(lane == 0, g1, jnp.where(lane == 1, g2, jnp.where(lane == 2, g3, 0.0))).astype(BF16)


def _tab_offsets():
    offsets, col = [], len(CHUNK_ROWS)
    for cap in CHUNK_MAX:
        offsets.append((col, col + cap))
        col += 2 * cap
    return offsets, col


def _segment_copies(tab_ref, block, make_copy):
    offsets, width = _tab_offsets()
    base = block * width
    counts = []
    for k, (rows, (local_col, global_col)) in enumerate(zip(CHUNK_ROWS, offsets)):
        n_chunks = tab_ref[base + k]

        def start_one(j, rows=rows, local_col=local_col, global_col=global_col):
            make_copy(
                pl.multiple_of(tab_ref[base + local_col + j], SEG_ROWS),
                pl.multiple_of(tab_ref[base + global_col + j], SEG_ROWS),
                rows,
            ).start()

        def per_group(g, carry):
            for u in range(ISSUE_UNROLL):
                start_one(g * ISSUE_UNROLL + u)
            return carry

        def per_chunk(j, carry):
            start_one(j)
            return carry

        n_groups = n_chunks // ISSUE_UNROLL
        lax.fori_loop(0, n_groups, per_group, 0)
        lax.fori_loop(n_groups * ISSUE_UNROLL, n_chunks, per_chunk, 0)
        counts.append(n_chunks)
    return counts


def _wait_copies(counts, make_copy):
    for rows, count in zip(CHUNK_ROWS, counts):
        def wait_group(c, carry, rows=rows):
            make_copy(0, 0, rows * WAIT_GROUP).wait()
            return carry

        def wait_one(c, carry, rows=rows):
            make_copy(0, 0, rows).wait()
            return carry

        n_groups = count // WAIT_GROUP
        lax.fori_loop(0, n_groups, wait_group, 0)
        lax.fori_loop(n_groups * WAIT_GROUP, count, wait_one, 0)


def _dispatch_kernel(tab_ref, gap_ref, pos_ref, gate_ref, x_ref, xs_hbm, sorted_scr, zero_scr, pending, sem):
    step = pl.program_id(0)
    slot = step % 2
    d = x_ref.shape[1]
    n_tok = x_ref.shape[0]
    pos = pos_ref[...]
    gate = gate_ref[...]
    x = x_ref[...]
    piece_id = lax.broadcasted_iota(jnp.int32, (HOT_PIECE, n_tok), 0)
    pos_k = [jnp.broadcast_to(pos[k:k + 1, :], (HOT_PIECE, n_tok)) for k in range(TOP_K)]
    gate_k = [jnp.broadcast_to(gate[k:k + 1, :], (HOT_PIECE, n_tok)) for k in range(TOP_K)]
    for c in range(LOCAL_ROWS // SORT_CHUNK):
        one_hot, row_gate = [], []
        for p in range(SORT_CHUNK // HOT_PIECE):
            row_id = piece_id + (c * SORT_CHUNK + p * HOT_PIECE)
            hot = jnp.zeros((HOT_PIECE, n_tok), F32)
            gate_hit = jnp.zeros((HOT_PIECE, n_tok), F32)
            for k in range(TOP_K):
                hit = row_id == pos_k[k]
                hot = jnp.where(hit, 1.0, hot)
                gate_hit = jnp.where(hit, gate_k[k], gate_hit)
            one_hot.append(hot.astype(BF16))
            row_gate.append(jnp.sum(gate_hit, axis=1, keepdims=True))
        rows = jnp.dot(jnp.concatenate(one_hot, axis=0), x, preferred_element_type=F32)
        out_rows = pl.ds(c * SORT_CHUNK, SORT_CHUNK)
        sorted_scr[slot, out_rows, pl.ds(0, d)] = rows.astype(BF16)
        sorted_scr[slot, out_rows, pl.ds(d, LANES)] = _gate_lanes(jnp.concatenate(row_gate, axis=0))

    def copy_chunk(local_row, global_row, rows):
        return pltpu.make_async_copy(
            sorted_scr.at[slot, pl.ds(local_row, rows), :], xs_hbm.at[pl.ds(global_row, rows), :], sem
        )

    @pl.when(step > 0)
    def _():
        _wait_copies([pending[k] for k in range(len(CHUNK_ROWS))], copy_chunk)

    for k, count in enumerate(_segment_copies(tab_ref, step, copy_chunk)):
        pending[k] = count

    @pl.when(step == pl.num_programs(0) - 1)
    def _():
        zero_scr[...] = jnp.zeros(zero_scr.shape, BF16)
        tile_rows = zero_scr.shape[0]

        def zero_chunk(row):
            return pltpu.make_async_copy(
                zero_scr.at[pl.ds(0, SEG_ROWS), :],
                xs_hbm.at[pl.ds(pl.multiple_of(row, SEG_ROWS), SEG_ROWS), :],
                sem,
            )

        def zero_tile(row):
            return pltpu.make_async_copy(
                zero_scr, xs_hbm.at[pl.ds(pl.multiple_of(row, tile_rows), tile_rows), :], sem
            )

        def per_gap(g, total):
            start = gap_ref[g]
            chunks = gap_ref[N_EXPERTS + g]

            def per_chunk(c, carry):
                zero_chunk(start + c * SEG_ROWS).start()
                return carry

            lax.fori_loop(0, chunks, per_chunk, 0)
            return total + chunks

        n_fill = lax.fori_loop(0, N_EXPERTS, per_gap, 0)
        tail = gap_ref[2 * N_EXPERTS]
        n_tail = gap_ref[2 * N_EXPERTS + 1]

        def start_tile(c, carry):
            zero_tile(tail + c * tile_rows).start()
            return carry

        def wait_tile(c, carry):
            zero_tile(0).wait()
            return carry

        def wait_chunk(c, carry):
            zero_chunk(0).wait()
            return carry

        lax.fori_loop(0, n_tail, start_tile, 0)
        _wait_copies([pending[k] for k in range(len(CHUNK_ROWS))], copy_chunk)
        lax.fori_loop(0, n_fill, wait_chunk, 0)
        lax.fori_loop(0, n_tail, wait_tile, 0)


def _dispatch(tab, gaps, pos_t, gate_t, xn, n_sorted, *, tb, zero_rows):
    n, d = xn.shape
    width = d + LANES
    pick_spec = pl.BlockSpec((TOP_K, tb), lambda i, tab, gaps: (0, i))
    return pl.pallas_call(
        _dispatch_kernel,
        grid_spec=pltpu.PrefetchScalarGridSpec(
            num_scalar_prefetch=2,
            grid=(n // tb,),
            in_specs=[pick_spec, pick_spec, pl.BlockSpec((tb, d), lambda i, tab, gaps: (i, 0))],
            out_specs=pl.BlockSpec(memory_space=pl.ANY),
            scratch_shapes=[
                pltpu.VMEM((2, LOCAL_ROWS, width), BF16),
                pltpu.VMEM((zero_rows, width), BF16),
                pltpu.SMEM((len(CHUNK_ROWS),), jnp.int32),
                pltpu.SemaphoreType.DMA,
            ],
        ),
        out_shape=jax.ShapeDtypeStruct((n_sorted, width), BF16),
        compiler_params=_params(("arbitrary",)),
        name="dispatch",
    )(tab, gaps, pos_t, gate_t, xn)


def _experts_kernel(
    expert_ref, used_ref, fresh_ref, next_ref, xs_ref, wg_hbm, wu_hbm, wd_hbm, ys_ref,
    wg32, wu32, wd32, wg16, wu16, wd16, sems,
):
    it = pl.program_id(0)
    d = ys_ref.shape[1]

    def weight_copies(e):
        return (
            pltpu.make_async_copy(wg_hbm.at[e], wg32, sems.at[0]),
            pltpu.make_async_copy(wu_hbm.at[e], wu32, sems.at[1]),
            pltpu.make_async_copy(wd_hbm.at[e], wd32, sems.at[2]),
        )

    @pl.when(it == 0)
    def _():
        for copy in weight_copies(expert_ref[0]):
            copy.start()

    @pl.when(used_ref[it] == 0)
    def _():
        ys_ref[...] = jnp.zeros(ys_ref.shape, ys_ref.dtype)

    @pl.when(used_ref[it] == 1)
    def _():
        @pl.when(fresh_ref[it] == 1)
        def _():
            for copy in weight_copies(expert_ref[it]):
                copy.wait()
            wg16[...] = wg32[...].astype(BF16)
            wu16[...] = wu32[...].astype(BF16)
            wd16[...] = wd32[...].astype(BF16)

            @pl.when(next_ref[it] >= 0)
            def _():
                for copy in weight_copies(next_ref[it]):
                    copy.start()

        x = xs_ref[:, pl.ds(0, d)]
        gate = jnp.sum(xs_ref[:, pl.ds(d, LANES)].astype(F32), axis=1, keepdims=True)
        g = jnp.dot(x, wg16[...], preferred_element_type=F32)
        u = jnp.dot(x, wu16[...], preferred_element_type=F32)
        h = g * jax.nn.sigmoid(g) * u * gate
        ys_ref[...] = jnp.dot(h.astype(BF16), wd16[...], preferred_element_type=F32).astype(ys_ref.dtype)


def _experts(tiles, xs, w_gate, w_up, w_down, *, tm):
    p = xs.shape[0]
    d, f = w_gate.shape[1:]
    row_map = lambda i, expert, used, fresh, nxt: (i, 0)
    hbm = pl.BlockSpec(memory_space=pl.ANY)
    return pl.pallas_call(
        _experts_kernel,
        grid_spec=pltpu.PrefetchScalarGridSpec(
            num_scalar_prefetch=4,
            grid=(p // tm,),
            in_specs=[pl.BlockSpec((tm, xs.shape[1]), row_map), hbm, hbm, hbm],
            out_specs=pl.BlockSpec((tm, d), row_map),
            scratch_shapes=[
                pltpu.VMEM((d, f), F32), pltpu.VMEM((d, f), F32), pltpu.VMEM((f, d), F32),
                pltpu.VMEM((d, f), BF16), pltpu.VMEM((d, f), BF16), pltpu.VMEM((f, d), BF16),
                pltpu.SemaphoreType.DMA((3,)),
            ],
        ),
        out_shape=jax.ShapeDtypeStruct((p, d), BF16),
        compiler_params=_params(("arbitrary",)),
        name="experts",
    )(*tiles, xs, w_gate, w_up, w_down)


def _combine_kernel(tab_ref, pos_ref, x1_ref, xn_ref, sg_ref, su_ref, sd_ref, fg_ref, ys_hbm, *rest, n_blocks):
    o_refs = rest[:len(n_blocks)]
    local_scr, pending, sems = rest[len(n_blocks):]
    n_tok = x1_ref.shape[0]
    step = pl.program_id(0)
    slot = step % 2

    n_sizes = len(CHUNK_ROWS)

    def copy_chunk_into(dst_slot):
        def copy_chunk(local_row, global_row, rows):
            return pltpu.make_async_copy(
                ys_hbm.at[pl.ds(global_row, rows), :],
                local_scr.at[dst_slot, pl.ds(local_row, rows), :],
                sems.at[dst_slot],
            )

        return copy_chunk

    def fetch(block, dst_slot):
        for k, count in enumerate(_segment_copies(tab_ref, block, copy_chunk_into(dst_slot))):
            pending[dst_slot * n_sizes + k] = count

    @pl.when(step == 0)
    def _():
        local_scr[...] = jnp.zeros(local_scr.shape, local_scr.dtype)
        fetch(0, 0)

    @pl.when(step + 1 < pl.num_programs(0))
    def _():
        fetch(step + 1, 1 - slot)

    xn = xn_ref[...]
    sg = jnp.dot(xn, sg_ref[...], preferred_element_type=F32)
    su = jnp.dot(xn, su_ref[...], preferred_element_type=F32)
    acc = x1_ref[...] + jnp.dot(
        (sg * jax.nn.sigmoid(sg) * su).astype(BF16), sd_ref[...], preferred_element_type=F32
    )

    piece_id = lax.broadcasted_iota(jnp.int32, (HOT_PIECE, LANES), 1)
    one_hot = []
    for r in range(n_tok // HOT_PIECE):
        pos = pos_ref[pl.ds(r * HOT_PIECE, HOT_PIECE), :]
        pos_k = [jnp.broadcast_to(pos[:, k:k + 1], (HOT_PIECE, LANES)) for k in range(TOP_K)]
        cols = []
        for c in range(LOCAL_ROWS // LANES):
            col_id = piece_id + c * LANES
            hot = jnp.zeros((HOT_PIECE, LANES), F32)
            for k in range(TOP_K):
                hot = jnp.where(col_id == pos_k[k], 1.0, hot)
            cols.append(hot.astype(BF16))
        one_hot.append(jnp.concatenate(cols, axis=1))
    one_hot = jnp.concatenate(one_hot, axis=0)

    _wait_copies([pending[slot * n_sizes + k] for k in range(n_sizes)], copy_chunk_into(slot))
    routed = jnp.dot(one_hot, local_scr[slot], preferred_element_type=F32)
    out = _rms_norm(acc + routed, fg_ref[...])

    def body(p):
        o_refs[p][...] = out

    _for_part(pl.program_id(0), n_blocks, body)


def _combine(tab, pos, x1, xn, sh_gate, sh_up, sh_down, final_g, ys, part_rows, *, tb):
    n, d = x1.shape
    f = sh_gate.shape[1]
    n_blocks = tuple(r // tb for r in part_rows)
    const = lambda i, tab: (0, 0)
    row = lambda i, tab: (i, 0)
    return pl.pallas_call(
        functools.partial(_combine_kernel, n_blocks=n_blocks),
        grid_spec=pltpu.PrefetchScalarGridSpec(
            num_scalar_prefetch=1,
            grid=(n // tb,),
            in_specs=[
                pl.BlockSpec((tb, TOP_K), row),
                pl.BlockSpec((tb, d), row),
                pl.BlockSpec((tb, d), row),
                pl.BlockSpec((d, f), const),
                pl.BlockSpec((d, f), const),
                pl.BlockSpec((f, d), const),
                pl.BlockSpec((1, d), const),
                pl.BlockSpec(memory_space=pl.ANY),
            ],
            out_specs=[
                pl.BlockSpec((tb, d), _part_map(start, nb, 0))
                for start, nb in zip(_part_starts(n_blocks), n_blocks)
            ],
            scratch_shapes=[
                pltpu.VMEM((2, LOCAL_ROWS, d), BF16),
                pltpu.SMEM((2 * len(CHUNK_ROWS),), jnp.int32),
                pltpu.SemaphoreType.DMA((2,)),
            ],
        ),
        out_shape=[jax.ShapeDtypeStruct((r, d), F32) for r in part_rows],
        compiler_params=_params(("arbitrary",)),
        name="combine",
    )(tab, pos, x1, xn, sh_gate, sh_up, sh_down, final_g, ys)


def _segment_tables(block_counts, n_rows, *, tm):
    padded = (block_counts + SEG_ROWS - 1) // SEG_ROWS * SEG_ROWS
    local_start = jnp.cumsum(padded, axis=1) - padded
    per_expert = jnp.sum(padded, axis=0)
    region = (per_expert + tm - 1) // tm * tm
    region_start = jnp.concatenate([jnp.zeros((1,), jnp.int32), jnp.cumsum(region).astype(jnp.int32)])
    global_start = region_start[None, :N_EXPERTS] + jnp.cumsum(padded, axis=0) - padded
    counts, lists = [], []
    done = jnp.zeros_like(padded)
    for rows, cap in zip(CHUNK_ROWS, CHUNK_MAX):
        per_seg = (padded - done) // rows
        seg_end = jnp.cumsum(per_seg, axis=1)
        j = jnp.arange(cap, dtype=jnp.int32)
        seg_of_chunk = jnp.sum((seg_end[:, None, :] <= j[None, :, None]).astype(jnp.int32), axis=2)
        in_seg = seg_of_chunk[:, :, None] == jnp.arange(N_EXPERTS, dtype=jnp.int32)
        first = done - (seg_end - per_seg) * rows
        for start in (local_start, global_start):
            lists.append(jnp.sum(jnp.where(in_seg, (start + first)[:, None, :], 0), axis=2) + j[None, :] * rows)
        counts.append(seg_end[:, N_EXPERTS - 1:])
        done = done + per_seg * rows
    tab = jnp.concatenate(counts + lists, axis=1).astype(jnp.int32).reshape(-1)

    region_end = region_start[:N_EXPERTS] + per_expert
    tail = region_start[N_EXPERTS:]
    gaps = jnp.concatenate(
        [region_end, (region_start[1:] - region_end) // SEG_ROWS, tail, (n_rows - tail) // tm]
    ).astype(jnp.int32)

    tile_start = jnp.arange(n_rows // tm, dtype=jnp.int32) * tm
    expert = jnp.clip(
        jnp.sum((region_start[None, :] <= tile_start[:, None]).astype(jnp.int32), axis=1) - 1, 0, N_EXPERTS - 1
    )
    tile_end = jnp.sum(jnp.where(expert[:, None] == jnp.arange(N_EXPERTS), region_end[None, :], 0), axis=1)
    used = (tile_start < tile_end).astype(jnp.int32)
    fresh = jnp.concatenate([jnp.ones((1,), jnp.int32), (expert[1:] != expert[:-1]).astype(jnp.int32)])
    experts = jnp.arange(N_EXPERTS, dtype=jnp.int32)
    later = (experts[None, :] > expert[:, None]) & (per_expert[None, :] > 0)
    nxt = jnp.min(jnp.where(later, experts[None, :], N_EXPERTS), axis=1)
    nxt = jnp.where(nxt < N_EXPERTS, nxt, -1).astype(jnp.int32)
    return tab, gaps, (expert, used, fresh, nxt), local_start.astype(jnp.int32)


def _block_diag(w):
    h, dh, _ = w.shape
    eye = jnp.eye(h, dtype=w.dtype)
    return (eye[:, None, :, None] * w[:, :, None, :]).reshape(h * dh, h * dh)


ROW_TM = 512
OUT_PROJ_TM = 256
IN_PROJ_TN = 1024
PROMPT_T = 256
SAMPLE_SEQS = 16
EXPERT_TM = 256


def kernel(x_prompt, x_sample, state_lru_conv, state_lru_h, state_conf_conv, meta_tokens, norm1_g, w_in, lru_conv_w, lru_conv_b, lru_wa, lru_ba, lru_wi, lru_bi, lru_lambda, conf_conv_w, conf_conv_b, conf_ln_g, conf_ln_b, out_norm_a, out_norm_b, w_out, norm2_g, router_w, router_bias, exp_w_gate, exp_w_up, exp_w_down, sh_w_gate, sh_w_up, sh_w_down, final_norm_g):
    b_p, seq, d = x_prompt.shape
    b_s, t_s, _ = x_sample.shape
    n_p = b_p * seq
    n_s = b_s * t_s
    n = n_p + n_s
    x_parts = (x_prompt.reshape(n_p, d), x_sample.reshape(n_s, d))

    row = lambda v: v.reshape(1, -1)
    w_in16 = w_in[0].astype(BF16)
    mixer_w = (
        lru_conv_w[0], row(lru_conv_b[0]),
        _block_diag(lru_wa[0]).astype(BF16), row(lru_ba[0]),
        _block_diag(lru_wi[0]).astype(BF16), row(lru_bi[0]),
        row(lru_lambda[0]),
        conf_conv_w[0], row(conf_conv_b[0]), row(conf_ln_g[0]), row(conf_ln_b[0]),
        row(out_norm_a[0]), row(out_norm_b[0]),
    )

    proj_m = _in_proj((meta_tokens,), row(norm1_g[0]), w_in16, tm=N_META, tn=IN_PROJ_TN)
    _, m_lc, m_h, m_cc = _mixer(
        proj_m, 0,
        jnp.zeros((1, LRU_CONV - 1, W_A), F32), jnp.zeros((1, 1, W_A), F32), jnp.zeros((1, CONF_KERNEL - 1, W_B), F32),
        mixer_w, n_seq=1, n_t=N_META, n_tiles=1,
    )

    proj = _in_proj(x_parts, row(norm1_g[0]), w_in16, tm=ROW_TM, tn=IN_PROJ_TN)
    y_p, p_lc, p_h, p_cc = _mixer(
        proj, 0,
        jnp.broadcast_to(m_lc, (b_p,) + m_lc.shape[1:]), jnp.broadcast_to(m_h, (b_p,) + m_h.shape[1:]),
        jnp.broadcast_to(m_cc, (b_p,) + m_cc.shape[1:]),
        mixer_w, n_seq=1, n_t=PROMPT_T, n_tiles=seq // PROMPT_T,
    )
    y_s, s_lc, s_h, s_cc = _mixer(
        proj, n_p // (SAMPLE_SEQS * t_s),
        state_lru_conv[0], state_lru_h[0].reshape(b_s, 1, W_A), state_conf_conv[0],
        mixer_w, n_seq=SAMPLE_SEQS, n_t=t_s, n_tiles=1,
    )

    x1, xn2, logits = _out_proj(
        (y_p, y_s), x_parts, w_out[0].astype(BF16), row(norm2_g[0]), router_w[0].astype(BF16), tm=OUT_PROJ_TM
    )

    eidx_t, gate_t, rank_t, block_counts = _route(logits.T, router_bias[0].reshape(N_EXPERTS, 1), tl=TOKEN_BLOCK)
    n_blocks = n // TOKEN_BLOCK
    n_sorted = n_blocks * (TOKEN_BLOCK * TOP_K + N_EXPERTS * (SEG_ROWS - 1)) + N_EXPERTS * (EXPERT_TM - 1)
    n_sorted = -(-n_sorted // EXPERT_TM) * EXPERT_TM
    tab, gaps, tiles, local_start = _segment_tables(block_counts[:, :, 0], n_sorted, tm=EXPERT_TM)
    expert_one_hot = eidx_t[:, :, None] == jnp.arange(N_EXPERTS, dtype=jnp.int32)
    token_start = jnp.repeat(local_start, TOKEN_BLOCK, axis=0)
    pos_t = jnp.sum(jnp.where(expert_one_hot, token_start[None], 0), axis=2) + rank_t

    xs = _dispatch(tab, gaps, pos_t, gate_t, xn2, n_sorted, tb=TOKEN_BLOCK, zero_rows=EXPERT_TM)
    ys = _experts(tiles, xs, exp_w_gate[0], exp_w_up[0], exp_w_down[0], tm=EXPERT_TM)
    out_p, out_s = _combine(
        tab, pos_t.T, x1, xn2,
        sh_w_gate[0].astype(BF16), sh_w_up[0].astype(BF16), sh_w_down[0].astype(BF16),
        row(final_norm_g), ys, (n_p, n_s), tb=TOKEN_BLOCK,
    )

    return (
        out_p.reshape(b_p, seq, d), out_s.reshape(b_s, t_s, d),
        p_lc[None], p_h.reshape(1, b_p, W_A), p_cc[None],
        s_lc[None], s_h.reshape(1, b_s, W_A), s_cc[None],
    )
```

```python
import functools

import jax
import jax.numpy as jnp
from jax import lax
from jax.experimental import pallas as pl
from jax.experimental.pallas import tpu as pltpu

D_MODEL = 2048
N_META = 16
W_A = 1024
W_B = 1024
LRU_CONV = 4
LRU_C = 8.0
CONF_KERNEL = 31
N_EXPERTS = 64
N_EXPERT_GROUPS = 8
EXPERTS_PER_GROUP = 8
TOPK_GROUPS = 4
TOP_K = 8
ROUTED_SCALE = 2.5
EPS = 1e-6

SUBLANES = 8
LANES = 128
TOKEN_BLOCK = 256
SEG_ROWS = SUBLANES
LOCAL_ROWS = 2560
SORT_CHUNK = 512
HOT_PIECE = 16
CHUNK_ROWS = (SEG_ROWS,)
CHUNK_MAX = (LOCAL_ROWS // SEG_ROWS,)
ISSUE_UNROLL = 4
WAIT_GROUP = 32
assert LOCAL_ROWS >= TOKEN_BLOCK * TOP_K + N_EXPERTS * (SEG_ROWS - 1) and LOCAL_ROWS % SORT_CHUNK == 0
LRU_HIST = SUBLANES
CONF_HIST = 32
VMEM_LIMIT = 56 * 1024 * 1024

F32 = jnp.float32
BF16 = jnp.bfloat16


def _params(semantics):
    return pltpu.CompilerParams(dimension_semantics=semantics, vmem_limit_bytes=VMEM_LIMIT)


def _rms_norm(x, g):
    return x * lax.rsqrt(jnp.mean(x * x, axis=-1, keepdims=True) + EPS) * g


def _part_starts(n_blocks):
    starts, s = [], 0
    for nb in n_blocks:
        starts.append(s)
        s += nb
    return starts


def _part_map(start, nb, grid_axis):
    def index_map(*ids):
        return (jnp.clip(ids[grid_axis] - start, 0, nb - 1), 0)

    return index_map


def _for_part(i, n_blocks, body):
    for p, (start, nb) in enumerate(zip(_part_starts(n_blocks), n_blocks)):
        pl.when((i >= start) & (i < start + nb))(functools.partial(body, p))


def _in_proj_kernel(*refs, n_blocks):
    x_refs = refs[:len(n_blocks)]
    g_ref, w_ref, o_ref, w16 = refs[len(n_blocks):]

    @pl.when(pl.program_id(1) == 0)
    def _():
        w16[...] = w_ref[...].astype(BF16)

    def body(p):
        xn = _rms_norm(x_refs[p][...], g_ref[...])
        o_ref[...] = jnp.dot(xn.astype(BF16), w16[...], preferred_element_type=F32)

    _for_part(pl.program_id(1), n_blocks, body)


def _in_proj(x_parts, g, w, *, tm, tn):
    d, f = w.shape
    n_blocks = tuple(x.shape[0] // tm for x in x_parts)
    x_specs = [
        pl.BlockSpec((tm, d), _part_map(start, nb, 1)) for start, nb in zip(_part_starts(n_blocks), n_blocks)
    ]
    return pl.pallas_call(
        functools.partial(_in_proj_kernel, n_blocks=n_blocks),
        grid=(f // tn, sum(n_blocks)),
        in_specs=x_specs + [
            pl.BlockSpec((1, d), lambda j, i: (0, 0)),
            pl.BlockSpec((d, tn), lambda j, i: (0, j)),
        ],
        out_specs=pl.BlockSpec((tm, tn), lambda j, i: (i, j)),
        out_shape=jax.ShapeDtypeStruct((sum(n_blocks) * tm, f), F32),
        scratch_shapes=[pltpu.VMEM((d, tn), BF16)],
        compiler_params=_params(("arbitrary", "arbitrary")),
        name="in_proj",
    )(*x_parts, g, w)


def _causal_conv(ext, w_ref, n_t, hist):
    taps = w_ref.shape[0]
    length = ext.shape[1]
    shifted = {0: ext}
    acc = None
    for j in range(taps):
        q, r = divmod(hist - (taps - 1) + j, SUBLANES)
        if r not in shifted:
            shifted[r] = pltpu.roll(ext, length - r, 1)
        term = w_ref[pl.ds(j, 1), :][None] * shifted[r][:, q * SUBLANES:q * SUBLANES + n_t, :]
        acc = term if acc is None else acc + term
    return acc


def _mixer_kernel(
    proj_ref, lc_ref, h0_ref, cc_ref,
    wca_ref, bca_ref, wa_ref, ba_ref, wi_ref, bi_ref, lam_ref,
    wcb_ref, bcb_ref, lng_ref, lnb_ref, nag_ref, nbg_ref,
    y_ref, lc_out, h_out, cc_out,
    ua_ext, glu_ext, hst, a_scr, u_scr, h_scr,
    *, n_seq, n_t,
):
    t = pl.program_id(1)
    rows = n_seq * n_t
    c = W_A

    @pl.when(t == 0)
    def _():
        ua_ext[:, pl.ds(0, LRU_HIST), :] = jnp.zeros((n_seq, LRU_HIST, c), F32)
        ua_ext[:, pl.ds(LRU_HIST - (LRU_CONV - 1), LRU_CONV - 1), :] = lc_ref[...]
        glu_ext[:, pl.ds(0, CONF_HIST), :] = jnp.zeros((n_seq, CONF_HIST, c), F32)
        glu_ext[:, pl.ds(CONF_HIST - (CONF_KERNEL - 1), CONF_KERNEL - 1), :] = cc_ref[...]
        hst[...] = h0_ref[...]

    ua_ext[:, pl.ds(LRU_HIST, n_t), :] = proj_ref[:, pl.ds(0, c)].reshape(n_seq, n_t, c)
    c_a = _causal_conv(ua_ext[...], wca_ref, n_t, LRU_HIST)
    c_a = (c_a + bca_ref[...][None]).reshape(rows, c)
    c_a16 = c_a.astype(BF16)
    r = jax.nn.sigmoid(jnp.dot(c_a16, wa_ref[...], preferred_element_type=F32) + ba_ref[...])
    i = jax.nn.sigmoid(jnp.dot(c_a16, wi_ref[...], preferred_element_type=F32) + bi_ref[...])
    neg_lam = -lam_ref[...]
    softplus = jnp.maximum(neg_lam, 0.0) + jnp.log1p(jnp.exp(-jnp.abs(neg_lam)))
    log_a = (-LRU_C * r) * softplus
    a = jnp.exp(log_a)
    a_scr[...] = a
    u_scr[...] = jnp.sqrt(-jnp.tanh(log_a) * (a * a + 1.0)) * (i * c_a)

    row8 = lax.broadcasted_iota(jnp.int32, (SUBLANES, c), 0)
    groups = n_t // SUBLANES
    for s in range(n_seq):
        def scan_group(g, carry, s=s):
            off = pl.multiple_of(s * n_t + g * SUBLANES, SUBLANES)
            a = a_scr[pl.ds(off, SUBLANES), :]
            u = u_scr[pl.ds(off, SUBLANES), :]
            for d in (1, 2, 4):
                keep = row8 >= d
                a_sh = jnp.where(keep, pltpu.roll(a, d, 0), 1.0)
                u_sh = jnp.where(keep, pltpu.roll(u, d, 0), 0.0)
                u = a * u_sh + u
                a = a * a_sh
            h = a * carry + u
            h_scr[pl.ds(off, SUBLANES), :] = h
            return jnp.broadcast_to(h[SUBLANES - 1:SUBLANES, :], (SUBLANES, c))

        carry = lax.fori_loop(0, groups, scan_group, jnp.broadcast_to(hst[s], (SUBLANES, c)))
        hst[s] = carry[0:1, :]

    y_a = jax.nn.gelu(proj_ref[:, pl.ds(c, c)]) * h_scr[...]
    y_ref[:, pl.ds(0, c)] = _rms_norm(y_a, nag_ref[...]).astype(y_ref.dtype)

    glu = proj_ref[:, pl.ds(2 * c, c)] * jax.nn.sigmoid(proj_ref[:, pl.ds(3 * c, c)])
    glu_ext[:, pl.ds(CONF_HIST, n_t), :] = glu.reshape(n_seq, n_t, c)
    c_b = _causal_conv(glu_ext[...], wcb_ref, n_t, CONF_HIST)
    c_b = (c_b + bcb_ref[...][None]).reshape(rows, c)
    mu = jnp.mean(c_b, axis=-1, keepdims=True)
    cen = c_b - mu
    var = jnp.mean(cen * cen, axis=-1, keepdims=True)
    ln = cen * lax.rsqrt(var + EPS) * lng_ref[...] + lnb_ref[...]
    y_b = ln * jax.nn.sigmoid(ln)
    y_ref[:, pl.ds(c, c)] = _rms_norm(y_b, nbg_ref[...]).astype(y_ref.dtype)

    lc_out[...] = ua_ext[:, pl.ds(LRU_HIST + n_t - (LRU_CONV - 1), LRU_CONV - 1), :]
    cc_out[...] = glu_ext[:, pl.ds(CONF_HIST + n_t - (CONF_KERNEL - 1), CONF_KERNEL - 1), :]
    h_out[...] = hst[...]
    ua_ext[:, pl.ds(0, LRU_HIST), :] = ua_ext[:, pl.ds(n_t, LRU_HIST), :]
    glu_ext[:, pl.ds(0, CONF_HIST), :] = glu_ext[:, pl.ds(n_t, CONF_HIST), :]


def _mixer(proj, row_block0, lc, h0, cc, weights, *, n_seq, n_t, n_tiles):
    b = lc.shape[0]
    c = W_A
    rows = n_seq * n_t
    n_sb = b // n_seq

    def row_map(sb, t):
        return (row_block0 + sb * n_tiles + t, 0)

    def const2(sb, t):
        return (0, 0)

    def seq_map(sb, t):
        return (sb, 0, 0)

    w_specs = [pl.BlockSpec(w.shape, const2) for w in weights]
    return pl.pallas_call(
        functools.partial(_mixer_kernel, n_seq=n_seq, n_t=n_t),
        grid=(n_sb, n_tiles),
        in_specs=[
            pl.BlockSpec((rows, 4 * c), row_map),
            pl.BlockSpec((n_seq, LRU_CONV - 1, c), seq_map),
            pl.BlockSpec((n_seq, 1, c), seq_map),
            pl.BlockSpec((n_seq, CONF_KERNEL - 1, c), seq_map),
        ] + w_specs,
        out_specs=[
            pl.BlockSpec((rows, 2 * c), lambda sb, t: (sb * n_tiles + t, 0)),
            pl.BlockSpec((n_seq, LRU_CONV - 1, c), seq_map),
            pl.BlockSpec((n_seq, 1, c), seq_map),
            pl.BlockSpec((n_seq, CONF_KERNEL - 1, c), seq_map),
        ],
        out_shape=[
            jax.ShapeDtypeStruct((b * n_tiles * n_t, 2 * c), BF16),
            jax.ShapeDtypeStruct((b, LRU_CONV - 1, c), F32),
            jax.ShapeDtypeStruct((b, 1, c), F32),
            jax.ShapeDtypeStruct((b, CONF_KERNEL - 1, c), F32),
        ],
        scratch_shapes=[
            pltpu.VMEM((n_seq, LRU_HIST + n_t, c), F32),
            pltpu.VMEM((n_seq, CONF_HIST + n_t, c), F32),
            pltpu.VMEM((n_seq, 1, c), F32),
            pltpu.VMEM((rows, c), F32),
            pltpu.VMEM((rows, c), F32),
            pltpu.VMEM((rows, c), F32),
        ],
        compiler_params=_params(("arbitrary", "arbitrary")),
        name="mixer",
    )(proj, lc, h0, cc, *weights)


def _out_proj_kernel(*refs, n_blocks):
    k = len(n_blocks)
    y_refs, x_refs = refs[:k], refs[k:2 * k]
    w_ref, g_ref, rw_ref, x1_ref, xn_ref, logit_ref = refs[2 * k:]

    def body(p):
        x1 = x_refs[p][...] + jnp.dot(y_refs[p][...], w_ref[...], preferred_element_type=F32)
        x1_ref[...] = x1
        xn = _rms_norm(x1, g_ref[...]).astype(BF16)
        xn_ref[...] = xn
        logit_ref[...] = lax.dot_general(rw_ref[...], xn, (((1,), (1,)), ((), ())), preferred_element_type=F32)

    _for_part(pl.program_id(0), n_blocks, body)


def _out_proj(y_parts, x_parts, w_out, g2, router_w_t, *, tm):
    d = w_out.shape[0]
    e = router_w_t.shape[0]
    n_blocks = tuple(x.shape[0] // tm for x in x_parts)
    part_specs = [
        pl.BlockSpec((tm, d), _part_map(start, nb, 0)) for start, nb in zip(_part_starts(n_blocks), n_blocks)
    ]
    n = sum(n_blocks) * tm
    const = lambda i: (0, 0)
    row = lambda i: (i, 0)
    return pl.pallas_call(
        functools.partial(_out_proj_kernel, n_blocks=n_blocks),
        grid=(sum(n_blocks),),
        in_specs=part_specs + part_specs + [
            pl.BlockSpec((d, d), const),
            pl.BlockSpec((1, d), const),
            pl.BlockSpec((e, d), const),
        ],
        out_specs=[pl.BlockSpec((tm, d), row), pl.BlockSpec((tm, d), row), pl.BlockSpec((e, tm), lambda i: (0, i))],
        out_shape=[
            jax.ShapeDtypeStruct((n, d), F32),
            jax.ShapeDtypeStruct((n, d), BF16),
            jax.ShapeDtypeStruct((e, n), F32),
        ],
        compiler_params=_params(("arbitrary",)),
        name="out_proj",
    )(*y_parts, *x_parts, w_out, g2, router_w_t)


def _first_argmax(work, index, sentinel):
    m = jnp.max(work, axis=0, keepdims=True)
    return jnp.min(jnp.where(work == m, index, sentinel), axis=0, keepdims=True)


def _route_kernel(logit_ref, bias_ref, eidx_ref, gate_ref, rank_ref, count_ref):
    n_tok = logit_ref.shape[1]

    scores = jax.nn.sigmoid(logit_ref[...])
    biased = scores + bias_ref[...]
    grouped = biased.reshape(N_EXPERT_GROUPS, EXPERTS_PER_GROUP, n_tok)
    in_group = lax.broadcasted_iota(jnp.int32, grouped.shape, 1)
    top1 = jnp.max(grouped, axis=1, keepdims=True)
    first1 = jnp.min(jnp.where(grouped == top1, in_group, EXPERTS_PER_GROUP), axis=1, keepdims=True)
    top2 = jnp.max(jnp.where(in_group == first1, -jnp.inf, grouped), axis=1, keepdims=True)
    group_scores = (top1 + top2).reshape(N_EXPERT_GROUPS, n_tok)

    group_id = lax.broadcasted_iota(jnp.int32, group_scores.shape, 0)
    group_on = jnp.zeros(group_scores.shape, F32)
    work = group_scores
    for _ in range(TOPK_GROUPS):
        pick = group_id == _first_argmax(work, group_id, N_EXPERT_GROUPS)
        group_on = jnp.where(pick, 1.0, group_on)
        work = jnp.where(pick, -jnp.inf, work)

    masked = jnp.where(group_on.reshape(N_EXPERT_GROUPS, 1, n_tok) > 0.0, grouped, -jnp.inf)
    work = masked.reshape(N_EXPERTS, n_tok)
    expert_id = lax.broadcasted_iota(jnp.int32, work.shape, 0)
    ids, sel = [], []
    chosen = jnp.zeros(work.shape, F32)
    for _ in range(TOP_K):
        first = _first_argmax(work, expert_id, N_EXPERTS)
        pick = expert_id == first
        ids.append(first)
        sel.append(jnp.sum(jnp.where(pick, scores, 0.0), axis=0, keepdims=True))
        chosen = jnp.where(pick, 1.0, chosen)
        work = jnp.where(pick, -jnp.inf, work)
    sel = jnp.concatenate(sel, axis=0)
    eidx_ref[...] = jnp.concatenate(ids, axis=0)
    gate_ref[...] = sel / (jnp.sum(sel, axis=0, keepdims=True) + 1e-20) * ROUTED_SCALE

    earlier = lax.broadcasted_iota(jnp.int32, (n_tok, n_tok), 0) < lax.broadcasted_iota(jnp.int32, (n_tok, n_tok), 1)
    before = jnp.dot(chosen.astype(BF16), earlier.astype(BF16), preferred_element_type=F32)
    rank_ref[...] = jnp.concatenate(
        [jnp.sum(jnp.where(expert_id == ids[k], before, 0.0), axis=0, keepdims=True) for k in range(TOP_K)], axis=0
    ).astype(jnp.int32)
    count_ref[0] = jnp.sum(chosen, axis=1, keepdims=True).astype(jnp.int32)


def _route(logits_t, bias, *, tl):
    e, n = logits_t.shape
    pick_spec = pl.BlockSpec((TOP_K, tl), lambda i: (0, i))
    return pl.pallas_call(
        _route_kernel,
        grid=(n // tl,),
        in_specs=[pl.BlockSpec((e, tl), lambda i: (0, i)), pl.BlockSpec((e, 1), lambda i: (0, 0))],
        out_specs=[pick_spec, pick_spec, pick_spec, pl.BlockSpec((1, e, 1), lambda i: (i, 0, 0))],
        out_shape=[
            jax.ShapeDtypeStruct((TOP_K, n), jnp.int32),
            jax.ShapeDtypeStruct((TOP_K, n), F32),
            jax.ShapeDtypeStruct((TOP_K, n), jnp.int32),
            jax.ShapeDtypeStruct((n // tl, e, 1), jnp.int32),
        ],
        compiler_params=_params(("arbitrary",)),
        name="route",
    )(logits_t, bias)


def _gate_lanes(gate):
    g1 = gate.astype(BF16).astype(F32)
    rest = gate - g1
    g2 = rest.astype(BF16).astype(F32)
    g3 = (rest - g2).astype(BF16).astype(F32)
    lane = lax.broadcasted_iota(jnp.int32, (gate.shape[0], LANES), 1)
    return jnp.where(lane == 0, g1, jnp.where(lane == 1, g2, jnp.where(lane == 2, g3, 0.0))).astype(BF16)


def _tab_offsets():
    offsets, col = [], len(CHUNK_ROWS)
    for cap in CHUNK_MAX:
        offsets.append((col, col + cap))
        col += 2 * cap
    return offsets, col


def _segment_copies(tab_ref, block, make_copy):
    offsets, width = _tab_offsets()
    base = block * width
    counts = []
    for k, (rows, (local_col, global_col)) in enumerate(zip(CHUNK_ROWS, offsets)):
        n_chunks = tab_ref[base + k]

        def start_one(j, rows=rows, local_col=local_col, global_col=global_col):
            make_copy(
                pl.multiple_of(tab_ref[base + local_col + j], SEG_ROWS),
                pl.multiple_of(tab_ref[base + global_col + j], SEG_ROWS),
                rows,
            ).start()

        def per_group(g, carry):
            for u in range(ISSUE_UNROLL):
                start_one(g * ISSUE_UNROLL + u)
            return carry

        def per_chunk(j, carry):
            start_one(j)
            return carry

        n_groups = n_chunks // ISSUE_UNROLL
        lax.fori_loop(0, n_groups, per_group, 0)
        lax.fori_loop(n_groups * ISSUE_UNROLL, n_chunks, per_chunk, 0)
        counts.append(n_chunks)
    return counts


def _wait_copies(counts, make_copy):
    for rows, count in zip(CHUNK_ROWS, counts):
        def wait_group(c, carry, rows=rows):
            make_copy(0, 0, rows * WAIT_GROUP).wait()
            return carry

        def wait_one(c, carry, rows=rows):
            make_copy(0, 0, rows).wait()
            return carry

        n_groups = count // WAIT_GROUP
        lax.fori_loop(0, n_groups, wait_group, 0)
        lax.fori_loop(n_groups * WAIT_GROUP, count, wait_one, 0)


def _dispatch_kernel(tab_ref, gap_ref, pos_ref, gate_ref, x_ref, xs_hbm, sorted_scr, zero_scr, pending, sem):
    step = pl.program_id(0)
    slot = step % 2
    d = x_ref.shape[1]
    n_tok = x_ref.shape[0]
    pos = pos_ref[...]
    gate = gate_ref[...]
    x = x_ref[...]
    piece_id = lax.broadcasted_iota(jnp.int32, (HOT_PIECE, n_tok), 0)
    pos_k = [jnp.broadcast_to(pos[k:k + 1, :], (HOT_PIECE, n_tok)) for k in range(TOP_K)]
    gate_k = [jnp.broadcast_to(gate[k:k + 1, :], (HOT_PIECE, n_tok)) for k in range(TOP_K)]
    for c in range(LOCAL_ROWS // SORT_CHUNK):
        one_hot, row_gate = [], []
        for p in range(SORT_CHUNK // HOT_PIECE):
            row_id = piece_id + (c * SORT_CHUNK + p * HOT_PIECE)
            hot = jnp.zeros((HOT_PIECE, n_tok), F32)
            gate_hit = jnp.zeros((HOT_PIECE, n_tok), F32)
            for k in range(TOP_K):
                hit = row_id == pos_k[k]
                hot = jnp.where(hit, 1.0, hot)
                gate_hit = jnp.where(hit, gate_k[k], gate_hit)
            one_hot.append(hot.astype(BF16))
            row_gate.append(jnp.sum(gate_hit, axis=1, keepdims=True))
        rows = jnp.dot(jnp.concatenate(one_hot, axis=0), x, preferred_element_type=F32)
        out_rows = pl.ds(c * SORT_CHUNK, SORT_CHUNK)
        sorted_scr[slot, out_rows, pl.ds(0, d)] = rows.astype(BF16)
        sorted_scr[slot, out_rows, pl.ds(d, LANES)] = _gate_lanes(jnp.concatenate(row_gate, axis=0))

    def copy_chunk(local_row, global_row, rows):
        return pltpu.make_async_copy(
            sorted_scr.at[slot, pl.ds(local_row, rows), :], xs_hbm.at[pl.ds(global_row, rows), :], sem
        )

    @pl.when(step > 0)
    def _():
        _wait_copies([pending[k] for k in range(len(CHUNK_ROWS))], copy_chunk)

    for k, count in enumerate(_segment_copies(tab_ref, step, copy_chunk)):
        pending[k] = count

    @pl.when(step == pl.num_programs(0) - 1)
    def _():
        zero_scr[...] = jnp.zeros(zero_scr.shape, BF16)
        tile_rows = zero_scr.shape[0]

        def zero_chunk(row):
            return pltpu.make_async_copy(
                zero_scr.at[pl.ds(0, SEG_ROWS), :],
                xs_hbm.at[pl.ds(pl.multiple_of(row, SEG_ROWS), SEG_ROWS), :],
                sem,
            )

        def zero_tile(row):
            return pltpu.make_async_copy(
                zero_scr, xs_hbm.at[pl.ds(pl.multiple_of(row, tile_rows), tile_rows), :], sem
            )

        def per_gap(g, total):
            start = gap_ref[g]
            chunks = gap_ref[N_EXPERTS + g]

            def per_chunk(c, carry):
                zero_chunk(start + c * SEG_ROWS).start()
                return carry

            lax.fori_loop(0, chunks, per_chunk, 0)
            return total + chunks

        n_fill = lax.fori_loop(0, N_EXPERTS, per_gap, 0)
        tail = gap_ref[2 * N_EXPERTS]
        n_tail = gap_ref[2 * N_EXPERTS + 1]

        def start_tile(c, carry):
            zero_tile(tail + c * tile_rows).start()
            return carry

        def wait_tile(c, carry):
            zero_tile(0).wait()
            return carry

        def wait_chunk(c, carry):
            zero_chunk(0).wait()
            return carry

        lax.fori_loop(0, n_tail, start_tile, 0)
        _wait_copies([pending[k] for k in range(len(CHUNK_ROWS))], copy_chunk)
        lax.fori_loop(0, n_fill, wait_chunk, 0)
        lax.fori_loop(0, n_tail, wait_tile, 0)


def _dispatch(tab, gaps, pos_t, gate_t, xn, n_sorted, *, tb, zero_rows):
    n, d = xn.shape
    width = d + LANES
    pick_spec = pl.BlockSpec((TOP_K, tb), lambda i, tab, gaps: (0, i))
    return pl.pallas_call(
        _dispatch_kernel,
        grid_spec=pltpu.PrefetchScalarGridSpec(
            num_scalar_prefetch=2,
            grid=(n // tb,),
            in_specs=[pick_spec, pick_spec, pl.BlockSpec((tb, d), lambda i, tab, gaps: (i, 0))],
            out_specs=pl.BlockSpec(memory_space=pl.ANY),
            scratch_shapes=[
                pltpu.VMEM((2, LOCAL_ROWS, width), BF16),
                pltpu.VMEM((zero_rows, width), BF16),
                pltpu.SMEM((len(CHUNK_ROWS),), jnp.int32),
                pltpu.SemaphoreType.DMA,
            ],
        ),
        out_shape=jax.ShapeDtypeStruct((n_sorted, width), BF16),
        compiler_params=_params(("arbitrary",)),
        name="dispatch",
    )(tab, gaps, pos_t, gate_t, xn)


def _experts_kernel(
    expert_ref, used_ref, fresh_ref, next_ref, xs_ref, wg_hbm, wu_hbm, wd_hbm, ys_ref,
    wg32, wu32, wd32, wg16, wu16, wd16, sems,
):
    it = pl.program_id(0)
    d = ys_ref.shape[1]

    def weight_copies(e):
        return (
            pltpu.make_async_copy(wg_hbm.at[e], wg32, sems.at[0]),
            pltpu.make_async_copy(wu_hbm.at[e], wu32, sems.at[1]),
            pltpu.make_async_copy(wd_hbm.at[e], wd32, sems.at[2]),
        )

    @pl.when(it == 0)
    def _():
        for copy in weight_copies(expert_ref[0]):
            copy.start()

    @pl.when(used_ref[it] == 0)
    def _():
        ys_ref[...] = jnp.zeros(ys_ref.shape, ys_ref.dtype)

    @pl.when((used_ref[it] > 0) & (fresh_ref[it] == 1))
    def _():
        for copy in weight_copies(expert_ref[it]):
            copy.wait()
        wg16[...] = wg32[...].astype(BF16)
        wu16[...] = wu32[...].astype(BF16)
        wd16[...] = wd32[...].astype(BF16)

        @pl.when(next_ref[it] >= 0)
        def _():
            for copy in weight_copies(next_ref[it]):
                copy.start()

    tm = ys_ref.shape[0]
    for n_sub in range(1, tm // EXPERT_SUB + 1):
        rows = n_sub * EXPERT_SUB

        @pl.when(used_ref[it] == n_sub)
        def _(rows=rows):
            x = xs_ref[pl.ds(0, rows), pl.ds(0, d)]
            gate = jnp.sum(xs_ref[pl.ds(0, rows), pl.ds(d, LANES)].astype(F32), axis=1, keepdims=True)
            g = jnp.dot(x, wg16[...], preferred_element_type=F32)
            u = jnp.dot(x, wu16[...], preferred_element_type=F32)
            h = g * jax.nn.sigmoid(g) * u * gate
            y = jnp.dot(h.astype(BF16), wd16[...], preferred_element_type=F32)
            ys_ref[pl.ds(0, rows), :] = y.astype(ys_ref.dtype)
            if rows < tm:
                ys_ref[pl.ds(rows, tm - rows), :] = jnp.zeros((tm - rows, d), ys_ref.dtype)


def _experts(tiles, xs, w_gate, w_up, w_down, *, tm):
    p = xs.shape[0]
    d, f = w_gate.shape[1:]
    row_map = lambda i, expert, used, fresh, nxt: (i, 0)
    hbm = pl.BlockSpec(memory_space=pl.ANY)
    return pl.pallas_call(
        _experts_kernel,
        grid_spec=pltpu.PrefetchScalarGridSpec(
            num_scalar_prefetch=4,
            grid=(p // tm,),
            in_specs=[pl.BlockSpec((tm, xs.shape[1]), row_map), hbm, hbm, hbm],
            out_specs=pl.BlockSpec((tm, d), row_map),
            scratch_shapes=[
                pltpu.VMEM((d, f), F32), pltpu.VMEM((d, f), F32), pltpu.VMEM((f, d), F32),
                pltpu.VMEM((d, f), BF16), pltpu.VMEM((d, f), BF16), pltpu.VMEM((f, d), BF16),
                pltpu.SemaphoreType.DMA((3,)),
            ],
        ),
        out_shape=jax.ShapeDtypeStruct((p, d), BF16),
        compiler_params=_params(("arbitrary",)),
        name="experts",
    )(*tiles, xs, w_gate, w_up, w_down)


def _combine_kernel(tab_ref, pos_ref, x1_ref, xn_ref, sg_ref, su_ref, sd_ref, fg_ref, ys_hbm, *rest, n_blocks):
    o_refs = rest[:len(n_blocks)]
    local_scr, pending, sems = rest[len(n_blocks):]
    n_tok = x1_ref.shape[0]
    step = pl.program_id(0)
    slot = step % 2

    n_sizes = len(CHUNK_ROWS)

    def copy_chunk_into(dst_slot):
        def copy_chunk(local_row, global_row, rows):
            return pltpu.make_async_copy(
                ys_hbm.at[pl.ds(global_row, rows), :],
                local_scr.at[dst_slot, pl.ds(local_row, rows), :],
                sems.at[dst_slot],
            )

        return copy_chunk

    def fetch(block, dst_slot):
        for k, count in enumerate(_segment_copies(tab_ref, block, copy_chunk_into(dst_slot))):
            pending[dst_slot * n_sizes + k] = count

    @pl.when(step == 0)
    def _():
        local_scr[...] = jnp.zeros(local_scr.shape, local_scr.dtype)
        fetch(0, 0)

    @pl.when(step + 1 < pl.num_programs(0))
    def _():
        fetch(step + 1, 1 - slot)

    xn = xn_ref[...]
    sg = jnp.dot(xn, sg_ref[...], preferred_element_type=F32)
    su = jnp.dot(xn, su_ref[...], preferred_element_type=F32)
    acc = x1_ref[...] + jnp.dot(
        (sg * jax.nn.sigmoid(sg) * su).astype(BF16), sd_ref[...], preferred_element_type=F32
    )

    piece_id = lax.broadcasted_iota(jnp.int32, (HOT_PIECE, LANES), 1)
    one_hot = []
    for r in range(n_tok // HOT_PIECE):
        pos = pos_ref[pl.ds(r * HOT_PIECE, HOT_PIECE), :]
        pos_k = [jnp.broadcast_to(pos[:, k:k + 1], (HOT_PIECE, LANES)) for k in range(TOP_K)]
        cols = []
        for c in range(LOCAL_ROWS // LANES):
            col_id = piece_id + c * LANES
            hot = jnp.zeros((HOT_PIECE, LANES), F32)
            for k in range(TOP_K):
                hot = jnp.where(col_id == pos_k[k], 1.0, hot)
            cols.append(hot.astype(BF16))
        one_hot.append(jnp.concatenate(cols, axis=1))
    one_hot = jnp.concatenate(one_hot, axis=0)

    _wait_copies([pending[slot * n_sizes + k] for k in range(n_sizes)], copy_chunk_into(slot))
    routed = jnp.dot(one_hot, local_scr[slot], preferred_element_type=F32)
    out = _rms_norm(acc + routed, fg_ref[...])

    def body(p):
        o_refs[p][...] = out

    _for_part(pl.program_id(0), n_blocks, body)


def _combine(tab, pos, x1, xn, sh_gate, sh_up, sh_down, final_g, ys, part_rows, *, tb):
    n, d = x1.shape
    f = sh_gate.shape[1]
    n_blocks = tuple(r // tb for r in part_rows)
    const = lambda i, tab: (0, 0)
    row = lambda i, tab: (i, 0)
    return pl.pallas_call(
        functools.partial(_combine_kernel, n_blocks=n_blocks),
        grid_spec=pltpu.PrefetchScalarGridSpec(
            num_scalar_prefetch=1,
            grid=(n // tb,),
            in_specs=[
                pl.BlockSpec((tb, TOP_K), row),
                pl.BlockSpec((tb, d), row),
                pl.BlockSpec((tb, d), row),
                pl.BlockSpec((d, f), const),
                pl.BlockSpec((d, f), const),
                pl.BlockSpec((f, d), const),
                pl.BlockSpec((1, d), const),
                pl.BlockSpec(memory_space=pl.ANY),
            ],
            out_specs=[
                pl.BlockSpec((tb, d), _part_map(start, nb, 0))
                for start, nb in zip(_part_starts(n_blocks), n_blocks)
            ],
            scratch_shapes=[
                pltpu.VMEM((2, LOCAL_ROWS, d), BF16),
                pltpu.SMEM((2 * len(CHUNK_ROWS),), jnp.int32),
                pltpu.SemaphoreType.DMA((2,)),
            ],
        ),
        out_shape=[jax.ShapeDtypeStruct((r, d), F32) for r in part_rows],
        compiler_params=_params(("arbitrary",)),
        name="combine",
    )(tab, pos, x1, xn, sh_gate, sh_up, sh_down, final_g, ys)


def _segment_tables(block_counts, n_rows, *, tm):
    padded = (block_counts + SEG_ROWS - 1) // SEG_ROWS * SEG_ROWS
    local_start = jnp.cumsum(padded, axis=1) - padded
    per_expert = jnp.sum(padded, axis=0)
    region = (per_expert + tm - 1) // tm * tm
    region_start = jnp.concatenate([jnp.zeros((1,), jnp.int32), jnp.cumsum(region).astype(jnp.int32)])
    global_start = region_start[None, :N_EXPERTS] + jnp.cumsum(padded, axis=0) - padded
    counts, lists = [], []
    done = jnp.zeros_like(padded)
    for rows, cap in zip(CHUNK_ROWS, CHUNK_MAX):
        per_seg = (padded - done) // rows
        seg_end = jnp.cumsum(per_seg, axis=1)
        j = jnp.arange(cap, dtype=jnp.int32)
        seg_of_chunk = jnp.sum((seg_end[:, None, :] <= j[None, :, None]).astype(jnp.int32), axis=2)
        in_seg = seg_of_chunk[:, :, None] == jnp.arange(N_EXPERTS, dtype=jnp.int32)
        first = done - (seg_end - per_seg) * rows
        for start in (local_start, global_start):
            lists.append(jnp.sum(jnp.where(in_seg, (start + first)[:, None, :], 0), axis=2) + j[None, :] * rows)
        counts.append(seg_end[:, N_EXPERTS - 1:])
        done = done + per_seg * rows
    tab = jnp.concatenate(counts + lists, axis=1).astype(jnp.int32).reshape(-1)

    region_end = region_start[:N_EXPERTS] + per_expert
    tail = region_start[N_EXPERTS:]
    gaps = jnp.concatenate(
        [region_end, (region_start[1:] - region_end) // SEG_ROWS, tail, (n_rows - tail) // tm]
    ).astype(jnp.int32)

    tile_start = jnp.arange(n_rows // tm, dtype=jnp.int32) * tm
    expert = jnp.clip(
        jnp.sum((region_start[None, :] <= tile_start[:, None]).astype(jnp.int32), axis=1) - 1, 0, N_EXPERTS - 1
    )
    tile_end = jnp.sum(jnp.where(expert[:, None] == jnp.arange(N_EXPERTS), region_end[None, :], 0), axis=1)
    used = jnp.clip((tile_end - tile_start + EXPERT_SUB - 1) // EXPERT_SUB, 0, tm // EXPERT_SUB).astype(jnp.int32)
    fresh = jnp.concatenate([jnp.ones((1,), jnp.int32), (expert[1:] != expert[:-1]).astype(jnp.int32)])
    experts = jnp.arange(N_EXPERTS, dtype=jnp.int32)
    later = (experts[None, :] > expert[:, None]) & (per_expert[None, :] > 0)
    nxt = jnp.min(jnp.where(later, experts[None, :], N_EXPERTS), axis=1)
    nxt = jnp.where(nxt < N_EXPERTS, nxt, -1).astype(jnp.int32)
    return tab, gaps, (expert, used, fresh, nxt), local_start.astype(jnp.int32)


def _block_diag(w):
    h, dh, _ = w.shape
    eye = jnp.eye(h, dtype=w.dtype)
    return (eye[:, None, :, None] * w[:, :, None, :]).reshape(h * dh, h * dh)


ROW_TM = 512
OUT_PROJ_TM = 256
IN_PROJ_TN = 1024
PROMPT_T = 256
SAMPLE_SEQS = 16
EXPERT_TM = 512
EXPERT_SUB = 256


def kernel(x_prompt, x_sample, state_lru_conv, state_lru_h, state_conf_conv, meta_tokens, norm1_g, w_in, lru_conv_w, lru_conv_b, lru_wa, lru_ba, lru_wi, lru_bi, lru_lambda, conf_conv_w, conf_conv_b, conf_ln_g, conf_ln_b, out_norm_a, out_norm_b, w_out, norm2_g, router_w, router_bias, exp_w_gate, exp_w_up, exp_w_down, sh_w_gate, sh_w_up, sh_w_down, final_norm_g):
    b_p, seq, d = x_prompt.shape
    b_s, t_s, _ = x_sample.shape
    n_p = b_p * seq
    n_s = b_s * t_s
    n = n_p + n_s
    x_parts = (x_prompt.reshape(n_p, d), x_sample.reshape(n_s, d))

    row = lambda v: v.reshape(1, -1)
    mixer_w = (
        lru_conv_w[0], row(lru_conv_b[0]),
        _block_diag(lru_wa[0]).astype(BF16), row(lru_ba[0]),
        _block_diag(lru_wi[0]).astype(BF16), row(lru_bi[0]),
        row(lru_lambda[0]),
        conf_conv_w[0], row(conf_conv_b[0]), row(conf_ln_g[0]), row(conf_ln_b[0]),
        row(out_norm_a[0]), row(out_norm_b[0]),
    )

    proj_m = _in_proj((meta_tokens,), row(norm1_g[0]), w_in[0], tm=N_META, tn=IN_PROJ_TN)
    _, m_lc, m_h, m_cc = _mixer(
        proj_m, 0,
        jnp.zeros((1, LRU_CONV - 1, W_A), F32), jnp.zeros((1, 1, W_A), F32), jnp.zeros((1, CONF_KERNEL - 1, W_B), F32),
        mixer_w, n_seq=1, n_t=N_META, n_tiles=1,
    )

    proj = _in_proj(x_parts, row(norm1_g[0]), w_in[0], tm=ROW_TM, tn=IN_PROJ_TN)
    y_p, p_lc, p_h, p_cc = _mixer(
        proj, 0,
        jnp.broadcast_to(m_lc, (b_p,) + m_lc.shape[1:]), jnp.broadcast_to(m_h, (b_p,) + m_h.shape[1:]),
        jnp.broadcast_to(m_cc, (b_p,) + m_cc.shape[1:]),
        mixer_w, n_seq=1, n_t=PROMPT_T, n_tiles=seq // PROMPT_T,
    )
    y_s, s_lc, s_h, s_cc = _mixer(
        proj, n_p // (SAMPLE_SEQS * t_s),
        state_lru_conv[0], state_lru_h[0].reshape(b_s, 1, W_A), state_conf_conv[0],
        mixer_w, n_seq=SAMPLE_SEQS, n_t=t_s, n_tiles=1,
    )

    x1, xn2, logits_t = _out_proj(
        (y_p, y_s), x_parts, w_out[0].astype(BF16), row(norm2_g[0]), router_w[0].T.astype(BF16), tm=OUT_PROJ_TM
    )

    eidx_t, gate_t, rank_t, block_counts = _route(logits_t, router_bias[0].reshape(N_EXPERTS, 1), tl=TOKEN_BLOCK)
    n_blocks = n // TOKEN_BLOCK
    n_sorted = n_blocks * (TOKEN_BLOCK * TOP_K + N_EXPERTS * (SEG_ROWS - 1)) + N_EXPERTS * (EXPERT_TM - 1)
    n_sorted = -(-n_sorted // EXPERT_TM) * EXPERT_TM
    tab, gaps, tiles, local_start = _segment_tables(block_counts[:, :, 0], n_sorted, tm=EXPERT_TM)
    expert_one_hot = eidx_t[:, :, None] == jnp.arange(N_EXPERTS, dtype=jnp.int32)
    token_start = jnp.repeat(local_start, TOKEN_BLOCK, axis=0)
    pos_t = jnp.sum(jnp.where(expert_one_hot, token_start[None], 0), axis=2) + rank_t

    xs = _dispatch(tab, gaps, pos_t, gate_t, xn2, n_sorted, tb=TOKEN_BLOCK, zero_rows=EXPERT_TM)
    ys = _experts(tiles, xs, exp_w_gate[0], exp_w_up[0], exp_w_down[0], tm=EXPERT_TM)
    out_p, out_s = _combine(
        tab, pos_t.T, x1, xn2,
        sh_w_gate[0].astype(BF16), sh_w_up[0].astype(BF16), sh_w_down[0].astype(BF16),
        row(final_norm_g), ys, (n_p, n_s), tb=TOKEN_BLOCK,
    )

    return (
        out_p.reshape(b_p, seq, d), out_s.reshape(b_s, t_s, d),
        p_lc[None], p_h.reshape(1, b_p, W_A), p_cc[None],
        s_lc[None], s_h.reshape(1, b_s, W_A), s_cc[None],
    )
```

```python
import functools

import jax
import jax.numpy as jnp
from jax import lax
from jax.experimental import pallas as pl
from jax.experimental.pallas import tpu as pltpu

D_MODEL = 2048
N_META = 16
W_A = 1024
W_B = 1024
LRU_CONV = 4
LRU_C = 8.0
CONF_KERNEL = 31
N_EXPERTS = 64
N_EXPERT_GROUPS = 8
EXPERTS_PER_GROUP = 8
TOPK_GROUPS = 4
TOP_K = 8
ROUTED_SCALE = 2.5
EPS = 1e-6

SUBLANES = 8
LANES = 128
TOKEN_BLOCK = 256
SEG_ROWS = SUBLANES
LOCAL_ROWS = 2560
SORT_CHUNK = 512
HOT_PIECE = 16
CHUNK_ROWS = (SEG_ROWS,)
CHUNK_MAX = (LOCAL_ROWS // SEG_ROWS,)
ISSUE_UNROLL = 4
WAIT_GROUP = 32
assert LOCAL_ROWS >= TOKEN_BLOCK * TOP_K + N_EXPERTS * (SEG_ROWS - 1) and LOCAL_ROWS % SORT_CHUNK == 0
LRU_HIST = SUBLANES
CONF_HIST = 32
VMEM_LIMIT = 56 * 1024 * 1024

F32 = jnp.float32
BF16 = jnp.bfloat16


def _params(semantics):
    return pltpu.CompilerParams(dimension_semantics=semantics, vmem_limit_bytes=VMEM_LIMIT)


def _rms_norm(x, g):
    return x * lax.rsqrt(jnp.mean(x * x, axis=-1, keepdims=True) + EPS) * g


def _part_starts(n_blocks):
    starts, s = [], 0
    for nb in n_blocks:
        starts.append(s)
        s += nb
    return starts


def _part_map(start, nb, grid_axis):
    def index_map(*ids):
        return (jnp.clip(ids[grid_axis] - start, 0, nb - 1), 0)

    return index_map


def _for_part(i, n_blocks, body):
    for p, (start, nb) in enumerate(zip(_part_starts(n_blocks), n_blocks)):
        pl.when((i >= start) & (i < start + nb))(functools.partial(body, p))


def _in_proj_kernel(*refs, n_blocks):
    x_refs = refs[:len(n_blocks)]
    g_ref, w_ref, o_ref, w16 = refs[len(n_blocks):]

    @pl.when(pl.program_id(1) == 0)
    def _():
        w16[...] = w_ref[...].astype(BF16)

    def body(p):
        xn = _rms_norm(x_refs[p][...], g_ref[...])
        o_ref[...] = jnp.dot(xn.astype(BF16), w16[...], preferred_element_type=F32)

    _for_part(pl.program_id(1), n_blocks, body)


def _in_proj(x_parts, g, w, *, tm, tn):
    d, f = w.shape
    n_blocks = tuple(x.shape[0] // tm for x in x_parts)
    x_specs = [
        pl.BlockSpec((tm, d), _part_map(start, nb, 1)) for start, nb in zip(_part_starts(n_blocks), n_blocks)
    ]
    return pl.pallas_call(
        functools.partial(_in_proj_kernel, n_blocks=n_blocks),
        grid=(f // tn, sum(n_blocks)),
        in_specs=x_specs + [
            pl.BlockSpec((1, d), lambda j, i: (0, 0)),
            pl.BlockSpec((d, tn), lambda j, i: (0, j), pipeline_mode=pl.Buffered(1)),
        ],
        out_specs=pl.BlockSpec((tm, tn), lambda j, i: (i, j)),
        out_shape=jax.ShapeDtypeStruct((sum(n_blocks) * tm, f), F32),
        scratch_shapes=[pltpu.VMEM((d, tn), BF16)],
        compiler_params=_params(("arbitrary", "arbitrary")),
        name="in_proj",
    )(*x_parts, g, w)


def _causal_conv(ext, w_ref, n_t, hist):
    taps = w_ref.shape[0]
    length = ext.shape[1]
    shifted = {0: ext}
    acc = None
    for j in range(taps):
        q, r = divmod(hist - (taps - 1) + j, SUBLANES)
        if r not in shifted:
            shifted[r] = pltpu.roll(ext, length - r, 1)
        term = w_ref[pl.ds(j, 1), :][None] * shifted[r][:, q * SUBLANES:q * SUBLANES + n_t, :]
        acc = term if acc is None else acc + term
    return acc


def _mixer_kernel(
    proj_ref, lc_ref, h0_ref, cc_ref,
    wca_ref, bca_ref, wa_ref, ba_ref, wi_ref, bi_ref, lam_ref,
    wcb_ref, bcb_ref, lng_ref, lnb_ref, nag_ref, nbg_ref,
    y_ref, lc_out, h_out, cc_out,
    ua_ext, glu_ext, hst, a_scr, u_scr, h_scr,
    *, n_seq, n_t,
):
    t = pl.program_id(1)
    rows = n_seq * n_t
    c = W_A

    @pl.when(t == 0)
    def _():
        ua_ext[:, pl.ds(0, LRU_HIST), :] = jnp.zeros((n_seq, LRU_HIST, c), F32)
        ua_ext[:, pl.ds(LRU_HIST - (LRU_CONV - 1), LRU_CONV - 1), :] = lc_ref[...]
        glu_ext[:, pl.ds(0, CONF_HIST), :] = jnp.zeros((n_seq, CONF_HIST, c), F32)
        glu_ext[:, pl.ds(CONF_HIST - (CONF_KERNEL - 1), CONF_KERNEL - 1), :] = cc_ref[...]
        hst[...] = h0_ref[...]

    ua_ext[:, pl.ds(LRU_HIST, n_t), :] = proj_ref[:, pl.ds(0, c)].reshape(n_seq, n_t, c)
    c_a = _causal_conv(ua_ext[...], wca_ref, n_t, LRU_HIST)
    c_a = (c_a + bca_ref[...][None]).reshape(rows, c)
    c_a16 = c_a.astype(BF16)
    r = jax.nn.sigmoid(jnp.dot(c_a16, wa_ref[...], preferred_element_type=F32) + ba_ref[...])
    i = jax.nn.sigmoid(jnp.dot(c_a16, wi_ref[...], preferred_element_type=F32) + bi_ref[...])
    neg_lam = -lam_ref[...]
    softplus = jnp.maximum(neg_lam, 0.0) + jnp.log1p(jnp.exp(-jnp.abs(neg_lam)))
    log_a = (-LRU_C * r) * softplus
    a = jnp.exp(log_a)
    a_scr[...] = a
    u_scr[...] = jnp.sqrt(-jnp.tanh(log_a) * (a * a + 1.0)) * (i * c_a)

    row8 = lax.broadcasted_iota(jnp.int32, (SUBLANES, c), 0)
    groups = n_t // SUBLANES
    for s in range(n_seq):
        def scan_group(g, carry, s=s):
            off = pl.multiple_of(s * n_t + g * SUBLANES, SUBLANES)
            a = a_scr[pl.ds(off, SUBLANES), :]
            u = u_scr[pl.ds(off, SUBLANES), :]
            for d in (1, 2, 4):
                keep = row8 >= d
                a_sh = jnp.where(keep, pltpu.roll(a, d, 0), 1.0)
                u_sh = jnp.where(keep, pltpu.roll(u, d, 0), 0.0)
                u = a * u_sh + u
                a = a * a_sh
            h = a * carry + u
            h_scr[pl.ds(off, SUBLANES), :] = h
            return jnp.broadcast_to(h[SUBLANES - 1:SUBLANES, :], (SUBLANES, c))

        carry = lax.fori_loop(0, groups, scan_group, jnp.broadcast_to(hst[s], (SUBLANES, c)))
        hst[s] = carry[0:1, :]

    y_a = jax.nn.gelu(proj_ref[:, pl.ds(c, c)]) * h_scr[...]
    y_ref[:, pl.ds(0, c)] = _rms_norm(y_a, nag_ref[...]).astype(y_ref.dtype)

    glu = proj_ref[:, pl.ds(2 * c, c)] * jax.nn.sigmoid(proj_ref[:, pl.ds(3 * c, c)])
    glu_ext[:, pl.ds(CONF_HIST, n_t), :] = glu.reshape(n_seq, n_t, c)
    c_b = _causal_conv(glu_ext[...], wcb_ref, n_t, CONF_HIST)
    c_b = (c_b + bcb_ref[...][None]).reshape(rows, c)
    mu = jnp.mean(c_b, axis=-1, keepdims=True)
    cen = c_b - mu
    var = jnp.mean(cen * cen, axis=-1, keepdims=True)
    ln = cen * lax.rsqrt(var + EPS) * lng_ref[...] + lnb_ref[...]
    y_b = ln * jax.nn.sigmoid(ln)
    y_ref[:, pl.ds(c, c)] = _rms_norm(y_b, nbg_ref[...]).astype(y_ref.dtype)

    lc_out[...] = ua_ext[:, pl.ds(LRU_HIST + n_t - (LRU_CONV - 1), LRU_CONV - 1), :]
    cc_out[...] = glu_ext[:, pl.ds(CONF_HIST + n_t - (CONF_KERNEL - 1), CONF_KERNEL - 1), :]
    h_out[...] = hst[...]
    ua_ext[:, pl.ds(0, LRU_HIST), :] = ua_ext[:, pl.ds(n_t, LRU_HIST), :]
    glu_ext[:, pl.ds(0, CONF_HIST), :] = glu_ext[:, pl.ds(n_t, CONF_HIST), :]


def _mixer(proj, row_block0, lc, h0, cc, weights, *, n_seq, n_t, n_tiles):
    b = lc.shape[1]
    c = W_A
    rows = n_seq * n_t
    n_sb = b // n_seq

    def row_map(sb, t):
        return (row_block0 + sb * n_tiles + t, 0)

    def const2(sb, t):
        return (0, 0)

    def seq_map(sb, t):
        return (sb, 0, 0)

    def depth_seq_map(sb, t):
        return (0, sb, 0, 0)

    w_specs = [pl.BlockSpec(w.shape, const2) for w in weights]
    lc_spec = pl.BlockSpec((None, n_seq, LRU_CONV - 1, c), depth_seq_map)
    cc_spec = pl.BlockSpec((None, n_seq, CONF_KERNEL - 1, c), depth_seq_map)
    return pl.pallas_call(
        functools.partial(_mixer_kernel, n_seq=n_seq, n_t=n_t),
        grid=(n_sb, n_tiles),
        in_specs=[
            pl.BlockSpec((rows, 4 * c), row_map),
            lc_spec,
            pl.BlockSpec((n_seq, 1, c), seq_map),
            cc_spec,
        ] + w_specs,
        out_specs=[
            pl.BlockSpec((rows, 2 * c), lambda sb, t: (sb * n_tiles + t, 0)),
            lc_spec,
            pl.BlockSpec((n_seq, 1, c), seq_map),
            cc_spec,
        ],
        out_shape=[
            jax.ShapeDtypeStruct((b * n_tiles * n_t, 2 * c), BF16),
            jax.ShapeDtypeStruct((1, b, LRU_CONV - 1, c), F32),
            jax.ShapeDtypeStruct((b, 1, c), F32),
            jax.ShapeDtypeStruct((1, b, CONF_KERNEL - 1, c), F32),
        ],
        scratch_shapes=[
            pltpu.VMEM((n_seq, LRU_HIST + n_t, c), F32),
            pltpu.VMEM((n_seq, CONF_HIST + n_t, c), F32),
            pltpu.VMEM((n_seq, 1, c), F32),
            pltpu.VMEM((rows, c), F32),
            pltpu.VMEM((rows, c), F32),
            pltpu.VMEM((rows, c), F32),
        ],
        compiler_params=_params(("arbitrary", "arbitrary")),
        name="mixer",
    )(proj, lc, h0, cc, *weights)


def _out_proj_kernel(*refs, n_blocks):
    k = len(n_blocks)
    y_refs, x_refs = refs[:k], refs[k:2 * k]
    w_ref, g_ref, rw_ref, x1_ref, xn_ref, logit_ref = refs[2 * k:]

    def body(p):
        x1 = x_refs[p][...] + jnp.dot(y_refs[p][...], w_ref[...], preferred_element_type=F32)
        x1_ref[...] = x1
        xn = _rms_norm(x1, g_ref[...]).astype(BF16)
        xn_ref[...] = xn
        logit_ref[...] = jnp.dot(xn, rw_ref[...], preferred_element_type=F32)

    _for_part(pl.program_id(0), n_blocks, body)


def _out_proj(y_parts, x_parts, w_out, g2, router_w, *, tm):
    d = w_out.shape[0]
    e = router_w.shape[1]
    n_blocks = tuple(x.shape[0] // tm for x in x_parts)
    part_specs = [
        pl.BlockSpec((tm, d), _part_map(start, nb, 0)) for start, nb in zip(_part_starts(n_blocks), n_blocks)
    ]
    n = sum(n_blocks) * tm
    const = lambda i: (0, 0)
    row = lambda i: (i, 0)
    return pl.pallas_call(
        functools.partial(_out_proj_kernel, n_blocks=n_blocks),
        grid=(sum(n_blocks),),
        in_specs=part_specs + part_specs + [
            pl.BlockSpec((d, d), const),
            pl.BlockSpec((1, d), const),
            pl.BlockSpec((d, e), const),
        ],
        out_specs=[pl.BlockSpec((tm, d), row), pl.BlockSpec((tm, d), row), pl.BlockSpec((tm, e), row)],
        out_shape=[
            jax.ShapeDtypeStruct((n, d), F32),
            jax.ShapeDtypeStruct((n, d), BF16),
            jax.ShapeDtypeStruct((n, e), F32),
        ],
        compiler_params=_params(("arbitrary",)),
        name="out_proj",
    )(*y_parts, *x_parts, w_out, g2, router_w)


def _first_argmax(work, index, sentinel):
    m = jnp.max(work, axis=0, keepdims=True)
    return jnp.min(jnp.where(work == m, index, sentinel), axis=0, keepdims=True)


def _route_kernel(logit_ref, bias_ref, eidx_ref, gate_ref, rank_ref, count_ref):
    n_tok = logit_ref.shape[1]

    scores = jax.nn.sigmoid(logit_ref[...])
    biased = scores + bias_ref[...]
    grouped = biased.reshape(N_EXPERT_GROUPS, EXPERTS_PER_GROUP, n_tok)
    in_group = lax.broadcasted_iota(jnp.int32, grouped.shape, 1)
    top1 = jnp.max(grouped, axis=1, keepdims=True)
    first1 = jnp.min(jnp.where(grouped == top1, in_group, EXPERTS_PER_GROUP), axis=1, keepdims=True)
    top2 = jnp.max(jnp.where(in_group == first1, -jnp.inf, grouped), axis=1, keepdims=True)
    group_scores = (top1 + top2).reshape(N_EXPERT_GROUPS, n_tok)

    group_id = lax.broadcasted_iota(jnp.int32, group_scores.shape, 0)
    group_on = jnp.zeros(group_scores.shape, F32)
    work = group_scores
    for _ in range(TOPK_GROUPS):
        pick = group_id == _first_argmax(work, group_id, N_EXPERT_GROUPS)
        group_on = jnp.where(pick, 1.0, group_on)
        work = jnp.where(pick, -jnp.inf, work)

    masked = jnp.where(group_on.reshape(N_EXPERT_GROUPS, 1, n_tok) > 0.0, grouped, -jnp.inf)
    work = masked.reshape(N_EXPERTS, n_tok)
    expert_id = lax.broadcasted_iota(jnp.int32, work.shape, 0)
    ids, sel = [], []
    chosen = jnp.zeros(work.shape, F32)
    for _ in range(TOP_K):
        first = _first_argmax(work, expert_id, N_EXPERTS)
        pick = expert_id == first
        ids.append(first)
        sel.append(jnp.sum(jnp.where(pick, scores, 0.0), axis=0, keepdims=True))
        chosen = jnp.where(pick, 1.0, chosen)
        work = jnp.where(pick, -jnp.inf, work)
    sel = jnp.concatenate(sel, axis=0)
    eidx_ref[...] = jnp.concatenate(ids, axis=0)
    gate_ref[...] = sel / (jnp.sum(sel, axis=0, keepdims=True) + 1e-20) * ROUTED_SCALE

    earlier = lax.broadcasted_iota(jnp.int32, (n_tok, n_tok), 0) < lax.broadcasted_iota(jnp.int32, (n_tok, n_tok), 1)
    before = jnp.dot(chosen.astype(BF16), earlier.astype(BF16), preferred_element_type=F32)
    rank_ref[...] = jnp.concatenate(
        [jnp.sum(jnp.where(expert_id == ids[k], before, 0.0), axis=0, keepdims=True) for k in range(TOP_K)], axis=0
    ).astype(jnp.int32)
    count_ref[0] = jnp.sum(chosen, axis=1, keepdims=True).astype(jnp.int32)


def _route(logits_t, bias, *, tl):
    e, n = logits_t.shape
    pick_spec = pl.BlockSpec((TOP_K, tl), lambda i: (0, i))
    return pl.pallas_call(
        _route_kernel,
        grid=(n // tl,),
        in_specs=[pl.BlockSpec((e, tl), lambda i: (0, i)), pl.BlockSpec((e, 1), lambda i: (0, 0))],
        out_specs=[pick_spec, pick_spec, pick_spec, pl.BlockSpec((1, e, 1), lambda i: (i, 0, 0))],
        out_shape=[
            jax.ShapeDtypeStruct((TOP_K, n), jnp.int32),
            jax.ShapeDtypeStruct((TOP_K, n), F32),
            jax.ShapeDtypeStruct((TOP_K, n), jnp.int32),
            jax.ShapeDtypeStruct((n // tl, e, 1), jnp.int32),
        ],
        compiler_params=_params(("arbitrary",)),
        name="route",
    )(logits_t, bias)


def _gate_lanes(gate):
    g1 = gate.astype(BF16).astype(F32)
    rest = gate - g1
    g2 = rest.astype(BF16).astype(F32)
    g3 = (rest - g2).astype(BF16).astype(F32)
    lane = lax.broadcasted_iota(jnp.int32, (gate.shape[0], LANES), 1)
    return jnp.where(lane == 0, g1, jnp.where(lane == 1, g2, jnp.where(lane == 2, g3, 0.0))).astype(BF16)


def _tab_offsets():
    offsets, col = [], len(CHUNK_ROWS)
    for cap in CHUNK_MAX:
        offsets.append((col, col + cap))
        col += 2 * cap
    return offsets, col


def _segment_copies(tab_ref, block, make_copy):
    offsets, width = _tab_offsets()
    base = block * width
    counts = []
    for k, (rows, (local_col, global_col)) in enumerate(zip(CHUNK_ROWS, offsets)):
        n_chunks = tab_ref[base + k]

        def start_one(j, rows=rows, local_col=local_col, global_col=global_col):
            make_copy(
                pl.multiple_of(tab_ref[base + local_col + j], SEG_ROWS),
                pl.multiple_of(tab_ref[base + global_col + j], SEG_ROWS),
                rows,
            ).start()

        def per_group(g, carry):
            for u in range(ISSUE_UNROLL):
                start_one(g * ISSUE_UNROLL + u)
            return carry

        def per_chunk(j, carry):
            start_one(j)
            return carry

        n_groups = n_chunks // ISSUE_UNROLL
        lax.fori_loop(0, n_groups, per_group, 0)
        lax.fori_loop(n_groups * ISSUE_UNROLL, n_chunks, per_chunk, 0)
        counts.append(n_chunks)
    return counts


def _wait_copies(counts, make_copy):
    for rows, count in zip(CHUNK_ROWS, counts):
        def wait_group(c, carry, rows=rows):
            make_copy(0, 0, rows * WAIT_GROUP).wait()
            return carry

        def wait_one(c, carry, rows=rows):
            make_copy(0, 0, rows).wait()
            return carry

        n_groups = count // WAIT_GROUP
        lax.fori_loop(0, n_groups, wait_group, 0)
        lax.fori_loop(n_groups * WAIT_GROUP, count, wait_one, 0)


def _dispatch_kernel(tab_ref, gap_ref, pos_ref, gate_ref, x_ref, xs_hbm, sorted_scr, zero_scr, pending, sem):
    step = pl.program_id(0)
    slot = step % 2
    d = x_ref.shape[1]
    n_tok = x_ref.shape[0]
    pos = pos_ref[...]
    gate = gate_ref[...]
    x = x_ref[...]
    piece_id = lax.broadcasted_iota(jnp.int32, (HOT_PIECE, n_tok), 0)
    pos_k = [jnp.broadcast_to(pos[k:k + 1, :], (HOT_PIECE, n_tok)) for k in range(TOP_K)]
    gate_k = [jnp.broadcast_to(gate[k:k + 1, :], (HOT_PIECE, n_tok)) for k in range(TOP_K)]
    for c in range(LOCAL_ROWS // SORT_CHUNK):
        one_hot, row_gate = [], []
        for p in range(SORT_CHUNK // HOT_PIECE):
            row_id = piece_id + (c * SORT_CHUNK + p * HOT_PIECE)
            hot = jnp.zeros((HOT_PIECE, n_tok), F32)
            gate_hit = jnp.zeros((HOT_PIECE, n_tok), F32)
            for k in range(TOP_K):
                hit = row_id == pos_k[k]
                hot = jnp.where(hit, 1.0, hot)
                gate_hit = jnp.where(hit, gate_k[k], gate_hit)
            one_hot.append(hot.astype(BF16))
            row_gate.append(jnp.sum(gate_hit, axis=1, keepdims=True))
        rows = jnp.dot(jnp.concatenate(one_hot, axis=0), x, preferred_element_type=F32)
        out_rows = pl.ds(c * SORT_CHUNK, SORT_CHUNK)
        sorted_scr[slot, out_rows, pl.ds(0, d)] = rows.astype(BF16)
        sorted_scr[slot, out_rows, pl.ds(d, LANES)] = _gate_lanes(jnp.concatenate(row_gate, axis=0))

    def copy_chunk(local_row, global_row, rows):
        return pltpu.make_async_copy(
            sorted_scr.at[slot, pl.ds(local_row, rows), :], xs_hbm.at[pl.ds(global_row, rows), :], sem
        )

    @pl.when(step > 0)
    def _():
        _wait_copies([pending[k] for k in range(len(CHUNK_ROWS))], copy_chunk)

    for k, count in enumerate(_segment_copies(tab_ref, step, copy_chunk)):
        pending[k] = count

    @pl.when(step == pl.num_programs(0) - 1)
    def _():
        zero_scr[...] = jnp.zeros(zero_scr.shape, BF16)
        tile_rows = zero_scr.shape[0]

        def zero_chunk(row):
            return pltpu.make_async_copy(
                zero_scr.at[pl.ds(0, SEG_ROWS), :],
                xs_hbm.at[pl.ds(pl.multiple_of(row, SEG_ROWS), SEG_ROWS), :],
                sem,
            )

        def zero_tile(row):
            return pltpu.make_async_copy(
                zero_scr, xs_hbm.at[pl.ds(pl.multiple_of(row, tile_rows), tile_rows), :], sem
            )

        def per_gap(g, total):
            start = gap_ref[g]
            chunks = gap_ref[N_EXPERTS + g]

            def per_chunk(c, carry):
                zero_chunk(start + c * SEG_ROWS).start()
                return carry

            lax.fori_loop(0, chunks, per_chunk, 0)
            return total + chunks

        n_fill = lax.fori_loop(0, N_EXPERTS, per_gap, 0)
        tail = gap_ref[2 * N_EXPERTS]
        n_tail = gap_ref[2 * N_EXPERTS + 1]

        def start_tile(c, carry):
            zero_tile(tail + c * tile_rows).start()
            return carry

        def wait_tile(c, carry):
            zero_tile(0).wait()
            return carry

        def wait_chunk(c, carry):
            zero_chunk(0).wait()
            return carry

        lax.fori_loop(0, n_tail, start_tile, 0)
        _wait_copies([pending[k] for k in range(len(CHUNK_ROWS))], copy_chunk)
        lax.fori_loop(0, n_fill, wait_chunk, 0)
        lax.fori_loop(0, n_tail, wait_tile, 0)


def _dispatch(tab, gaps, pos_t, gate_t, xn, n_sorted, *, tb, zero_rows):
    n, d = xn.shape
    width = d + LANES
    pick_spec = pl.BlockSpec((TOP_K, tb), lambda i, tab, gaps: (0, i))
    return pl.pallas_call(
        _dispatch_kernel,
        grid_spec=pltpu.PrefetchScalarGridSpec(
            num_scalar_prefetch=2,
            grid=(n // tb,),
            in_specs=[pick_spec, pick_spec, pl.BlockSpec((tb, d), lambda i, tab, gaps: (i, 0))],
            out_specs=pl.BlockSpec(memory_space=pl.ANY),
            scratch_shapes=[
                pltpu.VMEM((2, LOCAL_ROWS, width), BF16),
                pltpu.VMEM((zero_rows, width), BF16),
                pltpu.SMEM((len(CHUNK_ROWS),), jnp.int32),
                pltpu.SemaphoreType.DMA,
            ],
        ),
        out_shape=jax.ShapeDtypeStruct((n_sorted, width), BF16),
        compiler_params=_params(("arbitrary",)),
        name="dispatch",
    )(tab, gaps, pos_t, gate_t, xn)


def _experts_kernel(
    expert_ref, used_ref, fresh_ref, next_ref, src_ref, xs_ref, wg_hbm, wu_hbm, wd_hbm, ys_ref,
    wg32, wu32, wd32, wg16, wu16, wd16, sems,
):
    it = pl.program_id(0)
    d = ys_ref.shape[1]

    def weight_copies(e):
        return (
            pltpu.make_async_copy(wg_hbm.at[e], wg32, sems.at[0]),
            pltpu.make_async_copy(wu_hbm.at[e], wu32, sems.at[1]),
            pltpu.make_async_copy(wd_hbm.at[e], wd32, sems.at[2]),
        )

    @pl.when(it == 0)
    def _():
        for copy in weight_copies(expert_ref[0]):
            copy.start()

    @pl.when(used_ref[it] == 0)
    def _():
        ys_ref[...] = jnp.zeros(ys_ref.shape, ys_ref.dtype)

    @pl.when((used_ref[it] > 0) & (fresh_ref[it] == 1))
    def _():
        for copy in weight_copies(expert_ref[it]):
            copy.wait()
        wg16[...] = wg32[...].astype(BF16)
        wu16[...] = wu32[...].astype(BF16)
        wd16[...] = wd32[...].astype(BF16)

        @pl.when(next_ref[it] >= 0)
        def _():
            for copy in weight_copies(next_ref[it]):
                copy.start()

    tm = ys_ref.shape[0]
    for n_sub in range(1, tm // EXPERT_SUB + 1):
        rows = n_sub * EXPERT_SUB

        @pl.when(used_ref[it] == n_sub)
        def _(rows=rows):
            x = xs_ref[pl.ds(0, rows), pl.ds(0, d)]
            gate = jnp.sum(xs_ref[pl.ds(0, rows), pl.ds(d, LANES)].astype(F32), axis=1, keepdims=True)
            g = jnp.dot(x, wg16[...], preferred_element_type=F32)
            u = jnp.dot(x, wu16[...], preferred_element_type=F32)
            h = g * jax.nn.sigmoid(g) * u * gate
            y = jnp.dot(h.astype(BF16), wd16[...], preferred_element_type=F32)
            ys_ref[pl.ds(0, rows), :] = y.astype(ys_ref.dtype)
            if rows < tm:
                ys_ref[pl.ds(rows, tm - rows), :] = jnp.zeros((tm - rows, d), ys_ref.dtype)


def _experts(tiles, xs, w_gate, w_up, w_down, *, tm):
    p = xs.shape[0]
    d, f = w_gate.shape[1:]
    row_map = lambda i, expert, used, fresh, nxt, src: (i, 0)
    src_map = lambda i, expert, used, fresh, nxt, src: (src[i], 0)
    hbm = pl.BlockSpec(memory_space=pl.ANY)
    return pl.pallas_call(
        _experts_kernel,
        grid_spec=pltpu.PrefetchScalarGridSpec(
            num_scalar_prefetch=5,
            grid=(p // tm,),
            in_specs=[pl.BlockSpec((tm, xs.shape[1]), src_map), hbm, hbm, hbm],
            out_specs=pl.BlockSpec((tm, d), row_map),
            scratch_shapes=[
                pltpu.VMEM((d, f), F32), pltpu.VMEM((d, f), F32), pltpu.VMEM((f, d), F32),
                pltpu.VMEM((d, f), BF16), pltpu.VMEM((d, f), BF16), pltpu.VMEM((f, d), BF16),
                pltpu.SemaphoreType.DMA((3,)),
            ],
        ),
        out_shape=jax.ShapeDtypeStruct((p, d), BF16),
        compiler_params=_params(("arbitrary",)),
        name="experts",
    )(*tiles, xs, w_gate, w_up, w_down)


def _combine_kernel(tab_ref, pos_ref, x1_ref, xn_ref, sg_ref, su_ref, sd_ref, fg_ref, ys_hbm, *rest, n_blocks):
    o_refs = rest[:len(n_blocks)]
    local_scr, pending, sems = rest[len(n_blocks):]
    n_tok = x1_ref.shape[0]
    step = pl.program_id(0)
    slot = step % 2

    n_sizes = len(CHUNK_ROWS)

    def copy_chunk_into(dst_slot):
        def copy_chunk(local_row, global_row, rows):
            return pltpu.make_async_copy(
                ys_hbm.at[pl.ds(global_row, rows), :],
                local_scr.at[dst_slot, pl.ds(local_row, rows), :],
                sems.at[dst_slot],
            )

        return copy_chunk

    def fetch(block, dst_slot):
        for k, count in enumerate(_segment_copies(tab_ref, block, copy_chunk_into(dst_slot))):
            pending[dst_slot * n_sizes + k] = count

    @pl.when(step == 0)
    def _():
        local_scr[...] = jnp.zeros(local_scr.shape, local_scr.dtype)
        fetch(0, 0)

    @pl.when(step + 1 < pl.num_programs(0))
    def _():
        fetch(step + 1, 1 - slot)

    xn = xn_ref[...]
    sg = jnp.dot(xn, sg_ref[...], preferred_element_type=F32)
    su = jnp.dot(xn, su_ref[...], preferred_element_type=F32)
    acc = x1_ref[...] + jnp.dot(
        (sg * jax.nn.sigmoid(sg) * su).astype(BF16), sd_ref[...], preferred_element_type=F32
    )

    piece_id = lax.broadcasted_iota(jnp.int32, (HOT_PIECE, LANES), 1)
    one_hot = []
    for r in range(n_tok // HOT_PIECE):
        pos = pos_ref[pl.ds(r * HOT_PIECE, HOT_PIECE), :]
        pos_k = [jnp.broadcast_to(pos[:, k:k + 1], (HOT_PIECE, LANES)) for k in range(TOP_K)]
        cols = []
        for c in range(LOCAL_ROWS // LANES):
            col_id = piece_id + c * LANES
            hot = jnp.zeros((HOT_PIECE, LANES), F32)
            for k in range(TOP_K):
                hot = jnp.where(col_id == pos_k[k], 1.0, hot)
            cols.append(hot.astype(BF16))
        one_hot.append(jnp.concatenate(cols, axis=1))
    one_hot = jnp.concatenate(one_hot, axis=0)

    _wait_copies([pending[slot * n_sizes + k] for k in range(n_sizes)], copy_chunk_into(slot))
    routed = jnp.dot(one_hot, local_scr[slot], preferred_element_type=F32)
    out = _rms_norm(acc + routed, fg_ref[...])

    def body(p):
        o_refs[p][...] = out

    _for_part(pl.program_id(0), n_blocks, body)


def _combine(tab, pos, x1, xn, sh_gate, sh_up, sh_down, final_g, ys, part_rows, *, tb):
    n, d = x1.shape
    f = sh_gate.shape[1]
    n_blocks = tuple(r // tb for r in part_rows)
    const = lambda i, tab: (0, 0)
    row = lambda i, tab: (i, 0)
    return pl.pallas_call(
        functools.partial(_combine_kernel, n_blocks=n_blocks),
        grid_spec=pltpu.PrefetchScalarGridSpec(
            num_scalar_prefetch=1,
            grid=(n // tb,),
            in_specs=[
                pl.BlockSpec((tb, TOP_K), row),
                pl.BlockSpec((tb, d), row),
                pl.BlockSpec((tb, d), row),
                pl.BlockSpec((d, f), const),
                pl.BlockSpec((d, f), const),
                pl.BlockSpec((f, d), const),
                pl.BlockSpec((1, d), const),
                pl.BlockSpec(memory_space=pl.ANY),
            ],
            out_specs=[
                pl.BlockSpec((tb, d), _part_map(start, nb, 0))
                for start, nb in zip(_part_starts(n_blocks), n_blocks)
            ],
            scratch_shapes=[
                pltpu.VMEM((2, LOCAL_ROWS, d), BF16),
                pltpu.SMEM((2 * len(CHUNK_ROWS),), jnp.int32),
                pltpu.SemaphoreType.DMA((2,)),
            ],
        ),
        out_shape=[jax.ShapeDtypeStruct((r, d), F32) for r in part_rows],
        compiler_params=_params(("arbitrary",)),
        name="combine",
    )(tab, pos, x1, xn, sh_gate, sh_up, sh_down, final_g, ys)


def _segment_tables(block_counts, n_rows, *, tm):
    padded = (block_counts + SEG_ROWS - 1) // SEG_ROWS * SEG_ROWS
    local_start = jnp.cumsum(padded, axis=1) - padded
    per_expert = jnp.sum(padded, axis=0)
    region = (per_expert + tm - 1) // tm * tm
    region_start = jnp.concatenate([jnp.zeros((1,), jnp.int32), jnp.cumsum(region).astype(jnp.int32)])
    global_start = region_start[None, :N_EXPERTS] + jnp.cumsum(padded, axis=0) - padded
    counts, lists = [], []
    done = jnp.zeros_like(padded)
    for rows, cap in zip(CHUNK_ROWS, CHUNK_MAX):
        per_seg = (padded - done) // rows
        seg_end = jnp.cumsum(per_seg, axis=1)
        j = jnp.arange(cap, dtype=jnp.int32)
        seg_of_chunk = jnp.sum((seg_end[:, None, :] <= j[None, :, None]).astype(jnp.int32), axis=2)
        in_seg = seg_of_chunk[:, :, None] == jnp.arange(N_EXPERTS, dtype=jnp.int32)
        first = done - (seg_end - per_seg) * rows
        for start in (local_start, global_start):
            lists.append(jnp.sum(jnp.where(in_seg, (start + first)[:, None, :], 0), axis=2) + j[None, :] * rows)
        counts.append(seg_end[:, N_EXPERTS - 1:])
        done = done + per_seg * rows
    tab = jnp.concatenate(counts + lists, axis=1).astype(jnp.int32).reshape(-1)

    region_end = region_start[:N_EXPERTS] + per_expert
    tail = region_start[N_EXPERTS:]
    gaps = jnp.concatenate(
        [region_end, (region_start[1:] - region_end) // SEG_ROWS, tail, (n_rows - tail) // tm]
    ).astype(jnp.int32)

    tile_start = jnp.arange(n_rows // tm, dtype=jnp.int32) * tm
    expert = jnp.clip(
        jnp.sum((region_start[None, :] <= tile_start[:, None]).astype(jnp.int32), axis=1) - 1, 0, N_EXPERTS - 1
    )
    tile_end = jnp.sum(jnp.where(expert[:, None] == jnp.arange(N_EXPERTS), region_end[None, :], 0), axis=1)
    used = jnp.clip((tile_end - tile_start + EXPERT_SUB - 1) // EXPERT_SUB, 0, tm // EXPERT_SUB).astype(jnp.int32)
    fresh = jnp.concatenate([jnp.ones((1,), jnp.int32), (expert[1:] != expert[:-1]).astype(jnp.int32)])
    experts = jnp.arange(N_EXPERTS, dtype=jnp.int32)
    later = (experts[None, :] > expert[:, None]) & (per_expert[None, :] > 0)
    nxt = jnp.min(jnp.where(later, experts[None, :], N_EXPERTS), axis=1)
    nxt = jnp.where(nxt < N_EXPERTS, nxt, -1).astype(jnp.int32)
    tile_id = jnp.arange(n_rows // tm, dtype=jnp.int32)
    src = jnp.minimum(tile_id, jnp.max(jnp.where(used > 0, tile_id, 0)))
    return tab, gaps, (expert, used, fresh, nxt, src), local_start.astype(jnp.int32)


def _block_diag(w):
    h, dh, _ = w.shape
    eye = jnp.eye(h, dtype=w.dtype)
    return (eye[:, None, :, None] * w[:, :, None, :]).reshape(h * dh, h * dh)


ROW_TM = 512
OUT_PROJ_TM = 256
IN_PROJ_TN = 2048
PROMPT_T = 256
SAMPLE_SEQS = 16
EXPERT_TM = 512
EXPERT_SUB = 256


def kernel(x_prompt, x_sample, state_lru_conv, state_lru_h, state_conf_conv, meta_tokens, norm1_g, w_in, lru_conv_w, lru_conv_b, lru_wa, lru_ba, lru_wi, lru_bi, lru_lambda, conf_conv_w, conf_conv_b, conf_ln_g, conf_ln_b, out_norm_a, out_norm_b, w_out, norm2_g, router_w, router_bias, exp_w_gate, exp_w_up, exp_w_down, sh_w_gate, sh_w_up, sh_w_down, final_norm_g):
    b_p, seq, d = x_prompt.shape
    b_s, t_s, _ = x_sample.shape
    n_p = b_p * seq
    n_s = b_s * t_s
    n = n_p + n_s
    x_parts = (x_prompt.reshape(n_p, d), x_sample.reshape(n_s, d))

    row = lambda v: v.reshape(1, -1)
    mixer_w = (
        lru_conv_w[0], row(lru_conv_b[0]),
        _block_diag(lru_wa[0]).astype(BF16), row(lru_ba[0]),
        _block_diag(lru_wi[0]).astype(BF16), row(lru_bi[0]),
        row(lru_lambda[0]),
        conf_conv_w[0], row(conf_conv_b[0]), row(conf_ln_g[0]), row(conf_ln_b[0]),
        row(out_norm_a[0]), row(out_norm_b[0]),
    )

    proj_m = _in_proj((meta_tokens,), row(norm1_g[0]), w_in[0], tm=N_META, tn=IN_PROJ_TN)
    _, m_lc, m_h, m_cc = _mixer(
        proj_m, 0,
        jnp.zeros((1, 1, LRU_CONV - 1, W_A), F32), jnp.zeros((1, 1, W_A), F32),
        jnp.zeros((1, 1, CONF_KERNEL - 1, W_B), F32),
        mixer_w, n_seq=1, n_t=N_META, n_tiles=1,
    )

    proj = _in_proj(x_parts, row(norm1_g[0]), w_in[0], tm=ROW_TM, tn=IN_PROJ_TN)
    y_p, p_lc, p_h, p_cc = _mixer(
        proj, 0,
        jnp.broadcast_to(m_lc, (1, b_p) + m_lc.shape[2:]), jnp.broadcast_to(m_h, (b_p,) + m_h.shape[1:]),
        jnp.broadcast_to(m_cc, (1, b_p) + m_cc.shape[2:]),
        mixer_w, n_seq=1, n_t=PROMPT_T, n_tiles=seq // PROMPT_T,
    )
    y_s, s_lc, s_h, s_cc = _mixer(
        proj, n_p // (SAMPLE_SEQS * t_s),
        state_lru_conv, state_lru_h[0].reshape(b_s, 1, W_A), state_conf_conv,
        mixer_w, n_seq=SAMPLE_SEQS, n_t=t_s, n_tiles=1,
    )

    x1, xn2, logits = _out_proj(
        (y_p, y_s), x_parts, w_out[0].astype(BF16), row(norm2_g[0]), router_w[0].astype(BF16), tm=OUT_PROJ_TM
    )

    eidx_t, gate_t, rank_t, block_counts = _route(logits.T, router_bias[0].reshape(N_EXPERTS, 1), tl=TOKEN_BLOCK)
    n_blocks = n // TOKEN_BLOCK
    n_sorted = n_blocks * (TOKEN_BLOCK * TOP_K + N_EXPERTS * (SEG_ROWS - 1)) + N_EXPERTS * (EXPERT_TM - 1)
    n_sorted = -(-n_sorted // EXPERT_TM) * EXPERT_TM
    tab, gaps, tiles, local_start = _segment_tables(block_counts[:, :, 0], n_sorted, tm=EXPERT_TM)
    expert_one_hot = eidx_t[:, :, None] == jnp.arange(N_EXPERTS, dtype=jnp.int32)
    token_start = jnp.repeat(local_start, TOKEN_BLOCK, axis=0)
    pos_t = jnp.sum(jnp.where(expert_one_hot, token_start[None], 0), axis=2) + rank_t

    xs = _dispatch(tab, gaps, pos_t, gate_t, xn2, n_sorted, tb=TOKEN_BLOCK, zero_rows=EXPERT_TM)
    ys = _experts(tiles, xs, exp_w_gate[0], exp_w_up[0], exp_w_down[0], tm=EXPERT_TM)
    out_p, out_s = _combine(
        tab, pos_t.T, x1, xn2,
        sh_w_gate[0].astype(BF16), sh_w_up[0].astype(BF16), sh_w_down[0].astype(BF16),
        row(final_norm_g), ys, (n_p, n_s), tb=TOKEN_BLOCK,
    )

    return (
        out_p.reshape(b_p, seq, d), out_s.reshape(b_s, t_s, d),
        p_lc, p_h.reshape(1, b_p, W_A), p_cc,
        s_lc, s_h.reshape(1, b_s, W_A), s_cc,
    )
```

```python
import functools

import jax
import jax.numpy as jnp
from jax import lax
from jax.experimental import pallas as pl
from jax.experimental.pallas import tpu as pltpu

D_MODEL = 2048
N_META = 16
W_A = 1024
W_B = 1024
LRU_CONV = 4
LRU_C = 8.0
CONF_KERNEL = 31
N_EXPERTS = 64
N_EXPERT_GROUPS = 8
EXPERTS_PER_GROUP = 8
TOPK_GROUPS = 4
TOP_K = 8
ROUTED_SCALE = 2.5
EPS = 1e-6

SUBLANES = 8
LANES = 128
TOKEN_BLOCK = 256
SEG_ROWS = SUBLANES
LOCAL_ROWS = 2560
SORT_CHUNK = 512
HOT_PIECE = 16
CHUNK_ROWS = (SEG_ROWS,)
CHUNK_MAX = (LOCAL_ROWS // SEG_ROWS,)
ISSUE_UNROLL = 4
WAIT_GROUP = 32
assert LOCAL_ROWS >= TOKEN_BLOCK * TOP_K + N_EXPERTS * (SEG_ROWS - 1) and LOCAL_ROWS % SORT_CHUNK == 0
LRU_HIST = SUBLANES
CONF_HIST = 32
VMEM_LIMIT = 56 * 1024 * 1024

F32 = jnp.float32
BF16 = jnp.bfloat16


def _params(semantics):
    return pltpu.CompilerParams(dimension_semantics=semantics, vmem_limit_bytes=VMEM_LIMIT)


def _rms_norm(x, g):
    return x * lax.rsqrt(jnp.mean(x * x, axis=-1, keepdims=True) + EPS) * g


def _part_starts(n_blocks):
    starts, s = [], 0
    for nb in n_blocks:
        starts.append(s)
        s += nb
    return starts


def _part_map(start, nb, grid_axis):
    def index_map(*ids):
        return (jnp.clip(ids[grid_axis] - start, 0, nb - 1), 0)

    return index_map


def _for_part(i, n_blocks, body):
    for p, (start, nb) in enumerate(zip(_part_starts(n_blocks), n_blocks)):
        pl.when((i >= start) & (i < start + nb))(functools.partial(body, p))


def _in_proj_kernel(*refs, n_blocks):
    x_refs = refs[:len(n_blocks)]
    meta_ref, g_ref, w_ref, o_ref, meta_o_ref, w16 = refs[len(n_blocks):]

    def project(x):
        return jnp.dot(_rms_norm(x, g_ref[...]).astype(BF16), w16[...], preferred_element_type=F32)

    @pl.when(pl.program_id(1) == 0)
    def _():
        w16[...] = w_ref[...].astype(BF16)
        meta_o_ref[...] = project(meta_ref[...])

    def body(p):
        o_ref[...] = project(x_refs[p][...])

    _for_part(pl.program_id(1), n_blocks, body)


def _in_proj(x_parts, meta, g, w, *, tm, tn):
    d, f = w.shape
    n_meta = meta.shape[0]
    n_blocks = tuple(x.shape[0] // tm for x in x_parts)
    x_specs = [
        pl.BlockSpec((tm, d), _part_map(start, nb, 1)) for start, nb in zip(_part_starts(n_blocks), n_blocks)
    ]
    return pl.pallas_call(
        functools.partial(_in_proj_kernel, n_blocks=n_blocks),
        grid=(f // tn, sum(n_blocks)),
        in_specs=x_specs + [
            pl.BlockSpec((n_meta, d), lambda j, i: (0, 0)),
            pl.BlockSpec((1, d), lambda j, i: (0, 0)),
            pl.BlockSpec((d, tn), lambda j, i: (0, j), pipeline_mode=pl.Buffered(1)),
        ],
        out_specs=[pl.BlockSpec((tm, tn), lambda j, i: (i, j)), pl.BlockSpec((n_meta, tn), lambda j, i: (0, j))],
        out_shape=[
            jax.ShapeDtypeStruct((sum(n_blocks) * tm, f), F32),
            jax.ShapeDtypeStruct((n_meta, f), F32),
        ],
        scratch_shapes=[pltpu.VMEM((d, tn), BF16)],
        compiler_params=_params(("arbitrary", "arbitrary")),
        name="in_proj",
    )(*x_parts, meta, g, w)


def _causal_conv(ext, w_ref, n_t, hist):
    taps = w_ref.shape[0]
    length = ext.shape[1]
    shifted = {0: ext}
    acc = None
    for j in range(taps):
        q, r = divmod(hist - (taps - 1) + j, SUBLANES)
        if r not in shifted:
            shifted[r] = pltpu.roll(ext, length - r, 1)
        term = w_ref[pl.ds(j, 1), :][None] * shifted[r][:, q * SUBLANES:q * SUBLANES + n_t, :]
        acc = term if acc is None else acc + term
    return acc


def _mixer_kernel(
    proj_ref, lc_ref, h0_ref, cc_ref,
    wca_ref, bca_ref, wa_ref, ba_ref, wi_ref, bi_ref, lam_ref,
    wcb_ref, bcb_ref, lng_ref, lnb_ref, nag_ref, nbg_ref,
    y_ref, lc_out, h_out, cc_out,
    ua_ext, glu_ext, hst, a_scr, u_scr, h_scr, wa_bd, wi_bd,
    *, n_seq, n_t,
):
    t = pl.program_id(1)
    rows = n_seq * n_t
    c = W_A

    @pl.when((pl.program_id(0) == 0) & (t == 0))
    def _():
        n_heads, head_dim, _ = wa_ref.shape
        for w_ref, dense in ((wa_ref, wa_bd), (wi_ref, wi_bd)):
            dense[...] = jnp.zeros(dense.shape, dense.dtype)
            for h in range(n_heads):
                block = pl.ds(h * head_dim, head_dim)
                dense[block, block] = w_ref[h].astype(dense.dtype)

    @pl.when(t == 0)
    def _():
        ua_ext[:, pl.ds(0, LRU_HIST), :] = jnp.zeros((n_seq, LRU_HIST, c), F32)
        ua_ext[:, pl.ds(LRU_HIST - (LRU_CONV - 1), LRU_CONV - 1), :] = lc_ref[...]
        glu_ext[:, pl.ds(0, CONF_HIST), :] = jnp.zeros((n_seq, CONF_HIST, c), F32)
        glu_ext[:, pl.ds(CONF_HIST - (CONF_KERNEL - 1), CONF_KERNEL - 1), :] = cc_ref[...]
        hst[...] = h0_ref[...]

    ua_ext[:, pl.ds(LRU_HIST, n_t), :] = proj_ref[:, pl.ds(0, c)].reshape(n_seq, n_t, c)
    c_a = _causal_conv(ua_ext[...], wca_ref, n_t, LRU_HIST)
    c_a = (c_a + bca_ref[...][None]).reshape(rows, c)
    c_a16 = c_a.astype(BF16)
    r = jax.nn.sigmoid(jnp.dot(c_a16, wa_bd[...], preferred_element_type=F32) + ba_ref[...])
    i = jax.nn.sigmoid(jnp.dot(c_a16, wi_bd[...], preferred_element_type=F32) + bi_ref[...])
    neg_lam = -lam_ref[...]
    softplus = jnp.maximum(neg_lam, 0.0) + jnp.log1p(jnp.exp(-jnp.abs(neg_lam)))
    log_a = (-LRU_C * r) * softplus
    a = jnp.exp(log_a)
    a_scr[...] = a
    u_scr[...] = jnp.sqrt(-jnp.tanh(log_a) * (a * a + 1.0)) * (i * c_a)

    row8 = lax.broadcasted_iota(jnp.int32, (SUBLANES, c), 0)
    groups = n_t // SUBLANES
    for s in range(n_seq):
        def scan_group(g, carry, s=s):
            off = pl.multiple_of(s * n_t + g * SUBLANES, SUBLANES)
            a = a_scr[pl.ds(off, SUBLANES), :]
            u = u_scr[pl.ds(off, SUBLANES), :]
            for d in (1, 2, 4):
                keep = row8 >= d
                a_sh = jnp.where(keep, pltpu.roll(a, d, 0), 1.0)
                u_sh = jnp.where(keep, pltpu.roll(u, d, 0), 0.0)
                u = a * u_sh + u
                a = a * a_sh
            h = a * carry + u
            h_scr[pl.ds(off, SUBLANES), :] = h
            return jnp.broadcast_to(h[SUBLANES - 1:SUBLANES, :], (SUBLANES, c))

        carry = lax.fori_loop(0, groups, scan_group, jnp.broadcast_to(hst[s], (SUBLANES, c)))
        hst[s] = carry[0:1, :]

    y_a = jax.nn.gelu(proj_ref[:, pl.ds(c, c)]) * h_scr[...]
    y_ref[:, pl.ds(0, c)] = _rms_norm(y_a, nag_ref[...]).astype(y_ref.dtype)

    glu = proj_ref[:, pl.ds(2 * c, c)] * jax.nn.sigmoid(proj_ref[:, pl.ds(3 * c, c)])
    glu_ext[:, pl.ds(CONF_HIST, n_t), :] = glu.reshape(n_seq, n_t, c)
    c_b = _causal_conv(glu_ext[...], wcb_ref, n_t, CONF_HIST)
    c_b = (c_b + bcb_ref[...][None]).reshape(rows, c)
    mu = jnp.mean(c_b, axis=-1, keepdims=True)
    cen = c_b - mu
    var = jnp.mean(cen * cen, axis=-1, keepdims=True)
    ln = cen * lax.rsqrt(var + EPS) * lng_ref[...] + lnb_ref[...]
    y_b = ln * jax.nn.sigmoid(ln)
    y_ref[:, pl.ds(c, c)] = _rms_norm(y_b, nbg_ref[...]).astype(y_ref.dtype)

    lc_out[...] = ua_ext[:, pl.ds(LRU_HIST + n_t - (LRU_CONV - 1), LRU_CONV - 1), :]
    cc_out[...] = glu_ext[:, pl.ds(CONF_HIST + n_t - (CONF_KERNEL - 1), CONF_KERNEL - 1), :]
    h_out[...] = hst[...]
    ua_ext[:, pl.ds(0, LRU_HIST), :] = ua_ext[:, pl.ds(n_t, LRU_HIST), :]
    glu_ext[:, pl.ds(0, CONF_HIST), :] = glu_ext[:, pl.ds(n_t, CONF_HIST), :]


def _mixer(proj, row_block0, lc, h0, cc, weights, *, n_seq, n_t, n_tiles):
    b = lc.shape[1]
    c = W_A
    rows = n_seq * n_t
    n_sb = b // n_seq

    def row_map(sb, t):
        return (row_block0 + sb * n_tiles + t, 0)

    def seq_map(sb, t):
        return (sb, 0, 0)

    def depth_seq_map(sb, t):
        return (0, sb, 0, 0)

    w_specs = [pl.BlockSpec(w.shape, (lambda sb, t, nd=w.ndim: (0,) * nd)) for w in weights]
    lc_spec = pl.BlockSpec((None, n_seq, LRU_CONV - 1, c), depth_seq_map)
    cc_spec = pl.BlockSpec((None, n_seq, CONF_KERNEL - 1, c), depth_seq_map)
    return pl.pallas_call(
        functools.partial(_mixer_kernel, n_seq=n_seq, n_t=n_t),
        grid=(n_sb, n_tiles),
        in_specs=[
            pl.BlockSpec((rows, 4 * c), row_map),
            lc_spec,
            pl.BlockSpec((n_seq, 1, c), seq_map),
            cc_spec,
        ] + w_specs,
        out_specs=[
            pl.BlockSpec((rows, 2 * c), lambda sb, t: (sb * n_tiles + t, 0)),
            lc_spec,
            pl.BlockSpec((n_seq, 1, c), seq_map),
            cc_spec,
        ],
        out_shape=[
            jax.ShapeDtypeStruct((b * n_tiles * n_t, 2 * c), BF16),
            jax.ShapeDtypeStruct((1, b, LRU_CONV - 1, c), F32),
            jax.ShapeDtypeStruct((b, 1, c), F32),
            jax.ShapeDtypeStruct((1, b, CONF_KERNEL - 1, c), F32),
        ],
        scratch_shapes=[
            pltpu.VMEM((n_seq, LRU_HIST + n_t, c), F32),
            pltpu.VMEM((n_seq, CONF_HIST + n_t, c), F32),
            pltpu.VMEM((n_seq, 1, c), F32),
            pltpu.VMEM((rows, c), F32),
            pltpu.VMEM((rows, c), F32),
            pltpu.VMEM((rows, c), F32),
            pltpu.VMEM((c, c), BF16),
            pltpu.VMEM((c, c), BF16),
        ],
        compiler_params=_params(("arbitrary", "arbitrary")),
        name="mixer",
    )(proj, lc, h0, cc, *weights)


def _out_proj_kernel(*refs, n_blocks):
    k = len(n_blocks)
    y_refs, x_refs = refs[:k], refs[k:2 * k]
    w_ref, g_ref, rw_ref, x1_ref, xn_ref, logit_ref, w16 = refs[2 * k:]

    @pl.when(pl.program_id(0) == 0)
    def _():
        w16[...] = w_ref[...].astype(BF16)

    def body(p):
        x1 = x_refs[p][...] + jnp.dot(y_refs[p][...], w16[...], preferred_element_type=F32)
        x1_ref[...] = x1
        xn = _rms_norm(x1, g_ref[...]).astype(BF16)
        xn_ref[...] = xn
        logit_ref[...] = jnp.dot(xn, rw_ref[...], preferred_element_type=F32)

    _for_part(pl.program_id(0), n_blocks, body)


def _out_proj(y_parts, x_parts, w_out, g2, router_w, *, tm):
    d = w_out.shape[0]
    e = router_w.shape[1]
    n_blocks = tuple(x.shape[0] // tm for x in x_parts)
    part_specs = [
        pl.BlockSpec((tm, d), _part_map(start, nb, 0)) for start, nb in zip(_part_starts(n_blocks), n_blocks)
    ]
    n = sum(n_blocks) * tm
    const = lambda i: (0, 0)
    row = lambda i: (i, 0)
    return pl.pallas_call(
        functools.partial(_out_proj_kernel, n_blocks=n_blocks),
        grid=(sum(n_blocks),),
        in_specs=part_specs + part_specs + [
            pl.BlockSpec((d, d), const, pipeline_mode=pl.Buffered(1)),
            pl.BlockSpec((1, d), const),
            pl.BlockSpec((d, e), const),
        ],
        out_specs=[pl.BlockSpec((tm, d), row), pl.BlockSpec((tm, d), row), pl.BlockSpec((tm, e), row)],
        out_shape=[
            jax.ShapeDtypeStruct((n, d), F32),
            jax.ShapeDtypeStruct((n, d), BF16),
            jax.ShapeDtypeStruct((n, e), F32),
        ],
        scratch_shapes=[pltpu.VMEM((d, d), BF16)],
        compiler_params=_params(("arbitrary",)),
        name="out_proj",
    )(*y_parts, *x_parts, w_out, g2, router_w)


def _first_argmax(work, index, sentinel):
    m = jnp.max(work, axis=0, keepdims=True)
    return jnp.min(jnp.where(work == m, index, sentinel), axis=0, keepdims=True)


def _route_kernel(logit_ref, bias_ref, eidx_ref, gate_ref, rank_ref, count_ref):
    n_tok = logit_ref.shape[1]

    scores = jax.nn.sigmoid(logit_ref[...])
    biased = scores + bias_ref[...]
    grouped = biased.reshape(N_EXPERT_GROUPS, EXPERTS_PER_GROUP, n_tok)
    in_group = lax.broadcasted_iota(jnp.int32, grouped.shape, 1)
    top1 = jnp.max(grouped, axis=1, keepdims=True)
    first1 = jnp.min(jnp.where(grouped == top1, in_group, EXPERTS_PER_GROUP), axis=1, keepdims=True)
    top2 = jnp.max(jnp.where(in_group == first1, -jnp.inf, grouped), axis=1, keepdims=True)
    group_scores = (top1 + top2).reshape(N_EXPERT_GROUPS, n_tok)

    group_id = lax.broadcasted_iota(jnp.int32, group_scores.shape, 0)
    group_on = jnp.zeros(group_scores.shape, F32)
    work = group_scores
    for _ in range(TOPK_GROUPS):
        pick = group_id == _first_argmax(work, group_id, N_EXPERT_GROUPS)
        group_on = jnp.where(pick, 1.0, group_on)
        work = jnp.where(pick, -jnp.inf, work)

    masked = jnp.where(group_on.reshape(N_EXPERT_GROUPS, 1, n_tok) > 0.0, grouped, -jnp.inf)
    work = masked.reshape(N_EXPERTS, n_tok)
    expert_id = lax.broadcasted_iota(jnp.int32, work.shape, 0)
    ids, sel = [], []
    chosen = jnp.zeros(work.shape, F32)
    for _ in range(TOP_K):
        first = _first_argmax(work, expert_id, N_EXPERTS)
        pick = expert_id == first
        ids.append(first)
        sel.append(jnp.sum(jnp.where(pick, scores, 0.0), axis=0, keepdims=True))
        chosen = jnp.where(pick, 1.0, chosen)
        work = jnp.where(pick, -jnp.inf, work)
    sel = jnp.concatenate(sel, axis=0)
    eidx_ref[...] = jnp.concatenate(ids, axis=0)
    gate_ref[...] = sel / (jnp.sum(sel, axis=0, keepdims=True) + 1e-20) * ROUTED_SCALE

    earlier = lax.broadcasted_iota(jnp.int32, (n_tok, n_tok), 0) < lax.broadcasted_iota(jnp.int32, (n_tok, n_tok), 1)
    before = jnp.dot(chosen.astype(BF16), earlier.astype(BF16), preferred_element_type=F32)
    rank_ref[...] = jnp.concatenate(
        [jnp.sum(jnp.where(expert_id == ids[k], before, 0.0), axis=0, keepdims=True) for k in range(TOP_K)], axis=0
    ).astype(jnp.int32)
    count_ref[0] = jnp.sum(chosen, axis=1, keepdims=True).astype(jnp.int32)


def _route(logits_t, bias, *, tl):
    e, n = logits_t.shape
    pick_spec = pl.BlockSpec((TOP_K, tl), lambda i: (0, i))
    return pl.pallas_call(
        _route_kernel,
        grid=(n // tl,),
        in_specs=[pl.BlockSpec((e, tl), lambda i: (0, i)), pl.BlockSpec((e, 1), lambda i: (0, 0))],
        out_specs=[pick_spec, pick_spec, pick_spec, pl.BlockSpec((1, e, 1), lambda i: (i, 0, 0))],
        out_shape=[
            jax.ShapeDtypeStruct((TOP_K, n), jnp.int32),
            jax.ShapeDtypeStruct((TOP_K, n), F32),
            jax.ShapeDtypeStruct((TOP_K, n), jnp.int32),
            jax.ShapeDtypeStruct((n // tl, e, 1), jnp.int32),
        ],
        compiler_params=_params(("arbitrary",)),
        name="route",
    )(logits_t, bias)


def _gate_lanes(gate):
    g1 = gate.astype(BF16).astype(F32)
    rest = gate - g1
    g2 = rest.astype(BF16).astype(F32)
    g3 = (rest - g2).astype(BF16).astype(F32)
    lane = lax.broadcasted_iota(jnp.int32, (gate.shape[0], LANES), 1)
    return jnp.where(lane == 0, g1, jnp.where(lane == 1, g2, jnp.where(lane == 2, g3, 0.0))).astype(BF16)


def _tab_offsets():
    offsets, col = [], len(CHUNK_ROWS)
    for cap in CHUNK_MAX:
        offsets.append((col, col + cap))
        col += 2 * cap
    return offsets, col


def _segment_copies(tab_ref, block, make_copy):
    offsets, width = _tab_offsets()
    base = block * width
    counts = []
    for k, (rows, (local_col, global_col)) in enumerate(zip(CHUNK_ROWS, offsets)):
        n_chunks = tab_ref[base + k]

        def start_one(j, rows=rows, local_col=local_col, global_col=global_col):
            make_copy(
                pl.multiple_of(tab_ref[base + local_col + j], SEG_ROWS),
                pl.multiple_of(tab_ref[base + global_col + j], SEG_ROWS),
                rows,
            ).start()

        def per_group(g, carry):
            for u in range(ISSUE_UNROLL):
                start_one(g * ISSUE_UNROLL + u)
            return carry

        def per_chunk(j, carry):
            start_one(j)
            return carry

        n_groups = n_chunks // ISSUE_UNROLL
        lax.fori_loop(0, n_groups, per_group, 0)
        lax.fori_loop(n_groups * ISSUE_UNROLL, n_chunks, per_chunk, 0)
        counts.append(n_chunks)
    return counts


def _wait_copies(counts, make_copy):
    for rows, count in zip(CHUNK_ROWS, counts):
        def wait_group(c, carry, rows=rows):
            make_copy(0, 0, rows * WAIT_GROUP).wait()
            return carry

        def wait_one(c, carry, rows=rows):
            make_copy(0, 0, rows).wait()
            return carry

        n_groups = count // WAIT_GROUP
        lax.fori_loop(0, n_groups, wait_group, 0)
        lax.fori_loop(n_groups * WAIT_GROUP, count, wait_one, 0)


def _dispatch_kernel(tab_ref, gap_ref, pos_ref, gate_ref, x_ref, xs_hbm, sorted_scr, zero_scr, pending, sem):
    step = pl.program_id(0)
    slot = step % 2
    d = x_ref.shape[1]
    n_tok = x_ref.shape[0]
    pos = pos_ref[...]
    gate = gate_ref[...]
    x = x_ref[...]
    piece_id = lax.broadcasted_iota(jnp.int32, (HOT_PIECE, n_tok), 0)
    pos_k = [jnp.broadcast_to(pos[k:k + 1, :], (HOT_PIECE, n_tok)) for k in range(TOP_K)]
    gate_k = [jnp.broadcast_to(gate[k:k + 1, :], (HOT_PIECE, n_tok)) for k in range(TOP_K)]
    for c in range(LOCAL_ROWS // SORT_CHUNK):
        one_hot, row_gate = [], []
        for p in range(SORT_CHUNK // HOT_PIECE):
            row_id = piece_id + (c * SORT_CHUNK + p * HOT_PIECE)
            hot = jnp.zeros((HOT_PIECE, n_tok), F32)
            gate_hit = jnp.zeros((HOT_PIECE, n_tok), F32)
            for k in range(TOP_K):
                hit = row_id == pos_k[k]
                hot = jnp.where(hit, 1.0, hot)
                gate_hit = jnp.where(hit, gate_k[k], gate_hit)
            one_hot.append(hot.astype(BF16))
            row_gate.append(jnp.sum(gate_hit, axis=1, keepdims=True))
        rows = jnp.dot(jnp.concatenate(one_hot, axis=0), x, preferred_element_type=F32)
        out_rows = pl.ds(c * SORT_CHUNK, SORT_CHUNK)
        sorted_scr[slot, out_rows, pl.ds(0, d)] = rows.astype(BF16)
        sorted_scr[slot, out_rows, pl.ds(d, LANES)] = _gate_lanes(jnp.concatenate(row_gate, axis=0))

    def copy_chunk(local_row, global_row, rows):
        return pltpu.make_async_copy(
            sorted_scr.at[slot, pl.ds(local_row, rows), :], xs_hbm.at[pl.ds(global_row, rows), :], sem
        )

    @pl.when(step > 0)
    def _():
        _wait_copies([pending[k] for k in range(len(CHUNK_ROWS))], copy_chunk)

    for k, count in enumerate(_segment_copies(tab_ref, step, copy_chunk)):
        pending[k] = count

    @pl.when(step == pl.num_programs(0) - 1)
    def _():
        zero_scr[...] = jnp.zeros(zero_scr.shape, BF16)
        tile_rows = zero_scr.shape[0]

        def zero_chunk(row):
            return pltpu.make_async_copy(
                zero_scr.at[pl.ds(0, SEG_ROWS), :],
                xs_hbm.at[pl.ds(pl.multiple_of(row, SEG_ROWS), SEG_ROWS), :],
                sem,
            )

        def zero_tile(row):
            return pltpu.make_async_copy(
                zero_scr, xs_hbm.at[pl.ds(pl.multiple_of(row, tile_rows), tile_rows), :], sem
            )

        def per_gap(g, total):
            start = gap_ref[g]
            chunks = gap_ref[N_EXPERTS + g]

            def per_chunk(c, carry):
                zero_chunk(start + c * SEG_ROWS).start()
                return carry

            lax.fori_loop(0, chunks, per_chunk, 0)
            return total + chunks

        n_fill = lax.fori_loop(0, N_EXPERTS, per_gap, 0)
        tail = gap_ref[2 * N_EXPERTS]
        n_tail = gap_ref[2 * N_EXPERTS + 1]

        def start_tile(c, carry):
            zero_tile(tail + c * tile_rows).start()
            return carry

        def wait_tile(c, carry):
            zero_tile(0).wait()
            return carry

        def wait_chunk(c, carry):
            zero_chunk(0).wait()
            return carry

        lax.fori_loop(0, n_tail, start_tile, 0)
        _wait_copies([pending[k] for k in range(len(CHUNK_ROWS))], copy_chunk)
        lax.fori_loop(0, n_fill, wait_chunk, 0)
        lax.fori_loop(0, n_tail, wait_tile, 0)


def _dispatch(tab, gaps, pos_t, gate_t, xn, n_sorted, *, tb, zero_rows):
    n, d = xn.shape
    width = d + LANES
    pick_spec = pl.BlockSpec((TOP_K, tb), lambda i, tab, gaps: (0, i))
    return pl.pallas_call(
        _dispatch_kernel,
        grid_spec=pltpu.PrefetchScalarGridSpec(
            num_scalar_prefetch=2,
            grid=(n // tb,),
            in_specs=[pick_spec, pick_spec, pl.BlockSpec((tb, d), lambda i, tab, gaps: (i, 0))],
            out_specs=pl.BlockSpec(memory_space=pl.ANY),
            scratch_shapes=[
                pltpu.VMEM((2, LOCAL_ROWS, width), BF16),
                pltpu.VMEM((zero_rows, width), BF16),
                pltpu.SMEM((len(CHUNK_ROWS),), jnp.int32),
                pltpu.SemaphoreType.DMA,
            ],
        ),
        out_shape=jax.ShapeDtypeStruct((n_sorted, width), BF16),
        compiler_params=_params(("arbitrary",)),
        name="dispatch",
    )(tab, gaps, pos_t, gate_t, xn)


def _experts_kernel(
    expert_ref, used_ref, fresh_ref, next_ref, src_ref, xs_ref, wg_hbm, wu_hbm, wd_hbm, ys_ref,
    wg32, wu32, wd32, wg16, wu16, wd16, sems,
):
    it = pl.program_id(0)
    d = ys_ref.shape[1]

    def weight_copies(e):
        return (
            pltpu.make_async_copy(wg_hbm.at[e], wg32, sems.at[0]),
            pltpu.make_async_copy(wu_hbm.at[e], wu32, sems.at[1]),
            pltpu.make_async_copy(wd_hbm.at[e], wd32, sems.at[2]),
        )

    @pl.when(it == 0)
    def _():
        for copy in weight_copies(expert_ref[0]):
            copy.start()

    @pl.when(used_ref[it] == 0)
    def _():
        ys_ref[...] = jnp.zeros(ys_ref.shape, ys_ref.dtype)

    @pl.when((used_ref[it] > 0) & (fresh_ref[it] == 1))
    def _():
        for copy in weight_copies(expert_ref[it]):
            copy.wait()
        wg16[...] = wg32[...].astype(BF16)
        wu16[...] = wu32[...].astype(BF16)
        wd16[...] = wd32[...].astype(BF16)

        @pl.when(next_ref[it] >= 0)
        def _():
            for copy in weight_copies(next_ref[it]):
                copy.start()

    tm = ys_ref.shape[0]
    for n_sub in range(1, tm // EXPERT_SUB + 1):
        rows = n_sub * EXPERT_SUB

        @pl.when(used_ref[it] == n_sub)
        def _(rows=rows):
            x = xs_ref[pl.ds(0, rows), pl.ds(0, d)]
            gate = jnp.sum(xs_ref[pl.ds(0, rows), pl.ds(d, LANES)].astype(F32), axis=1, keepdims=True)
            g = jnp.dot(x, wg16[...], preferred_element_type=F32)
            u = jnp.dot(x, wu16[...], preferred_element_type=F32)
            h = g * jax.nn.sigmoid(g) * u * gate
            y = jnp.dot(h.astype(BF16), wd16[...], preferred_element_type=F32)
            ys_ref[pl.ds(0, rows), :] = y.astype(ys_ref.dtype)
            if rows < tm:
                ys_ref[pl.ds(rows, tm - rows), :] = jnp.zeros((tm - rows, d), ys_ref.dtype)


def _experts(tiles, xs, w_gate, w_up, w_down, *, tm):
    p = xs.shape[0]
    d, f = w_gate.shape[1:]
    row_map = lambda i, expert, used, fresh, nxt, src: (i, 0)
    src_map = lambda i, expert, used, fresh, nxt, src: (src[i], 0)
    hbm = pl.BlockSpec(memory_space=pl.ANY)
    return pl.pallas_call(
        _experts_kernel,
        grid_spec=pltpu.PrefetchScalarGridSpec(
            num_scalar_prefetch=5,
            grid=(p // tm,),
            in_specs=[pl.BlockSpec((tm, xs.shape[1]), src_map), hbm, hbm, hbm],
            out_specs=pl.BlockSpec((tm, d), row_map),
            scratch_shapes=[
                pltpu.VMEM((d, f), F32), pltpu.VMEM((d, f), F32), pltpu.VMEM((f, d), F32),
                pltpu.VMEM((d, f), BF16), pltpu.VMEM((d, f), BF16), pltpu.VMEM((f, d), BF16),
                pltpu.SemaphoreType.DMA((3,)),
            ],
        ),
        out_shape=jax.ShapeDtypeStruct((p, d), BF16),
        compiler_params=_params(("arbitrary",)),
        name="experts",
    )(*tiles, xs, w_gate, w_up, w_down)


def _combine_kernel(tab_ref, pos_ref, x1_ref, xn_ref, sg_ref, su_ref, sd_ref, fg_ref, ys_hbm, *rest, n_blocks):
    o_refs = rest[:len(n_blocks)]
    local_scr, pending, sems = rest[len(n_blocks):]
    n_tok = x1_ref.shape[0]
    step = pl.program_id(0)
    slot = step % 2

    n_sizes = len(CHUNK_ROWS)

    def copy_chunk_into(dst_slot):
        def copy_chunk(local_row, global_row, rows):
            return pltpu.make_async_copy(
                ys_hbm.at[pl.ds(global_row, rows), :],
                local_scr.at[dst_slot, pl.ds(local_row, rows), :],
                sems.at[dst_slot],
            )

        return copy_chunk

    def fetch(block, dst_slot):
        for k, count in enumerate(_segment_copies(tab_ref, block, copy_chunk_into(dst_slot))):
            pending[dst_slot * n_sizes + k] = count

    @pl.when(step == 0)
    def _():
        local_scr[...] = jnp.zeros(local_scr.shape, local_scr.dtype)
        fetch(0, 0)

    @pl.when(step + 1 < pl.num_programs(0))
    def _():
        fetch(step + 1, 1 - slot)

    xn = xn_ref[...]
    sg = jnp.dot(xn, sg_ref[...], preferred_element_type=F32)
    su = jnp.dot(xn, su_ref[...], preferred_element_type=F32)
    acc = x1_ref[...] + jnp.dot(
        (sg * jax.nn.sigmoid(sg) * su).astype(BF16), sd_ref[...], preferred_element_type=F32
    )

    piece_id = lax.broadcasted_iota(jnp.int32, (HOT_PIECE, LANES), 1)
    one_hot = []
    for r in range(n_tok // HOT_PIECE):
        pos = pos_ref[pl.ds(r * HOT_PIECE, HOT_PIECE), :]
        pos_k = [jnp.broadcast_to(pos[:, k:k + 1], (HOT_PIECE, LANES)) for k in range(TOP_K)]
        cols = []
        for c in range(LOCAL_ROWS // LANES):
            col_id = piece_id + c * LANES
            hot = jnp.zeros((HOT_PIECE, LANES), F32)
            for k in range(TOP_K):
                hot = jnp.where(col_id == pos_k[k], 1.0, hot)
            cols.append(hot.astype(BF16))
        one_hot.append(jnp.concatenate(cols, axis=1))
    one_hot = jnp.concatenate(one_hot, axis=0)

    _wait_copies([pending[slot * n_sizes + k] for k in range(n_sizes)], copy_chunk_into(slot))
    routed = jnp.dot(one_hot, local_scr[slot], preferred_element_type=F32)
    out = _rms_norm(acc + routed, fg_ref[...])

    def body(p):
        o_refs[p][...] = out

    _for_part(pl.program_id(0), n_blocks, body)


def _combine(tab, pos, x1, xn, sh_gate, sh_up, sh_down, final_g, ys, part_rows, *, tb):
    n, d = x1.shape
    f = sh_gate.shape[1]
    n_blocks = tuple(r // tb for r in part_rows)
    const = lambda i, tab: (0, 0)
    row = lambda i, tab: (i, 0)
    return pl.pallas_call(
        functools.partial(_combine_kernel, n_blocks=n_blocks),
        grid_spec=pltpu.PrefetchScalarGridSpec(
            num_scalar_prefetch=1,
            grid=(n // tb,),
            in_specs=[
                pl.BlockSpec((tb, TOP_K), row),
                pl.BlockSpec((tb, d), row),
                pl.BlockSpec((tb, d), row),
                pl.BlockSpec((d, f), const),
                pl.BlockSpec((d, f), const),
                pl.BlockSpec((f, d), const),
                pl.BlockSpec((1, d), const),
                pl.BlockSpec(memory_space=pl.ANY),
            ],
            out_specs=[
                pl.BlockSpec((tb, d), _part_map(start, nb, 0))
                for start, nb in zip(_part_starts(n_blocks), n_blocks)
            ],
            scratch_shapes=[
                pltpu.VMEM((2, LOCAL_ROWS, d), BF16),
                pltpu.SMEM((2 * len(CHUNK_ROWS),), jnp.int32),
                pltpu.SemaphoreType.DMA((2,)),
            ],
        ),
        out_shape=[jax.ShapeDtypeStruct((r, d), F32) for r in part_rows],
        compiler_params=_params(("arbitrary",)),
        name="combine",
    )(tab, pos, x1, xn, sh_gate, sh_up, sh_down, final_g, ys)


def _segment_tables(block_counts, n_rows, *, tm):
    padded = (block_counts + SEG_ROWS - 1) // SEG_ROWS * SEG_ROWS
    local_start = jnp.cumsum(padded, axis=1) - padded
    per_expert = jnp.sum(padded, axis=0)
    region = (per_expert + tm - 1) // tm * tm
    region_start = jnp.concatenate([jnp.zeros((1,), jnp.int32), jnp.cumsum(region).astype(jnp.int32)])
    global_start = region_start[None, :N_EXPERTS] + jnp.cumsum(padded, axis=0) - padded
    counts, lists = [], []
    done = jnp.zeros_like(padded)
    for rows, cap in zip(CHUNK_ROWS, CHUNK_MAX):
        per_seg = (padded - done) // rows
        seg_end = jnp.cumsum(per_seg, axis=1)
        j = jnp.arange(cap, dtype=jnp.int32)
        seg_of_chunk = jnp.sum((seg_end[:, None, :] <= j[None, :, None]).astype(jnp.int32), axis=2)
        in_seg = seg_of_chunk[:, :, None] == jnp.arange(N_EXPERTS, dtype=jnp.int32)
        first = done - (seg_end - per_seg) * rows
        for start in (local_start, global_start):
            lists.append(jnp.sum(jnp.where(in_seg, (start + first)[:, None, :], 0), axis=2) + j[None, :] * rows)
        counts.append(seg_end[:, N_EXPERTS - 1:])
        done = done + per_seg * rows
    tab = jnp.concatenate(counts + lists, axis=1).astype(jnp.int32).reshape(-1)

    region_end = region_start[:N_EXPERTS] + per_expert
    tail = region_start[N_EXPERTS:]
    gaps = jnp.concatenate(
        [region_end, (region_start[1:] - region_end) // SEG_ROWS, tail, (n_rows - tail) // tm]
    ).astype(jnp.int32)

    tile_start = jnp.arange(n_rows // tm, dtype=jnp.int32) * tm
    expert = jnp.clip(
        jnp.sum((region_start[None, :] <= tile_start[:, None]).astype(jnp.int32), axis=1) - 1, 0, N_EXPERTS - 1
    )
    tile_end = jnp.sum(jnp.where(expert[:, None] == jnp.arange(N_EXPERTS), region_end[None, :], 0), axis=1)
    used = jnp.clip((tile_end - tile_start + EXPERT_SUB - 1) // EXPERT_SUB, 0, tm // EXPERT_SUB).astype(jnp.int32)
    fresh = jnp.concatenate([jnp.ones((1,), jnp.int32), (expert[1:] != expert[:-1]).astype(jnp.int32)])
    experts = jnp.arange(N_EXPERTS, dtype=jnp.int32)
    later = (experts[None, :] > expert[:, None]) & (per_expert[None, :] > 0)
    nxt = jnp.min(jnp.where(later, experts[None, :], N_EXPERTS), axis=1)
    nxt = jnp.where(nxt < N_EXPERTS, nxt, -1).astype(jnp.int32)
    tile_id = jnp.arange(n_rows // tm, dtype=jnp.int32)
    src = jnp.minimum(tile_id, jnp.max(jnp.where(used > 0, tile_id, 0)))
    return tab, gaps, (expert, used, fresh, nxt, src), local_start.astype(jnp.int32)


ROW_TM = 512
OUT_PROJ_TM = 256
IN_PROJ_TN = 2048
PROMPT_T = 256
SAMPLE_SEQS = 16
EXPERT_TM = 512
EXPERT_SUB = 256


def kernel(x_prompt, x_sample, state_lru_conv, state_lru_h, state_conf_conv, meta_tokens, norm1_g, w_in, lru_conv_w, lru_conv_b, lru_wa, lru_ba, lru_wi, lru_bi, lru_lambda, conf_conv_w, conf_conv_b, conf_ln_g, conf_ln_b, out_norm_a, out_norm_b, w_out, norm2_g, router_w, router_bias, exp_w_gate, exp_w_up, exp_w_down, sh_w_gate, sh_w_up, sh_w_down, final_norm_g):
    b_p, seq, d = x_prompt.shape
    b_s, t_s, _ = x_sample.shape
    n_p = b_p * seq
    n_s = b_s * t_s
    n = n_p + n_s
    x_parts = (x_prompt.reshape(n_p, d), x_sample.reshape(n_s, d))

    row = lambda v: v.reshape(1, -1)
    mixer_w = (
        lru_conv_w[0], row(lru_conv_b[0]),
        lru_wa[0], row(lru_ba[0]),
        lru_wi[0], row(lru_bi[0]),
        row(lru_lambda[0]),
        conf_conv_w[0], row(conf_conv_b[0]), row(conf_ln_g[0]), row(conf_ln_b[0]),
        row(out_norm_a[0]), row(out_norm_b[0]),
    )

    proj, proj_m = _in_proj(x_parts, meta_tokens, row(norm1_g[0]), w_in[0], tm=ROW_TM, tn=IN_PROJ_TN)
    _, m_lc, m_h, m_cc = _mixer(
        proj_m, 0,
        jnp.zeros((1, 1, LRU_CONV - 1, W_A), F32), jnp.zeros((1, 1, W_A), F32),
        jnp.zeros((1, 1, CONF_KERNEL - 1, W_B), F32),
        mixer_w, n_seq=1, n_t=N_META, n_tiles=1,
    )

    y_p, p_lc, p_h, p_cc = _mixer(
        proj, 0,
        jnp.broadcast_to(m_lc, (1, b_p) + m_lc.shape[2:]), jnp.broadcast_to(m_h, (b_p,) + m_h.shape[1:]),
        jnp.broadcast_to(m_cc, (1, b_p) + m_cc.shape[2:]),
        mixer_w, n_seq=1, n_t=PROMPT_T, n_tiles=seq // PROMPT_T,
    )
    y_s, s_lc, s_h, s_cc = _mixer(
        proj, n_p // (SAMPLE_SEQS * t_s),
        state_lru_conv, state_lru_h[0].reshape(b_s, 1, W_A), state_conf_conv,
        mixer_w, n_seq=SAMPLE_SEQS, n_t=t_s, n_tiles=1,
    )

    x1, xn2, logits = _out_proj(
        (y_p, y_s), x_parts, w_out[0], row(norm2_g[0]), router_w[0].astype(BF16), tm=OUT_PROJ_TM
    )

    eidx_t, gate_t, rank_t, block_counts = _route(logits.T, router_bias[0].reshape(N_EXPERTS, 1), tl=TOKEN_BLOCK)
    n_blocks = n // TOKEN_BLOCK
    n_sorted = n_blocks * (TOKEN_BLOCK * TOP_K + N_EXPERTS * (SEG_ROWS - 1)) + N_EXPERTS * (EXPERT_TM - 1)
    n_sorted = -(-n_sorted // EXPERT_TM) * EXPERT_TM
    tab, gaps, tiles, local_start = _segment_tables(block_counts[:, :, 0], n_sorted, tm=EXPERT_TM)
    expert_one_hot = eidx_t[:, :, None] == jnp.arange(N_EXPERTS, dtype=jnp.int32)
    token_start = jnp.repeat(local_start, TOKEN_BLOCK, axis=0)
    pos_t = jnp.sum(jnp.where(expert_one_hot, token_start[None], 0), axis=2) + rank_t

    xs = _dispatch(tab, gaps, pos_t, gate_t, xn2, n_sorted, tb=TOKEN_BLOCK, zero_rows=EXPERT_TM)
    ys = _experts(tiles, xs, exp_w_gate[0], exp_w_up[0], exp_w_down[0], tm=EXPERT_TM)
    out_p, out_s = _combine(
        tab, pos_t.T, x1, xn2,
        sh_w_gate[0].astype(BF16), sh_w_up[0].astype(BF16), sh_w_down[0].astype(BF16),
        row(final_norm_g), ys, (n_p, n_s), tb=TOKEN_BLOCK,
    )

    return (
        out_p.reshape(b_p, seq, d), out_s.reshape(b_s, t_s, d),
        p_lc, p_h.reshape(1, b_p, W_A), p_cc,
        s_lc, s_h.reshape(1, b_s, W_A), s_cc,
    )
```

```python
import functools

import jax
import jax.numpy as jnp
from jax import lax
from jax.experimental import pallas as pl
from jax.experimental.pallas import tpu as pltpu

D_MODEL = 2048
N_META = 16
W_A = 1024
W_B = 1024
LRU_CONV = 4
LRU_C = 8.0
CONF_KERNEL = 31
N_EXPERTS = 64
N_EXPERT_GROUPS = 8
EXPERTS_PER_GROUP = 8
TOPK_GROUPS = 4
TOP_K = 8
ROUTED_SCALE = 2.5
EPS = 1e-6

SUBLANES = 8
LANES = 128
TOKEN_BLOCK = 256
SEG_ROWS = SUBLANES
LOCAL_ROWS = 2560
SORT_CHUNK = 512
HOT_PIECE = 16
CHUNK_ROWS = (SEG_ROWS,)
CHUNK_MAX = (LOCAL_ROWS // SEG_ROWS,)
ISSUE_UNROLL = 8
WAIT_GROUP = 32
assert LOCAL_ROWS >= TOKEN_BLOCK * TOP_K + N_EXPERTS * (SEG_ROWS - 1) and LOCAL_ROWS % SORT_CHUNK == 0
LRU_HIST = SUBLANES
CONF_HIST = 32
VMEM_LIMIT = 56 * 1024 * 1024

F32 = jnp.float32
BF16 = jnp.bfloat16


def _params(semantics):
    return pltpu.CompilerParams(dimension_semantics=semantics, vmem_limit_bytes=VMEM_LIMIT)


def _rms_norm(x, g):
    return x * lax.rsqrt(jnp.mean(x * x, axis=-1, keepdims=True) + EPS) * g


def _part_starts(n_blocks):
    starts, s = [], 0
    for nb in n_blocks:
        starts.append(s)
        s += nb
    return starts


def _part_map(start, nb, grid_axis):
    def index_map(*ids):
        return (jnp.clip(ids[grid_axis] - start, 0, nb - 1), 0)

    return index_map


def _for_part(i, n_blocks, body):
    for p, (start, nb) in enumerate(zip(_part_starts(n_blocks), n_blocks)):
        pl.when((i >= start) & (i < start + nb))(functools.partial(body, p))


def _in_proj_kernel(*refs, n_blocks):
    x_refs = refs[:len(n_blocks)]
    meta_ref, g_ref, w_ref, o_ref, meta_o_ref, w16 = refs[len(n_blocks):]

    def project(x):
        return jnp.dot(_rms_norm(x, g_ref[...]).astype(BF16), w16[...], preferred_element_type=F32)

    @pl.when(pl.program_id(1) == 0)
    def _():
        w16[...] = w_ref[...].astype(BF16)
        meta_o_ref[...] = project(meta_ref[...])

    def body(p):
        o_ref[...] = project(x_refs[p][...])

    _for_part(pl.program_id(1), n_blocks, body)


def _in_proj(x_parts, meta, g, w, *, tm, tn):
    d, f = w.shape
    n_meta = meta.shape[0]
    n_blocks = tuple(x.shape[0] // tm for x in x_parts)
    x_specs = [
        pl.BlockSpec((tm, d), _part_map(start, nb, 1)) for start, nb in zip(_part_starts(n_blocks), n_blocks)
    ]
    return pl.pallas_call(
        functools.partial(_in_proj_kernel, n_blocks=n_blocks),
        grid=(f // tn, sum(n_blocks)),
        in_specs=x_specs + [
            pl.BlockSpec((n_meta, d), lambda j, i: (0, 0)),
            pl.BlockSpec((1, d), lambda j, i: (0, 0)),
            pl.BlockSpec((d, tn), lambda j, i: (0, j), pipeline_mode=pl.Buffered(1)),
        ],
        out_specs=[pl.BlockSpec((tm, tn), lambda j, i: (i, j)), pl.BlockSpec((n_meta, tn), lambda j, i: (0, j))],
        out_shape=[
            jax.ShapeDtypeStruct((sum(n_blocks) * tm, f), F32),
            jax.ShapeDtypeStruct((n_meta, f), F32),
        ],
        scratch_shapes=[pltpu.VMEM((d, tn), BF16)],
        compiler_params=_params(("arbitrary", "arbitrary")),
        name="in_proj",
    )(*x_parts, meta, g, w)


def _causal_conv(ext, w_ref, n_t, hist):
    taps = w_ref.shape[0]
    length = ext.shape[1]
    shifted = {0: ext}
    acc = None
    for j in range(taps):
        q, r = divmod(hist - (taps - 1) + j, SUBLANES)
        if r not in shifted:
            shifted[r] = pltpu.roll(ext, length - r, 1)
        term = w_ref[pl.ds(j, 1), :][None] * shifted[r][:, q * SUBLANES:q * SUBLANES + n_t, :]
        acc = term if acc is None else acc + term
    return acc


def _mixer_kernel(
    proj_ref, lc_ref, h0_ref, cc_ref,
    wca_ref, bca_ref, wa_ref, ba_ref, wi_ref, bi_ref, lam_ref,
    wcb_ref, bcb_ref, lng_ref, lnb_ref, nag_ref, nbg_ref,
    y_ref, lc_out, h_out, cc_out,
    ua_ext, glu_ext, hst, a_scr, u_scr, h_scr, wa_bd, wi_bd,
    *, n_seq, n_t,
):
    t = pl.program_id(1)
    rows = n_seq * n_t
    c = W_A

    @pl.when((pl.program_id(0) == 0) & (t == 0))
    def _():
        n_heads, head_dim, _ = wa_ref.shape
        for w_ref, dense in ((wa_ref, wa_bd), (wi_ref, wi_bd)):
            dense[...] = jnp.zeros(dense.shape, dense.dtype)
            for h in range(n_heads):
                block = pl.ds(h * head_dim, head_dim)
                dense[block, block] = w_ref[h].astype(dense.dtype)

    @pl.when(t == 0)
    def _():
        ua_ext[:, pl.ds(0, LRU_HIST), :] = jnp.zeros((n_seq, LRU_HIST, c), F32)
        ua_ext[:, pl.ds(LRU_HIST - (LRU_CONV - 1), LRU_CONV - 1), :] = lc_ref[...]
        glu_ext[:, pl.ds(0, CONF_HIST), :] = jnp.zeros((n_seq, CONF_HIST, c), F32)
        glu_ext[:, pl.ds(CONF_HIST - (CONF_KERNEL - 1), CONF_KERNEL - 1), :] = cc_ref[...]
        hst[...] = h0_ref[...]

    ua_ext[:, pl.ds(LRU_HIST, n_t), :] = proj_ref[:, pl.ds(0, c)].reshape(n_seq, n_t, c)
    c_a = _causal_conv(ua_ext[...], wca_ref, n_t, LRU_HIST)
    c_a = (c_a + bca_ref[...][None]).reshape(rows, c)
    c_a16 = c_a.astype(BF16)
    r = jax.nn.sigmoid(jnp.dot(c_a16, wa_bd[...], preferred_element_type=F32) + ba_ref[...])
    i = jax.nn.sigmoid(jnp.dot(c_a16, wi_bd[...], preferred_element_type=F32) + bi_ref[...])
    neg_lam = -lam_ref[...]
    softplus = jnp.maximum(neg_lam, 0.0) + jnp.log1p(jnp.exp(-jnp.abs(neg_lam)))
    log_a = (-LRU_C * r) * softplus
    a = jnp.exp(log_a)
    a_scr[...] = a
    u_scr[...] = jnp.sqrt(-jnp.tanh(log_a) * (a * a + 1.0)) * (i * c_a)

    row8 = lax.broadcasted_iota(jnp.int32, (SUBLANES, c), 0)
    groups = n_t // SUBLANES
    for s in range(n_seq):
        def scan_group(g, carry, s=s):
            off = pl.multiple_of(s * n_t + g * SUBLANES, SUBLANES)
            a = a_scr[pl.ds(off, SUBLANES), :]
            u = u_scr[pl.ds(off, SUBLANES), :]
            for d in (1, 2, 4):
                keep = row8 >= d
                a_sh = jnp.where(keep, pltpu.roll(a, d, 0), 1.0)
                u_sh = jnp.where(keep, pltpu.roll(u, d, 0), 0.0)
                u = a * u_sh + u
                a = a * a_sh
            h = a * carry + u
            h_scr[pl.ds(off, SUBLANES), :] = h
            return jnp.broadcast_to(h[SUBLANES - 1:SUBLANES, :], (SUBLANES, c))

        carry = lax.fori_loop(0, groups, scan_group, jnp.broadcast_to(hst[s], (SUBLANES, c)))
        hst[s] = carry[0:1, :]

    y_a = jax.nn.gelu(proj_ref[:, pl.ds(c, c)]) * h_scr[...]
    y_ref[:, pl.ds(0, c)] = _rms_norm(y_a, nag_ref[...]).astype(y_ref.dtype)

    glu = proj_ref[:, pl.ds(2 * c, c)] * jax.nn.sigmoid(proj_ref[:, pl.ds(3 * c, c)])
    glu_ext[:, pl.ds(CONF_HIST, n_t), :] = glu.reshape(n_seq, n_t, c)
    c_b = _causal_conv(glu_ext[...], wcb_ref, n_t, CONF_HIST)
    c_b = (c_b + bcb_ref[...][None]).reshape(rows, c)
    mu = jnp.mean(c_b, axis=-1, keepdims=True)
    cen = c_b - mu
    var = jnp.mean(cen * cen, axis=-1, keepdims=True)
    ln = cen * lax.rsqrt(var + EPS) * lng_ref[...] + lnb_ref[...]
    y_b = ln * jax.nn.sigmoid(ln)
    y_ref[:, pl.ds(c, c)] = _rms_norm(y_b, nbg_ref[...]).astype(y_ref.dtype)

    lc_out[...] = ua_ext[:, pl.ds(LRU_HIST + n_t - (LRU_CONV - 1), LRU_CONV - 1), :]
    cc_out[...] = glu_ext[:, pl.ds(CONF_HIST + n_t - (CONF_KERNEL - 1), CONF_KERNEL - 1), :]
    h_out[...] = hst[...]
    ua_ext[:, pl.ds(0, LRU_HIST), :] = ua_ext[:, pl.ds(n_t, LRU_HIST), :]
    glu_ext[:, pl.ds(0, CONF_HIST), :] = glu_ext[:, pl.ds(n_t, CONF_HIST), :]


def _mixer(proj, row_block0, lc, h0, cc, weights, *, n_seq, n_t, n_tiles):
    b = lc.shape[1]
    c = W_A
    rows = n_seq * n_t
    n_sb = b // n_seq

    def row_map(sb, t):
        return (row_block0 + sb * n_tiles + t, 0)

    def seq_map(sb, t):
        return (sb, 0, 0)

    def depth_seq_map(sb, t):
        return (0, sb, 0, 0)

    w_specs = [pl.BlockSpec(w.shape, (lambda sb, t, nd=w.ndim: (0,) * nd)) for w in weights]
    lc_spec = pl.BlockSpec((None, n_seq, LRU_CONV - 1, c), depth_seq_map)
    cc_spec = pl.BlockSpec((None, n_seq, CONF_KERNEL - 1, c), depth_seq_map)
    return pl.pallas_call(
        functools.partial(_mixer_kernel, n_seq=n_seq, n_t=n_t),
        grid=(n_sb, n_tiles),
        in_specs=[
            pl.BlockSpec((rows, 4 * c), row_map),
            lc_spec,
            pl.BlockSpec((n_seq, 1, c), seq_map),
            cc_spec,
        ] + w_specs,
        out_specs=[
            pl.BlockSpec((rows, 2 * c), lambda sb, t: (sb * n_tiles + t, 0)),
            lc_spec,
            pl.BlockSpec((n_seq, 1, c), seq_map),
            cc_spec,
        ],
        out_shape=[
            jax.ShapeDtypeStruct((b * n_tiles * n_t, 2 * c), BF16),
            jax.ShapeDtypeStruct((1, b, LRU_CONV - 1, c), F32),
            jax.ShapeDtypeStruct((b, 1, c), F32),
            jax.ShapeDtypeStruct((1, b, CONF_KERNEL - 1, c), F32),
        ],
        scratch_shapes=[
            pltpu.VMEM((n_seq, LRU_HIST + n_t, c), F32),
            pltpu.VMEM((n_seq, CONF_HIST + n_t, c), F32),
            pltpu.VMEM((n_seq, 1, c), F32),
            pltpu.VMEM((rows, c), F32),
            pltpu.VMEM((rows, c), F32),
            pltpu.VMEM((rows, c), F32),
            pltpu.VMEM((c, c), BF16),
            pltpu.VMEM((c, c), BF16),
        ],
        compiler_params=_params(("arbitrary", "arbitrary")),
        name="mixer",
    )(proj, lc, h0, cc, *weights)


def _out_proj_kernel(*refs, n_blocks):
    k = len(n_blocks)
    y_refs, x_refs = refs[:k], refs[k:2 * k]
    w_ref, g_ref, rw_ref, x1_ref, xn_ref, logit_ref, w16 = refs[2 * k:]

    @pl.when(pl.program_id(0) == 0)
    def _():
        w16[...] = w_ref[...].astype(BF16)

    def body(p):
        x1 = x_refs[p][...] + jnp.dot(y_refs[p][...], w16[...], preferred_element_type=F32)
        x1_ref[...] = x1
        xn = _rms_norm(x1, g_ref[...]).astype(BF16)
        xn_ref[...] = xn
        logit_ref[...] = jnp.dot(xn, rw_ref[...], preferred_element_type=F32)

    _for_part(pl.program_id(0), n_blocks, body)


def _out_proj(y_parts, x_parts, w_out, g2, router_w, *, tm):
    d = w_out.shape[0]
    e = router_w.shape[1]
    n_blocks = tuple(x.shape[0] // tm for x in x_parts)
    part_specs = [
        pl.BlockSpec((tm, d), _part_map(start, nb, 0)) for start, nb in zip(_part_starts(n_blocks), n_blocks)
    ]
    n = sum(n_blocks) * tm
    const = lambda i: (0, 0)
    row = lambda i: (i, 0)
    return pl.pallas_call(
        functools.partial(_out_proj_kernel, n_blocks=n_blocks),
        grid=(sum(n_blocks),),
        in_specs=part_specs + part_specs + [
            pl.BlockSpec((d, d), const, pipeline_mode=pl.Buffered(1)),
            pl.BlockSpec((1, d), const),
            pl.BlockSpec((d, e), const),
        ],
        out_specs=[pl.BlockSpec((tm, d), row), pl.BlockSpec((tm, d), row), pl.BlockSpec((tm, e), row)],
        out_shape=[
            jax.ShapeDtypeStruct((n, d), F32),
            jax.ShapeDtypeStruct((n, d), BF16),
            jax.ShapeDtypeStruct((n, e), F32),
        ],
        scratch_shapes=[pltpu.VMEM((d, d), BF16)],
        compiler_params=_params(("arbitrary",)),
        name="out_proj",
    )(*y_parts, *x_parts, w_out, g2, router_w)


def _first_argmax(work, index, sentinel):
    m = jnp.max(work, axis=0, keepdims=True)
    return jnp.min(jnp.where(work == m, index, sentinel), axis=0, keepdims=True)


def _route_kernel(logit_ref, bias_ref, eidx_ref, gate_ref, rank_ref, count_ref):
    n_tok = logit_ref.shape[1]

    scores = jax.nn.sigmoid(logit_ref[...])
    biased = scores + bias_ref[...]
    grouped = biased.reshape(N_EXPERT_GROUPS, EXPERTS_PER_GROUP, n_tok)
    in_group = lax.broadcasted_iota(jnp.int32, grouped.shape, 1)
    top1 = jnp.max(grouped, axis=1, keepdims=True)
    first1 = jnp.min(jnp.where(grouped == top1, in_group, EXPERTS_PER_GROUP), axis=1, keepdims=True)
    top2 = jnp.max(jnp.where(in_group == first1, -jnp.inf, grouped), axis=1, keepdims=True)
    group_scores = (top1 + top2).reshape(N_EXPERT_GROUPS, n_tok)

    group_id = lax.broadcasted_iota(jnp.int32, group_scores.shape, 0)
    group_on = jnp.zeros(group_scores.shape, F32)
    work = group_scores
    for _ in range(TOPK_GROUPS):
        pick = group_id == _first_argmax(work, group_id, N_EXPERT_GROUPS)
        group_on = jnp.where(pick, 1.0, group_on)
        work = jnp.where(pick, -jnp.inf, work)

    masked = jnp.where(group_on.reshape(N_EXPERT_GROUPS, 1, n_tok) > 0.0, grouped, -jnp.inf)
    work = masked.reshape(N_EXPERTS, n_tok)
    expert_id = lax.broadcasted_iota(jnp.int32, work.shape, 0)
    ids, sel = [], []
    chosen = jnp.zeros(work.shape, F32)
    for _ in range(TOP_K):
        first = _first_argmax(work, expert_id, N_EXPERTS)
        pick = expert_id == first
        ids.append(first)
        sel.append(jnp.sum(jnp.where(pick, scores, 0.0), axis=0, keepdims=True))
        chosen = jnp.where(pick, 1.0, chosen)
        work = jnp.where(pick, -jnp.inf, work)
    sel = jnp.concatenate(sel, axis=0)
    eidx_ref[...] = jnp.concatenate(ids, axis=0)
    gate_ref[...] = sel / (jnp.sum(sel, axis=0, keepdims=True) + 1e-20) * ROUTED_SCALE

    earlier = lax.broadcasted_iota(jnp.int32, (n_tok, n_tok), 0) < lax.broadcasted_iota(jnp.int32, (n_tok, n_tok), 1)
    before = jnp.dot(chosen.astype(BF16), earlier.astype(BF16), preferred_element_type=F32)
    rank_ref[...] = jnp.concatenate(
        [jnp.sum(jnp.where(expert_id == ids[k], before, 0.0), axis=0, keepdims=True) for k in range(TOP_K)], axis=0
    ).astype(jnp.int32)
    count_ref[0] = jnp.sum(chosen, axis=1, keepdims=True).astype(jnp.int32)


def _route(logits_t, bias, *, tl):
    e, n = logits_t.shape
    pick_spec = pl.BlockSpec((TOP_K, tl), lambda i: (0, i))
    return pl.pallas_call(
        _route_kernel,
        grid=(n // tl,),
        in_specs=[pl.BlockSpec((e, tl), lambda i: (0, i)), pl.BlockSpec((e, 1), lambda i: (0, 0))],
        out_specs=[pick_spec, pick_spec, pick_spec, pl.BlockSpec((1, e, 1), lambda i: (i, 0, 0))],
        out_shape=[
            jax.ShapeDtypeStruct((TOP_K, n), jnp.int32),
            jax.ShapeDtypeStruct((TOP_K, n), F32),
            jax.ShapeDtypeStruct((TOP_K, n), jnp.int32),
            jax.ShapeDtypeStruct((n // tl, e, 1), jnp.int32),
        ],
        compiler_params=_params(("arbitrary",)),
        name="route",
    )(logits_t, bias)


def _gate_lanes(gate):
    g1 = gate.astype(BF16).astype(F32)
    rest = gate - g1
    g2 = rest.astype(BF16).astype(F32)
    g3 = (rest - g2).astype(BF16).astype(F32)
    lane = lax.broadcasted_iota(jnp.int32, (gate.shape[0], LANES), 1)
    return jnp.where(lane == 0, g1, jnp.where(lane == 1, g2, jnp.where(lane == 2, g3, 0.0))).astype(BF16)


def _tab_offsets():
    offsets, col = [], len(CHUNK_ROWS)
    for cap in CHUNK_MAX:
        offsets.append((col, col + cap))
        col += 2 * cap
    return offsets, col


def _segment_copies(tab_ref, block, make_copy):
    offsets, width = _tab_offsets()
    base = block * width
    counts = []
    for k, (rows, (local_col, global_col)) in enumerate(zip(CHUNK_ROWS, offsets)):
        n_chunks = tab_ref[base + k]

        def start_one(j, rows=rows, local_col=local_col, global_col=global_col):
            make_copy(
                pl.multiple_of(tab_ref[base + local_col + j], SEG_ROWS),
                pl.multiple_of(tab_ref[base + global_col + j], SEG_ROWS),
                rows,
            ).start()

        def per_group(g, carry):
            for u in range(ISSUE_UNROLL):
                start_one(g * ISSUE_UNROLL + u)
            return carry

        def per_chunk(j, carry):
            start_one(j)
            return carry

        n_groups = n_chunks // ISSUE_UNROLL
        lax.fori_loop(0, n_groups, per_group, 0)
        lax.fori_loop(n_groups * ISSUE_UNROLL, n_chunks, per_chunk, 0)
        counts.append(n_chunks)
    return counts


def _wait_copies(counts, make_copy):
    for rows, count in zip(CHUNK_ROWS, counts):
        def wait_group(c, carry, rows=rows):
            make_copy(0, 0, rows * WAIT_GROUP).wait()
            return carry

        def wait_one(c, carry, rows=rows):
            make_copy(0, 0, rows).wait()
            return carry

        n_groups = count // WAIT_GROUP
        lax.fori_loop(0, n_groups, wait_group, 0)
        lax.fori_loop(n_groups * WAIT_GROUP, count, wait_one, 0)


def _dispatch_kernel(tab_ref, gap_ref, pos_ref, gate_ref, x_ref, xs_hbm, sorted_scr, zero_scr, pending, sem):
    step = pl.program_id(0)
    slot = step % 2
    d = x_ref.shape[1]
    n_tok = x_ref.shape[0]
    pos = pos_ref[...]
    gate = gate_ref[...]
    x = x_ref[...]
    piece_id = lax.broadcasted_iota(jnp.int32, (HOT_PIECE, n_tok), 0)
    pos_k = [jnp.broadcast_to(pos[k:k + 1, :], (HOT_PIECE, n_tok)) for k in range(TOP_K)]
    gate_k = [jnp.broadcast_to(gate[k:k + 1, :], (HOT_PIECE, n_tok)) for k in range(TOP_K)]
    for c in range(LOCAL_ROWS // SORT_CHUNK):
        one_hot, row_gate = [], []
        for p in range(SORT_CHUNK // HOT_PIECE):
            row_id = piece_id + (c * SORT_CHUNK + p * HOT_PIECE)
            hot = jnp.zeros((HOT_PIECE, n_tok), F32)
            gate_hit = jnp.zeros((HOT_PIECE, n_tok), F32)
            for k in range(TOP_K):
                hit = row_id == pos_k[k]
                hot = jnp.where(hit, 1.0, hot)
                gate_hit = jnp.where(hit, gate_k[k], gate_hit)
            one_hot.append(hot.astype(BF16))
            row_gate.append(jnp.sum(gate_hit, axis=1, keepdims=True))
        rows = jnp.dot(jnp.concatenate(one_hot, axis=0), x, preferred_element_type=F32)
        out_rows = pl.ds(c * SORT_CHUNK, SORT_CHUNK)
        sorted_scr[slot, out_rows, pl.ds(0, d)] = rows.astype(BF16)
        sorted_scr[slot, out_rows, pl.ds(d, LANES)] = _gate_lanes(jnp.concatenate(row_gate, axis=0))

    def copy_chunk_from(src_slot):
        def copy_chunk(local_row, global_row, rows):
            return pltpu.make_async_copy(
                sorted_scr.at[src_slot, pl.ds(local_row, rows), :], xs_hbm.at[pl.ds(global_row, rows), :], sem
            )

        return copy_chunk

    copy_chunk = copy_chunk_from(0)

    @pl.when(step > 0)
    def _():
        _wait_copies([pending[k] for k in range(len(CHUNK_ROWS))], copy_chunk)

    for s in range(2):
        @pl.when(slot == s)
        def _(s=s):
            for k, count in enumerate(_segment_copies(tab_ref, step, copy_chunk_from(s))):
                pending[k] = count

    @pl.when(step == pl.num_programs(0) - 1)
    def _():
        zero_scr[...] = jnp.zeros(zero_scr.shape, BF16)
        tile_rows = zero_scr.shape[0]

        def zero_chunk(row):
            return pltpu.make_async_copy(
                zero_scr.at[pl.ds(0, SEG_ROWS), :],
                xs_hbm.at[pl.ds(pl.multiple_of(row, SEG_ROWS), SEG_ROWS), :],
                sem,
            )

        def zero_tile(row):
            return pltpu.make_async_copy(
                zero_scr, xs_hbm.at[pl.ds(pl.multiple_of(row, tile_rows), tile_rows), :], sem
            )

        n_gaps = N_EXPERTS + 1

        def per_gap(g, totals):
            chunk_start, chunks = gap_ref[g], gap_ref[n_gaps + g]
            tile_start, tiles = gap_ref[2 * n_gaps + g], gap_ref[3 * n_gaps + g]

            def per_chunk(c, carry):
                zero_chunk(chunk_start + c * SEG_ROWS).start()
                return carry

            def per_tile(c, carry):
                zero_tile(tile_start + c * tile_rows).start()
                return carry

            lax.fori_loop(0, chunks, per_chunk, 0)
            lax.fori_loop(0, tiles, per_tile, 0)
            return totals[0] + chunks, totals[1] + tiles

        n_fill, n_tail = lax.fori_loop(0, n_gaps, per_gap, (0, 0))

        def wait_tile(c, carry):
            zero_tile(0).wait()
            return carry

        def wait_chunk(c, carry):
            zero_chunk(0).wait()
            return carry

        _wait_copies([pending[k] for k in range(len(CHUNK_ROWS))], copy_chunk)
        lax.fori_loop(0, n_fill, wait_chunk, 0)
        lax.fori_loop(0, n_tail, wait_tile, 0)


def _dispatch(tab, gaps, pos_t, gate_t, xn, n_sorted, *, tb, zero_rows):
    n, d = xn.shape
    width = d + LANES
    pick_spec = pl.BlockSpec((TOP_K, tb), lambda i, tab, gaps: (0, i))
    return pl.pallas_call(
        _dispatch_kernel,
        grid_spec=pltpu.PrefetchScalarGridSpec(
            num_scalar_prefetch=2,
            grid=(n // tb,),
            in_specs=[pick_spec, pick_spec, pl.BlockSpec((tb, d), lambda i, tab, gaps: (i, 0))],
            out_specs=pl.BlockSpec(memory_space=pl.ANY),
            scratch_shapes=[
                pltpu.VMEM((2, LOCAL_ROWS, width), BF16),
                pltpu.VMEM((zero_rows, width), BF16),
                pltpu.SMEM((len(CHUNK_ROWS),), jnp.int32),
                pltpu.SemaphoreType.DMA,
            ],
        ),
        out_shape=jax.ShapeDtypeStruct((n_sorted, width), BF16),
        compiler_params=_params(("arbitrary",)),
        name="dispatch",
    )(tab, gaps, pos_t, gate_t, xn)


def _experts_kernel(
    expert_ref, used_ref, fresh_ref, next_ref, src_ref, xs_ref, wg_hbm, wu_hbm, wd_hbm, ys_ref,
    wg32, wu32, wd32, wg16, wu16, wd16, sems,
):
    it = pl.program_id(0)
    d = ys_ref.shape[1]

    def weight_copies(e):
        return (
            pltpu.make_async_copy(wg_hbm.at[e], wg32, sems.at[0]),
            pltpu.make_async_copy(wu_hbm.at[e], wu32, sems.at[1]),
            pltpu.make_async_copy(wd_hbm.at[e], wd32, sems.at[2]),
        )

    @pl.when(it == 0)
    def _():
        for copy in weight_copies(expert_ref[0]):
            copy.start()

    @pl.when(used_ref[it] == 0)
    def _():
        ys_ref[...] = jnp.zeros(ys_ref.shape, ys_ref.dtype)

    @pl.when((used_ref[it] > 0) & (fresh_ref[it] == 1))
    def _():
        for copy in weight_copies(expert_ref[it]):
            copy.wait()
        wg16[...] = wg32[...].astype(BF16)
        wu16[...] = wu32[...].astype(BF16)
        wd16[...] = wd32[...].astype(BF16)

        @pl.when(next_ref[it] >= 0)
        def _():
            for copy in weight_copies(next_ref[it]):
                copy.start()

    tm = ys_ref.shape[0]
    for n_sub in range(1, tm // EXPERT_SUB + 1):
        rows = n_sub * EXPERT_SUB

        @pl.when(used_ref[it] == n_sub)
        def _(rows=rows):
            x = xs_ref[pl.ds(0, rows), pl.ds(0, d)]
            gate = jnp.sum(xs_ref[pl.ds(0, rows), pl.ds(d, LANES)].astype(F32), axis=1, keepdims=True)
            g = jnp.dot(x, wg16[...], preferred_element_type=F32)
            u = jnp.dot(x, wu16[...], preferred_element_type=F32)
            h = g * jax.nn.sigmoid(g) * u * gate
            y = jnp.dot(h.astype(BF16), wd16[...], preferred_element_type=F32)
            ys_ref[pl.ds(0, rows), :] = y.astype(ys_ref.dtype)
            if rows < tm:
                ys_ref[pl.ds(rows, tm - rows), :] = jnp.zeros((tm - rows, d), ys_ref.dtype)


def _experts(tiles, xs, w_gate, w_up, w_down, *, tm):
    p = xs.shape[0]
    d, f = w_gate.shape[1:]
    row_map = lambda i, expert, used, fresh, nxt, src: (i, 0)
    src_map = lambda i, expert, used, fresh, nxt, src: (src[i], 0)
    hbm = pl.BlockSpec(memory_space=pl.ANY)
    return pl.pallas_call(
        _experts_kernel,
        grid_spec=pltpu.PrefetchScalarGridSpec(
            num_scalar_prefetch=5,
            grid=(p // tm,),
            in_specs=[pl.BlockSpec((tm, xs.shape[1]), src_map), hbm, hbm, hbm],
            out_specs=pl.BlockSpec((tm, d), row_map),
            scratch_shapes=[
                pltpu.VMEM((d, f), F32), pltpu.VMEM((d, f), F32), pltpu.VMEM((f, d), F32),
                pltpu.VMEM((d, f), BF16), pltpu.VMEM((d, f), BF16), pltpu.VMEM((f, d), BF16),
                pltpu.SemaphoreType.DMA((3,)),
            ],
        ),
        out_shape=jax.ShapeDtypeStruct((p, d), BF16),
        compiler_params=_params(("arbitrary",)),
        name="experts",
    )(*tiles, xs, w_gate, w_up, w_down)


def _combine_kernel(tab_ref, pos_ref, x1_ref, xn_ref, sg_ref, su_ref, sd_ref, fg_ref, ys_hbm, *rest, n_blocks):
    o_refs = rest[:len(n_blocks)]
    local_scr, pending, sems = rest[len(n_blocks):]
    n_tok = x1_ref.shape[0]
    step = pl.program_id(0)
    slot = step % 2

    n_sizes = len(CHUNK_ROWS)

    def copy_chunk_into(dst_slot):
        def copy_chunk(local_row, global_row, rows):
            return pltpu.make_async_copy(
                ys_hbm.at[pl.ds(global_row, rows), :],
                local_scr.at[dst_slot, pl.ds(local_row, rows), :],
                sems.at[dst_slot],
            )

        return copy_chunk

    def fetch(block, dst_slot):
        for k, count in enumerate(_segment_copies(tab_ref, block, copy_chunk_into(dst_slot))):
            pending[dst_slot * n_sizes + k] = count

    @pl.when(step == 0)
    def _():
        local_scr[...] = jnp.zeros(local_scr.shape, local_scr.dtype)
        fetch(0, 0)

    for s in range(2):
        @pl.when((step + 1 < pl.num_programs(0)) & (slot == s))
        def _(s=s):
            fetch(step + 1, 1 - s)

    xn = xn_ref[...]
    sg = jnp.dot(xn, sg_ref[...], preferred_element_type=F32)
    su = jnp.dot(xn, su_ref[...], preferred_element_type=F32)
    acc = x1_ref[...] + jnp.dot(
        (sg * jax.nn.sigmoid(sg) * su).astype(BF16), sd_ref[...], preferred_element_type=F32
    )

    piece_id = lax.broadcasted_iota(jnp.int32, (HOT_PIECE, LANES), 1)
    one_hot = []
    for r in range(n_tok // HOT_PIECE):
        pos = pos_ref[pl.ds(r * HOT_PIECE, HOT_PIECE), :]
        pos_k = [jnp.broadcast_to(pos[:, k:k + 1], (HOT_PIECE, LANES)) for k in range(TOP_K)]
        cols = []
        for c in range(LOCAL_ROWS // LANES):
            col_id = piece_id + c * LANES
            hot = jnp.zeros((HOT_PIECE, LANES), F32)
            for k in range(TOP_K):
                hot = jnp.where(col_id == pos_k[k], 1.0, hot)
            cols.append(hot.astype(BF16))
        one_hot.append(jnp.concatenate(cols, axis=1))
    one_hot = jnp.concatenate(one_hot, axis=0)

    _wait_copies([pending[slot * n_sizes + k] for k in range(n_sizes)], copy_chunk_into(slot))
    routed = jnp.dot(one_hot, local_scr[slot], preferred_element_type=F32)
    out = _rms_norm(acc + routed, fg_ref[...])

    def body(p):
        o_refs[p][...] = out

    _for_part(pl.program_id(0), n_blocks, body)


def _combine(tab, pos, x1, xn, sh_gate, sh_up, sh_down, final_g, ys, part_rows, *, tb):
    n, d = x1.shape
    f = sh_gate.shape[1]
    n_blocks = tuple(r // tb for r in part_rows)
    const = lambda i, tab: (0, 0)
    row = lambda i, tab: (i, 0)
    return pl.pallas_call(
        functools.partial(_combine_kernel, n_blocks=n_blocks),
        grid_spec=pltpu.PrefetchScalarGridSpec(
            num_scalar_prefetch=1,
            grid=(n // tb,),
            in_specs=[
                pl.BlockSpec((tb, TOP_K), row),
                pl.BlockSpec((tb, d), row),
                pl.BlockSpec((tb, d), row),
                pl.BlockSpec((d, f), const, pipeline_mode=pl.Buffered(1)),
                pl.BlockSpec((d, f), const, pipeline_mode=pl.Buffered(1)),
                pl.BlockSpec((f, d), const, pipeline_mode=pl.Buffered(1)),
                pl.BlockSpec((1, d), const),
                pl.BlockSpec(memory_space=pl.ANY),
            ],
            out_specs=[
                pl.BlockSpec((tb, d), _part_map(start, nb, 0))
                for start, nb in zip(_part_starts(n_blocks), n_blocks)
            ],
            scratch_shapes=[
                pltpu.VMEM((2, LOCAL_ROWS, d), BF16),
                pltpu.SMEM((2 * len(CHUNK_ROWS),), jnp.int32),
                pltpu.SemaphoreType.DMA((2,)),
            ],
        ),
        out_shape=[jax.ShapeDtypeStruct((r, d), F32) for r in part_rows],
        compiler_params=_params(("arbitrary",)),
        name="combine",
    )(tab, pos, x1, xn, sh_gate, sh_up, sh_down, final_g, ys)


def _segment_tables(block_counts, n_rows, *, tm):
    padded = (block_counts + SEG_ROWS - 1) // SEG_ROWS * SEG_ROWS
    local_start = jnp.cumsum(padded, axis=1) - padded
    per_expert = jnp.sum(padded, axis=0)
    region = (per_expert + tm - 1) // tm * tm
    region_start = jnp.concatenate([jnp.zeros((1,), jnp.int32), jnp.cumsum(region).astype(jnp.int32)])
    global_start = region_start[None, :N_EXPERTS] + jnp.cumsum(padded, axis=0) - padded
    counts, lists = [], []
    done = jnp.zeros_like(padded)
    for rows, cap in zip(CHUNK_ROWS, CHUNK_MAX):
        per_seg = (padded - done) // rows
        seg_end = jnp.cumsum(per_seg, axis=1)
        j = jnp.arange(cap, dtype=jnp.int32)
        seg_of_chunk = jnp.sum((seg_end[:, None, :] <= j[None, :, None]).astype(jnp.int32), axis=2)
        in_seg = seg_of_chunk[:, :, None] == jnp.arange(N_EXPERTS, dtype=jnp.int32)
        first = done - (seg_end - per_seg) * rows
        for start in (local_start, global_start):
            lists.append(jnp.sum(jnp.where(in_seg, (start + first)[:, None, :], 0), axis=2) + j[None, :] * rows)
        counts.append(seg_end[:, N_EXPERTS - 1:])
        done = done + per_seg * rows
    tab = jnp.concatenate(counts + lists, axis=1).astype(jnp.int32).reshape(-1)

    region_end = region_start[:N_EXPERTS] + per_expert
    gap_start = jnp.concatenate([region_end, region_start[N_EXPERTS:]])
    gap_end = jnp.concatenate([region_start[1:], jnp.full((1,), n_rows, jnp.int32)])
    sub_start = jnp.minimum((gap_start + EXPERT_SUB - 1) // EXPERT_SUB * EXPERT_SUB, gap_end)
    gaps = jnp.concatenate(
        [gap_start, (sub_start - gap_start) // SEG_ROWS, sub_start, (gap_end - sub_start) // EXPERT_SUB]
    ).astype(jnp.int32)

    tile_start = jnp.arange(n_rows // tm, dtype=jnp.int32) * tm
    expert = jnp.clip(
        jnp.sum((region_start[None, :] <= tile_start[:, None]).astype(jnp.int32), axis=1) - 1, 0, N_EXPERTS - 1
    )
    tile_end = jnp.sum(jnp.where(expert[:, None] == jnp.arange(N_EXPERTS), region_end[None, :], 0), axis=1)
    used = jnp.clip((tile_end - tile_start + EXPERT_SUB - 1) // EXPERT_SUB, 0, tm // EXPERT_SUB).astype(jnp.int32)
    fresh = jnp.concatenate([jnp.ones((1,), jnp.int32), (expert[1:] != expert[:-1]).astype(jnp.int32)])
    experts = jnp.arange(N_EXPERTS, dtype=jnp.int32)
    later = (experts[None, :] > expert[:, None]) & (per_expert[None, :] > 0)
    nxt = jnp.min(jnp.where(later, experts[None, :], N_EXPERTS), axis=1)
    nxt = jnp.where(nxt < N_EXPERTS, nxt, -1).astype(jnp.int32)
    tile_id = jnp.arange(n_rows // tm, dtype=jnp.int32)
    src = jnp.minimum(tile_id, jnp.max(jnp.where(used > 0, tile_id, 0)))
    return tab, gaps, (expert, used, fresh, nxt, src), local_start.astype(jnp.int32)


ROW_TM = 512
OUT_PROJ_TM = 256
IN_PROJ_TN = 2048
PROMPT_T = 256
SAMPLE_SEQS = 16
EXPERT_TM = 512
EXPERT_SUB = 256


def kernel(x_prompt, x_sample, state_lru_conv, state_lru_h, state_conf_conv, meta_tokens, norm1_g, w_in, lru_conv_w, lru_conv_b, lru_wa, lru_ba, lru_wi, lru_bi, lru_lambda, conf_conv_w, conf_conv_b, conf_ln_g, conf_ln_b, out_norm_a, out_norm_b, w_out, norm2_g, router_w, router_bias, exp_w_gate, exp_w_up, exp_w_down, sh_w_gate, sh_w_up, sh_w_down, final_norm_g):
    b_p, seq, d = x_prompt.shape
    b_s, t_s, _ = x_sample.shape
    n_p = b_p * seq
    n_s = b_s * t_s
    n = n_p + n_s
    x_parts = (x_prompt.reshape(n_p, d), x_sample.reshape(n_s, d))

    row = lambda v: v.reshape(1, -1)
    mixer_w = (
        lru_conv_w[0], row(lru_conv_b[0]),
        lru_wa[0], row(lru_ba[0]),
        lru_wi[0], row(lru_bi[0]),
        row(lru_lambda[0]),
        conf_conv_w[0], row(conf_conv_b[0]), row(conf_ln_g[0]), row(conf_ln_b[0]),
        row(out_norm_a[0]), row(out_norm_b[0]),
    )

    proj, proj_m = _in_proj(x_parts, meta_tokens, row(norm1_g[0]), w_in[0], tm=ROW_TM, tn=IN_PROJ_TN)
    _, m_lc, m_h, m_cc = _mixer(
        proj_m, 0,
        jnp.zeros((1, 1, LRU_CONV - 1, W_A), F32), jnp.zeros((1, 1, W_A), F32),
        jnp.zeros((1, 1, CONF_KERNEL - 1, W_B), F32),
        mixer_w, n_seq=1, n_t=N_META, n_tiles=1,
    )

    y_p, p_lc, p_h, p_cc = _mixer(
        proj, 0,
        jnp.broadcast_to(m_lc, (1, b_p) + m_lc.shape[2:]), jnp.broadcast_to(m_h, (b_p,) + m_h.shape[1:]),
        jnp.broadcast_to(m_cc, (1, b_p) + m_cc.shape[2:]),
        mixer_w, n_seq=1, n_t=PROMPT_T, n_tiles=seq // PROMPT_T,
    )
    y_s, s_lc, s_h, s_cc = _mixer(
        proj, n_p // (SAMPLE_SEQS * t_s),
        state_lru_conv, state_lru_h[0].reshape(b_s, 1, W_A), state_conf_conv,
        mixer_w, n_seq=SAMPLE_SEQS, n_t=t_s, n_tiles=1,
    )

    x1, xn2, logits = _out_proj(
        (y_p, y_s), x_parts, w_out[0], row(norm2_g[0]), router_w[0].astype(BF16), tm=OUT_PROJ_TM
    )

    eidx_t, gate_t, rank_t, block_counts = _route(logits.T, router_bias[0].reshape(N_EXPERTS, 1), tl=TOKEN_BLOCK)
    n_blocks = n // TOKEN_BLOCK
    n_sorted = n_blocks * (TOKEN_BLOCK * TOP_K + N_EXPERTS * (SEG_ROWS - 1)) + N_EXPERTS * (EXPERT_TM - 1)
    n_sorted = -(-n_sorted // EXPERT_TM) * EXPERT_TM
    tab, gaps, tiles, local_start = _segment_tables(block_counts[:, :, 0], n_sorted, tm=EXPERT_TM)
    expert_one_hot = eidx_t[:, :, None] == jnp.arange(N_EXPERTS, dtype=jnp.int32)
    token_start = jnp.repeat(local_start, TOKEN_BLOCK, axis=0)
    pos_t = jnp.sum(jnp.where(expert_one_hot, token_start[None], 0), axis=2) + rank_t

    xs = _dispatch(tab, gaps, pos_t, gate_t, xn2, n_sorted, tb=TOKEN_BLOCK, zero_rows=EXPERT_SUB)
    ys = _experts(tiles, xs, exp_w_gate[0], exp_w_up[0], exp_w_down[0], tm=EXPERT_TM)
    out_p, out_s = _combine(
        tab, pos_t.T, x1, xn2,
        sh_w_gate[0].astype(BF16), sh_w_up[0].astype(BF16), sh_w_down[0].astype(BF16),
        row(final_norm_g), ys, (n_p, n_s), tb=TOKEN_BLOCK,
    )

    return (
        out_p.reshape(b_p, seq, d), out_s.reshape(b_s, t_s, d),
        p_lc, p_h.reshape(1, b_p, W_A), p_cc,
        s_lc, s_h.reshape(1, b_s, W_A), s_cc,
    )
```

```python
import functools

import jax
import jax.numpy as jnp
from jax import lax
from jax.experimental import pallas as pl
from jax.experimental.pallas import tpu as pltpu

D_MODEL = 2048
N_META = 16
W_A = 1024
W_B = 1024
LRU_CONV = 4
LRU_C = 8.0
CONF_KERNEL = 31
N_EXPERTS = 64
N_EXPERT_GROUPS = 8
EXPERTS_PER_GROUP = 8
TOPK_GROUPS = 4
TOP_K = 8
ROUTED_SCALE = 2.5
EPS = 1e-6

SUBLANES = 8
LANES = 128
TOKEN_BLOCK = 256
SEG_ROWS = SUBLANES
LOCAL_ROWS = 2560
SORT_CHUNK = 512
HOT_PIECE = 16
CHUNK_ROWS = (SEG_ROWS,)
CHUNK_MAX = (LOCAL_ROWS // SEG_ROWS,)
ISSUE_UNROLL = 8
WAIT_GROUP = 32
assert LOCAL_ROWS >= TOKEN_BLOCK * TOP_K + N_EXPERTS * (SEG_ROWS - 1) and LOCAL_ROWS % SORT_CHUNK == 0
LRU_HIST = SUBLANES
CONF_HIST = 32
VMEM_LIMIT = 56 * 1024 * 1024

F32 = jnp.float32
BF16 = jnp.bfloat16


def _params(semantics):
    return pltpu.CompilerParams(dimension_semantics=semantics, vmem_limit_bytes=VMEM_LIMIT)


def _rms_norm(x, g):
    return x * lax.rsqrt(jnp.mean(x * x, axis=-1, keepdims=True) + EPS) * g


def _part_starts(n_blocks):
    starts, s = [], 0
    for nb in n_blocks:
        starts.append(s)
        s += nb
    return starts


def _part_map(start, nb, grid_axis):
    def index_map(*ids):
        return (jnp.clip(ids[grid_axis] - start, 0, nb - 1), 0)

    return index_map


def _for_part(i, n_blocks, body):
    for p, (start, nb) in enumerate(zip(_part_starts(n_blocks), n_blocks)):
        pl.when((i >= start) & (i < start + nb))(functools.partial(body, p))


def _in_proj_kernel(*refs, n_blocks):
    x_refs = refs[:len(n_blocks)]
    meta_ref, g_ref, w_ref, o_ref, meta_o_ref, w16 = refs[len(n_blocks):]

    def project(x):
        return jnp.dot(_rms_norm(x, g_ref[...]).astype(BF16), w16[...], preferred_element_type=F32)

    @pl.when(pl.program_id(1) == 0)
    def _():
        w16[...] = w_ref[...].astype(BF16)
        meta_o_ref[...] = project(meta_ref[...])

    def body(p):
        o_ref[...] = project(x_refs[p][...])

    _for_part(pl.program_id(1), n_blocks, body)


def _in_proj(x_parts, meta, g, w, *, tm, tn):
    d, f = w.shape
    n_meta = meta.shape[0]
    n_blocks = tuple(x.shape[0] // tm for x in x_parts)
    x_specs = [
        pl.BlockSpec((tm, d), _part_map(start, nb, 1)) for start, nb in zip(_part_starts(n_blocks), n_blocks)
    ]
    return pl.pallas_call(
        functools.partial(_in_proj_kernel, n_blocks=n_blocks),
        grid=(f // tn, sum(n_blocks)),
        in_specs=x_specs + [
            pl.BlockSpec((n_meta, d), lambda j, i: (0, 0)),
            pl.BlockSpec((1, d), lambda j, i: (0, 0)),
            pl.BlockSpec((d, tn), lambda j, i: (0, j), pipeline_mode=pl.Buffered(1)),
        ],
        out_specs=[pl.BlockSpec((tm, tn), lambda j, i: (i, j)), pl.BlockSpec((n_meta, tn), lambda j, i: (0, j))],
        out_shape=[
            jax.ShapeDtypeStruct((sum(n_blocks) * tm, f), F32),
            jax.ShapeDtypeStruct((n_meta, f), F32),
        ],
        scratch_shapes=[pltpu.VMEM((d, tn), BF16)],
        compiler_params=_params(("arbitrary", "arbitrary")),
        name="in_proj",
    )(*x_parts, meta, g, w)


def _causal_conv(ext, w_ref, n_t, hist):
    taps = w_ref.shape[0]
    length = ext.shape[1]
    shifted = {0: ext}
    acc = None
    for j in range(taps):
        q, r = divmod(hist - (taps - 1) + j, SUBLANES)
        if r not in shifted:
            shifted[r] = pltpu.roll(ext, length - r, 1)
        term = w_ref[pl.ds(j, 1), :][None] * shifted[r][:, q * SUBLANES:q * SUBLANES + n_t, :]
        acc = term if acc is None else acc + term
    return acc


def _mixer_kernel(
    proj_ref, lc_ref, h0_ref, cc_ref,
    wca_ref, bca_ref, wa_ref, ba_ref, wi_ref, bi_ref, lam_ref,
    wcb_ref, bcb_ref, lng_ref, lnb_ref, nag_ref, nbg_ref,
    y_ref, lc_out, h_out, cc_out,
    ua_ext, glu_ext, hst, a_scr, u_scr, h_scr, wa_bd, wi_bd,
    *, n_seq, n_t,
):
    t = pl.program_id(1)
    rows = n_seq * n_t
    c = W_A

    @pl.when((pl.program_id(0) == 0) & (t == 0))
    def _():
        n_heads, head_dim, _ = wa_ref.shape
        for w_ref, dense in ((wa_ref, wa_bd), (wi_ref, wi_bd)):
            dense[...] = jnp.zeros(dense.shape, dense.dtype)
            for h in range(n_heads):
                block = pl.ds(h * head_dim, head_dim)
                dense[block, block] = w_ref[h].astype(dense.dtype)

    @pl.when(t == 0)
    def _():
        ua_ext[:, pl.ds(0, LRU_HIST), :] = jnp.zeros((n_seq, LRU_HIST, c), F32)
        ua_ext[:, pl.ds(LRU_HIST - (LRU_CONV - 1), LRU_CONV - 1), :] = lc_ref[...]
        glu_ext[:, pl.ds(0, CONF_HIST), :] = jnp.zeros((n_seq, CONF_HIST, c), F32)
        glu_ext[:, pl.ds(CONF_HIST - (CONF_KERNEL - 1), CONF_KERNEL - 1), :] = cc_ref[...]
        hst[...] = h0_ref[...]

    ua_ext[:, pl.ds(LRU_HIST, n_t), :] = proj_ref[:, pl.ds(0, c)].reshape(n_seq, n_t, c)
    c_a = _causal_conv(ua_ext[...], wca_ref, n_t, LRU_HIST)
    c_a = (c_a + bca_ref[...][None]).reshape(rows, c)
    c_a16 = c_a.astype(BF16)
    r = jax.nn.sigmoid(jnp.dot(c_a16, wa_bd[...], preferred_element_type=F32) + ba_ref[...])
    i = jax.nn.sigmoid(jnp.dot(c_a16, wi_bd[...], preferred_element_type=F32) + bi_ref[...])
    neg_lam = -lam_ref[...]
    softplus = jnp.maximum(neg_lam, 0.0) + jnp.log1p(jnp.exp(-jnp.abs(neg_lam)))
    log_a = (-LRU_C * r) * softplus
    a = jnp.exp(log_a)
    a_scr[...] = a
    u_scr[...] = jnp.sqrt(-jnp.tanh(log_a) * (a * a + 1.0)) * (i * c_a)

    row8 = lax.broadcasted_iota(jnp.int32, (SUBLANES, c), 0)
    groups = n_t // SUBLANES
    for s in range(n_seq):
        def scan_group(g, carry, s=s):
            off = pl.multiple_of(s * n_t + g * SUBLANES, SUBLANES)
            a = a_scr[pl.ds(off, SUBLANES), :]
            u = u_scr[pl.ds(off, SUBLANES), :]
            for d in (1, 2, 4):
                keep = row8 >= d
                a_sh = jnp.where(keep, pltpu.roll(a, d, 0), 1.0)
                u_sh = jnp.where(keep, pltpu.roll(u, d, 0), 0.0)
                u = a * u_sh + u
                a = a * a_sh
            h = a * carry + u
            h_scr[pl.ds(off, SUBLANES), :] = h
            return jnp.broadcast_to(h[SUBLANES - 1:SUBLANES, :], (SUBLANES, c))

        carry = lax.fori_loop(0, groups, scan_group, jnp.broadcast_to(hst[s], (SUBLANES, c)))
        hst[s] = carry[0:1, :]

    y_a = jax.nn.gelu(proj_ref[:, pl.ds(c, c)]) * h_scr[...]
    y_ref[:, pl.ds(0, c)] = _rms_norm(y_a, nag_ref[...]).astype(y_ref.dtype)

    glu = proj_ref[:, pl.ds(2 * c, c)] * jax.nn.sigmoid(proj_ref[:, pl.ds(3 * c, c)])
    glu_ext[:, pl.ds(CONF_HIST, n_t), :] = glu.reshape(n_seq, n_t, c)
    c_b = _causal_conv(glu_ext[...], wcb_ref, n_t, CONF_HIST)
    c_b = (c_b + bcb_ref[...][None]).reshape(rows, c)
    mu = jnp.mean(c_b, axis=-1, keepdims=True)
    cen = c_b - mu
    var = jnp.mean(cen * cen, axis=-1, keepdims=True)
    ln = cen * lax.rsqrt(var + EPS) * lng_ref[...] + lnb_ref[...]
    y_b = ln * jax.nn.sigmoid(ln)
    y_ref[:, pl.ds(c, c)] = _rms_norm(y_b, nbg_ref[...]).astype(y_ref.dtype)

    lc_out[...] = ua_ext[:, pl.ds(LRU_HIST + n_t - (LRU_CONV - 1), LRU_CONV - 1), :]
    cc_out[...] = glu_ext[:, pl.ds(CONF_HIST + n_t - (CONF_KERNEL - 1), CONF_KERNEL - 1), :]
    h_out[...] = hst[...]
    ua_ext[:, pl.ds(0, LRU_HIST), :] = ua_ext[:, pl.ds(n_t, LRU_HIST), :]
    glu_ext[:, pl.ds(0, CONF_HIST), :] = glu_ext[:, pl.ds(n_t, CONF_HIST), :]


def _mixer(proj, row_block0, lc, h0, cc, weights, *, n_seq, n_t, n_tiles):
    b = lc.shape[1]
    c = W_A
    rows = n_seq * n_t
    n_sb = b // n_seq

    def row_map(sb, t):
        return (row_block0 + sb * n_tiles + t, 0)

    def seq_map(sb, t):
        return (sb, 0, 0)

    def depth_seq_map(sb, t):
        return (0, sb, 0, 0)

    w_specs = [pl.BlockSpec(w.shape, (lambda sb, t, nd=w.ndim: (0,) * nd)) for w in weights]
    lc_spec = pl.BlockSpec((None, n_seq, LRU_CONV - 1, c), depth_seq_map)
    cc_spec = pl.BlockSpec((None, n_seq, CONF_KERNEL - 1, c), depth_seq_map)
    return pl.pallas_call(
        functools.partial(_mixer_kernel, n_seq=n_seq, n_t=n_t),
        grid=(n_sb, n_tiles),
        in_specs=[
            pl.BlockSpec((rows, 4 * c), row_map),
            lc_spec,
            pl.BlockSpec((n_seq, 1, c), seq_map),
            cc_spec,
        ] + w_specs,
        out_specs=[
            pl.BlockSpec((rows, 2 * c), lambda sb, t: (sb * n_tiles + t, 0)),
            lc_spec,
            pl.BlockSpec((n_seq, 1, c), seq_map),
            cc_spec,
        ],
        out_shape=[
            jax.ShapeDtypeStruct((b * n_tiles * n_t, 2 * c), BF16),
            jax.ShapeDtypeStruct((1, b, LRU_CONV - 1, c), F32),
            jax.ShapeDtypeStruct((b, 1, c), F32),
            jax.ShapeDtypeStruct((1, b, CONF_KERNEL - 1, c), F32),
        ],
        scratch_shapes=[
            pltpu.VMEM((n_seq, LRU_HIST + n_t, c), F32),
            pltpu.VMEM((n_seq, CONF_HIST + n_t, c), F32),
            pltpu.VMEM((n_seq, 1, c), F32),
            pltpu.VMEM((rows, c), F32),
            pltpu.VMEM((rows, c), F32),
            pltpu.VMEM((rows, c), F32),
            pltpu.VMEM((c, c), BF16),
            pltpu.VMEM((c, c), BF16),
        ],
        compiler_params=_params(("arbitrary", "arbitrary")),
        name="mixer",
    )(proj, lc, h0, cc, *weights)


def _out_proj_kernel(*refs, n_blocks):
    k = len(n_blocks)
    y_refs, x_refs = refs[:k], refs[k:2 * k]
    w_ref, g_ref, rw_ref, x1_ref, xn_ref, logit_ref, w16 = refs[2 * k:]

    @pl.when(pl.program_id(0) == 0)
    def _():
        w16[...] = w_ref[...].astype(BF16)

    def body(p):
        x1 = x_refs[p][...] + jnp.dot(y_refs[p][...], w16[...], preferred_element_type=F32)
        x1_ref[...] = x1
        xn = _rms_norm(x1, g_ref[...]).astype(BF16)
        xn_ref[...] = xn
        logit_ref[...] = jnp.dot(xn, rw_ref[...], preferred_element_type=F32)

    _for_part(pl.program_id(0), n_blocks, body)


def _out_proj(y_parts, x_parts, w_out, g2, router_w, *, tm):
    d = w_out.shape[0]
    e = router_w.shape[1]
    n_blocks = tuple(x.shape[0] // tm for x in x_parts)
    part_specs = [
        pl.BlockSpec((tm, d), _part_map(start, nb, 0)) for start, nb in zip(_part_starts(n_blocks), n_blocks)
    ]
    n = sum(n_blocks) * tm
    const = lambda i: (0, 0)
    row = lambda i: (i, 0)
    return pl.pallas_call(
        functools.partial(_out_proj_kernel, n_blocks=n_blocks),
        grid=(sum(n_blocks),),
        in_specs=part_specs + part_specs + [
            pl.BlockSpec((d, d), const, pipeline_mode=pl.Buffered(1)),
            pl.BlockSpec((1, d), const),
            pl.BlockSpec((d, e), const),
        ],
        out_specs=[pl.BlockSpec((tm, d), row), pl.BlockSpec((tm, d), row), pl.BlockSpec((tm, e), row)],
        out_shape=[
            jax.ShapeDtypeStruct((n, d), F32),
            jax.ShapeDtypeStruct((n, d), BF16),
            jax.ShapeDtypeStruct((n, e), F32),
        ],
        scratch_shapes=[pltpu.VMEM((d, d), BF16)],
        compiler_params=_params(("arbitrary",)),
        name="out_proj",
    )(*y_parts, *x_parts, w_out, g2, router_w)


def _first_argmax(work, index, sentinel):
    m = jnp.max(work, axis=0, keepdims=True)
    return jnp.min(jnp.where(work == m, index, sentinel), axis=0, keepdims=True)


def _route_kernel(logit_ref, bias_ref, eidx_ref, gate_ref, rank_ref, count_ref):
    n_tok = logit_ref.shape[1]

    scores = jax.nn.sigmoid(logit_ref[...])
    biased = scores + bias_ref[...]
    grouped = biased.reshape(N_EXPERT_GROUPS, EXPERTS_PER_GROUP, n_tok)
    in_group = lax.broadcasted_iota(jnp.int32, grouped.shape, 1)
    top1 = jnp.max(grouped, axis=1, keepdims=True)
    first1 = jnp.min(jnp.where(grouped == top1, in_group, EXPERTS_PER_GROUP), axis=1, keepdims=True)
    top2 = jnp.max(jnp.where(in_group == first1, -jnp.inf, grouped), axis=1, keepdims=True)
    group_scores = (top1 + top2).reshape(N_EXPERT_GROUPS, n_tok)

    group_id = lax.broadcasted_iota(jnp.int32, group_scores.shape, 0)
    group_on = jnp.zeros(group_scores.shape, F32)
    work = group_scores
    for _ in range(TOPK_GROUPS):
        pick = group_id == _first_argmax(work, group_id, N_EXPERT_GROUPS)
        group_on = jnp.where(pick, 1.0, group_on)
        work = jnp.where(pick, -jnp.inf, work)

    masked = jnp.where(group_on.reshape(N_EXPERT_GROUPS, 1, n_tok) > 0.0, grouped, -jnp.inf)
    work = masked.reshape(N_EXPERTS, n_tok)
    expert_id = lax.broadcasted_iota(jnp.int32, work.shape, 0)
    ids, sel = [], []
    chosen = jnp.zeros(work.shape, F32)
    for _ in range(TOP_K):
        first = _first_argmax(work, expert_id, N_EXPERTS)
        pick = expert_id == first
        ids.append(first)
        sel.append(jnp.sum(jnp.where(pick, scores, 0.0), axis=0, keepdims=True))
        chosen = jnp.where(pick, 1.0, chosen)
        work = jnp.where(pick, -jnp.inf, work)
    sel = jnp.concatenate(sel, axis=0)
    eidx_ref[...] = jnp.concatenate(ids, axis=0)
    gate_ref[...] = sel / (jnp.sum(sel, axis=0, keepdims=True) + 1e-20) * ROUTED_SCALE

    earlier = lax.broadcasted_iota(jnp.int32, (n_tok, n_tok), 0) < lax.broadcasted_iota(jnp.int32, (n_tok, n_tok), 1)
    before = jnp.dot(chosen.astype(BF16), earlier.astype(BF16), preferred_element_type=F32)
    rank_ref[...] = jnp.concatenate(
        [jnp.sum(jnp.where(expert_id == ids[k], before, 0.0), axis=0, keepdims=True) for k in range(TOP_K)], axis=0
    ).astype(jnp.int32)
    count_ref[0] = jnp.sum(chosen, axis=1, keepdims=True).astype(jnp.int32)


def _route(logits_t, bias, *, tl):
    e, n = logits_t.shape
    pick_spec = pl.BlockSpec((TOP_K, tl), lambda i: (0, i))
    return pl.pallas_call(
        _route_kernel,
        grid=(n // tl,),
        in_specs=[pl.BlockSpec((e, tl), lambda i: (0, i)), pl.BlockSpec((e, 1), lambda i: (0, 0))],
        out_specs=[pick_spec, pick_spec, pick_spec, pl.BlockSpec((1, e, 1), lambda i: (i, 0, 0))],
        out_shape=[
            jax.ShapeDtypeStruct((TOP_K, n), jnp.int32),
            jax.ShapeDtypeStruct((TOP_K, n), F32),
            jax.ShapeDtypeStruct((TOP_K, n), jnp.int32),
            jax.ShapeDtypeStruct((n // tl, e, 1), jnp.int32),
        ],
        compiler_params=_params(("arbitrary",)),
        name="route",
    )(logits_t, bias)


def _gate_lanes(gate):
    g1 = gate.astype(BF16).astype(F32)
    rest = gate - g1
    g2 = rest.astype(BF16).astype(F32)
    g3 = (rest - g2).astype(BF16).astype(F32)
    lane = lax.broadcasted_iota(jnp.int32, (gate.shape[0], LANES), 1)
    return jnp.where(lane == 0, g1, jnp.where(lane == 1, g2, jnp.where(lane == 2, g3, 0.0))).astype(BF16)


def _tab_offsets():
    offsets, col = [], len(CHUNK_ROWS)
    for cap in CHUNK_MAX:
        offsets.append((col, col + cap))
        col += 2 * cap
    return offsets, col


def _segment_copies(tab_ref, block, make_copy):
    offsets, width = _tab_offsets()
    base = block * width
    counts = []
    for k, (rows, (local_col, global_col)) in enumerate(zip(CHUNK_ROWS, offsets)):
        n_chunks = tab_ref[base + k]

        def start_one(j, rows=rows, local_col=local_col, global_col=global_col):
            make_copy(
                pl.multiple_of(tab_ref[base + local_col + j], SEG_ROWS),
                pl.multiple_of(tab_ref[base + global_col + j], SEG_ROWS),
                rows,
            ).start()

        def per_group(g, carry):
            for u in range(ISSUE_UNROLL):
                start_one(g * ISSUE_UNROLL + u)
            return carry

        def per_chunk(j, carry):
            start_one(j)
            return carry

        n_groups = n_chunks // ISSUE_UNROLL
        lax.fori_loop(0, n_groups, per_group, 0)
        lax.fori_loop(n_groups * ISSUE_UNROLL, n_chunks, per_chunk, 0)
        counts.append(n_chunks)
    return counts


def _wait_copies(counts, make_copy):
    for rows, count in zip(CHUNK_ROWS, counts):
        def wait_group(c, carry, rows=rows):
            make_copy(0, 0, rows * WAIT_GROUP).wait()
            return carry

        def wait_one(c, carry, rows=rows):
            make_copy(0, 0, rows).wait()
            return carry

        n_groups = count // WAIT_GROUP
        lax.fori_loop(0, n_groups, wait_group, 0)
        lax.fori_loop(n_groups * WAIT_GROUP, count, wait_one, 0)


def _dispatch_kernel(
    tab_ref, gap_ref, pos_ref, gate_ref, x_ref, xs_hbm, sorted_scr, zero_scr, pending, fill_counts, sem, zero_sem
):
    step = pl.program_id(0)
    slot = step % 2
    d = x_ref.shape[1]
    n_tok = x_ref.shape[0]
    tile_rows = zero_scr.shape[0]

    def zero_chunk(row):
        return pltpu.make_async_copy(
            zero_scr.at[pl.ds(0, SEG_ROWS), :],
            xs_hbm.at[pl.ds(pl.multiple_of(row, SEG_ROWS), SEG_ROWS), :],
            zero_sem,
        )

    def zero_tile(row):
        return pltpu.make_async_copy(
            zero_scr, xs_hbm.at[pl.ds(pl.multiple_of(row, tile_rows), tile_rows), :], zero_sem
        )

    @pl.when(step == 0)
    def _():
        zero_scr[...] = jnp.zeros(zero_scr.shape, BF16)
        n_gaps = N_EXPERTS + 1

        def per_gap(g, totals):
            chunk_start, chunks = gap_ref[g], gap_ref[n_gaps + g]
            tile_start, tiles = gap_ref[2 * n_gaps + g], gap_ref[3 * n_gaps + g]

            def per_chunk(c, carry):
                zero_chunk(chunk_start + c * SEG_ROWS).start()
                return carry

            def per_tile(c, carry):
                zero_tile(tile_start + c * tile_rows).start()
                return carry

            lax.fori_loop(0, chunks, per_chunk, 0)
            lax.fori_loop(0, tiles, per_tile, 0)
            return totals[0] + chunks, totals[1] + tiles

        n_fill, n_tail = lax.fori_loop(0, n_gaps, per_gap, (0, 0))
        fill_counts[0] = n_fill
        fill_counts[1] = n_tail

    pos = pos_ref[...]
    gate = gate_ref[...]
    x = x_ref[...]
    piece_id = lax.broadcasted_iota(jnp.int32, (HOT_PIECE, n_tok), 0)
    pos_k = [jnp.broadcast_to(pos[k:k + 1, :], (HOT_PIECE, n_tok)) for k in range(TOP_K)]
    gate_k = [jnp.broadcast_to(gate[k:k + 1, :], (HOT_PIECE, n_tok)) for k in range(TOP_K)]
    for c in range(LOCAL_ROWS // SORT_CHUNK):
        one_hot, row_gate = [], []
        for p in range(SORT_CHUNK // HOT_PIECE):
            row_id = piece_id + (c * SORT_CHUNK + p * HOT_PIECE)
            hot = jnp.zeros((HOT_PIECE, n_tok), F32)
            gate_hit = jnp.zeros((HOT_PIECE, n_tok), F32)
            for k in range(TOP_K):
                hit = row_id == pos_k[k]
                hot = jnp.where(hit, 1.0, hot)
                gate_hit = jnp.where(hit, gate_k[k], gate_hit)
            one_hot.append(hot.astype(BF16))
            row_gate.append(jnp.sum(gate_hit, axis=1, keepdims=True))
        rows = jnp.dot(jnp.concatenate(one_hot, axis=0), x, preferred_element_type=F32)
        out_rows = pl.ds(c * SORT_CHUNK, SORT_CHUNK)
        sorted_scr[slot, out_rows, pl.ds(0, d)] = rows.astype(BF16)
        sorted_scr[slot, out_rows, pl.ds(d, LANES)] = _gate_lanes(jnp.concatenate(row_gate, axis=0))

    def copy_chunk_from(src_slot):
        def copy_chunk(local_row, global_row, rows):
            return pltpu.make_async_copy(
                sorted_scr.at[src_slot, pl.ds(local_row, rows), :], xs_hbm.at[pl.ds(global_row, rows), :], sem
            )

        return copy_chunk

    copy_chunk = copy_chunk_from(0)

    @pl.when(step > 0)
    def _():
        _wait_copies([pending[k] for k in range(len(CHUNK_ROWS))], copy_chunk)

    for s in range(2):
        @pl.when(slot == s)
        def _(s=s):
            for k, count in enumerate(_segment_copies(tab_ref, step, copy_chunk_from(s))):
                pending[k] = count

    @pl.when(step == pl.num_programs(0) - 1)
    def _():
        def wait_tile(c, carry):
            zero_tile(0).wait()
            return carry

        def wait_chunk(c, carry):
            zero_chunk(0).wait()
            return carry

        _wait_copies([pending[k] for k in range(len(CHUNK_ROWS))], copy_chunk)
        lax.fori_loop(0, fill_counts[0], wait_chunk, 0)
        lax.fori_loop(0, fill_counts[1], wait_tile, 0)


def _dispatch(tab, gaps, pos_t, gate_t, xn, n_sorted, *, tb, zero_rows):
    n, d = xn.shape
    width = d + LANES
    pick_spec = pl.BlockSpec((TOP_K, tb), lambda i, tab, gaps: (0, i))
    return pl.pallas_call(
        _dispatch_kernel,
        grid_spec=pltpu.PrefetchScalarGridSpec(
            num_scalar_prefetch=2,
            grid=(n // tb,),
            in_specs=[pick_spec, pick_spec, pl.BlockSpec((tb, d), lambda i, tab, gaps: (i, 0))],
            out_specs=pl.BlockSpec(memory_space=pl.ANY),
            scratch_shapes=[
                pltpu.VMEM((2, LOCAL_ROWS, width), BF16),
                pltpu.VMEM((zero_rows, width), BF16),
                pltpu.SMEM((len(CHUNK_ROWS),), jnp.int32),
                pltpu.SMEM((2,), jnp.int32),
                pltpu.SemaphoreType.DMA,
                pltpu.SemaphoreType.DMA,
            ],
        ),
        out_shape=jax.ShapeDtypeStruct((n_sorted, width), BF16),
        compiler_params=_params(("arbitrary",)),
        name="dispatch",
    )(tab, gaps, pos_t, gate_t, xn)


def _experts_kernel(
    expert_ref, used_ref, fresh_ref, next_ref, src_ref, xs_ref, wg_hbm, wu_hbm, wd_hbm, ys_ref,
    wg32, wu32, wd32, wg16, wu16, wd16, sems,
):
    it = pl.program_id(0)
    d = ys_ref.shape[1]

    def weight_copies(e):
        return (
            pltpu.make_async_copy(wg_hbm.at[e], wg32, sems.at[0]),
            pltpu.make_async_copy(wu_hbm.at[e], wu32, sems.at[1]),
            pltpu.make_async_copy(wd_hbm.at[e], wd32, sems.at[2]),
        )

    @pl.when(it == 0)
    def _():
        for copy in weight_copies(expert_ref[0]):
            copy.start()

    @pl.when(used_ref[it] == 0)
    def _():
        ys_ref[...] = jnp.zeros(ys_ref.shape, ys_ref.dtype)

    @pl.when((used_ref[it] > 0) & (fresh_ref[it] == 1))
    def _():
        for copy in weight_copies(expert_ref[it]):
            copy.wait()
        wg16[...] = wg32[...].astype(BF16)
        wu16[...] = wu32[...].astype(BF16)
        wd16[...] = wd32[...].astype(BF16)

        @pl.when(next_ref[it] >= 0)
        def _():
            for copy in weight_copies(next_ref[it]):
                copy.start()

    tm = ys_ref.shape[0]
    for n_sub in range(1, tm // EXPERT_SUB + 1):
        rows = n_sub * EXPERT_SUB

        @pl.when(used_ref[it] == n_sub)
        def _(rows=rows):
            x = xs_ref[pl.ds(0, rows), pl.ds(0, d)]
            gate = jnp.sum(xs_ref[pl.ds(0, rows), pl.ds(d, LANES)].astype(F32), axis=1, keepdims=True)
            g = jnp.dot(x, wg16[...], preferred_element_type=F32)
            u = jnp.dot(x, wu16[...], preferred_element_type=F32)
            h = g * jax.nn.sigmoid(g) * u * gate
            y = jnp.dot(h.astype(BF16), wd16[...], preferred_element_type=F32)
            ys_ref[pl.ds(0, rows), :] = y.astype(ys_ref.dtype)
            if rows < tm:
                ys_ref[pl.ds(rows, tm - rows), :] = jnp.zeros((tm - rows, d), ys_ref.dtype)


def _experts(tiles, xs, w_gate, w_up, w_down, *, tm):
    p = xs.shape[0]
    d, f = w_gate.shape[1:]
    row_map = lambda i, expert, used, fresh, nxt, src: (i, 0)
    src_map = lambda i, expert, used, fresh, nxt, src: (src[i], 0)
    hbm = pl.BlockSpec(memory_space=pl.ANY)
    return pl.pallas_call(
        _experts_kernel,
        grid_spec=pltpu.PrefetchScalarGridSpec(
            num_scalar_prefetch=5,
            grid=(p // tm,),
            in_specs=[pl.BlockSpec((tm, xs.shape[1]), src_map), hbm, hbm, hbm],
            out_specs=pl.BlockSpec((tm, d), row_map),
            scratch_shapes=[
                pltpu.VMEM((d, f), F32), pltpu.VMEM((d, f), F32), pltpu.VMEM((f, d), F32),
                pltpu.VMEM((d, f), BF16), pltpu.VMEM((d, f), BF16), pltpu.VMEM((f, d), BF16),
                pltpu.SemaphoreType.DMA((3,)),
            ],
        ),
        out_shape=jax.ShapeDtypeStruct((p, d), BF16),
        compiler_params=_params(("arbitrary",)),
        name="experts",
    )(*tiles, xs, w_gate, w_up, w_down)


def _combine_kernel(tab_ref, pos_ref, x1_ref, xn_ref, sg_ref, su_ref, sd_ref, fg_ref, ys_hbm, *rest, n_blocks):
    o_refs = rest[:len(n_blocks)]
    local_scr, pending, sems = rest[len(n_blocks):]
    n_tok = x1_ref.shape[0]
    step = pl.program_id(0)
    slot = step % 2

    n_sizes = len(CHUNK_ROWS)

    def copy_chunk_into(dst_slot):
        def copy_chunk(local_row, global_row, rows):
            return pltpu.make_async_copy(
                ys_hbm.at[pl.ds(global_row, rows), :],
                local_scr.at[dst_slot, pl.ds(local_row, rows), :],
                sems.at[dst_slot],
            )

        return copy_chunk

    def fetch(block, dst_slot):
        for k, count in enumerate(_segment_copies(tab_ref, block, copy_chunk_into(dst_slot))):
            pending[dst_slot * n_sizes + k] = count

    @pl.when(step == 0)
    def _():
        local_scr[...] = jnp.zeros(local_scr.shape, local_scr.dtype)
        fetch(0, 0)

    for s in range(2):
        @pl.when((step + 1 < pl.num_programs(0)) & (slot == s))
        def _(s=s):
            fetch(step + 1, 1 - s)

    xn = xn_ref[...]
    sg = jnp.dot(xn, sg_ref[...], preferred_element_type=F32)
    su = jnp.dot(xn, su_ref[...], preferred_element_type=F32)
    acc = x1_ref[...] + jnp.dot(
        (sg * jax.nn.sigmoid(sg) * su).astype(BF16), sd_ref[...], preferred_element_type=F32
    )

    piece_id = lax.broadcasted_iota(jnp.int32, (HOT_PIECE, LANES), 1)
    one_hot = []
    for r in range(n_tok // HOT_PIECE):
        pos = pos_ref[pl.ds(r * HOT_PIECE, HOT_PIECE), :]
        pos_k = [jnp.broadcast_to(pos[:, k:k + 1], (HOT_PIECE, LANES)) for k in range(TOP_K)]
        cols = []
        for c in range(LOCAL_ROWS // LANES):
            col_id = piece_id + c * LANES
            hot = jnp.zeros((HOT_PIECE, LANES), F32)
            for k in range(TOP_K):
                hot = jnp.where(col_id == pos_k[k], 1.0, hot)
            cols.append(hot.astype(BF16))
        one_hot.append(jnp.concatenate(cols, axis=1))
    one_hot = jnp.concatenate(one_hot, axis=0)

    _wait_copies([pending[slot * n_sizes + k] for k in range(n_sizes)], copy_chunk_into(slot))
    routed = jnp.dot(one_hot, local_scr[slot], preferred_element_type=F32)
    out = _rms_norm(acc + routed, fg_ref[...])

    def body(p):
        o_refs[p][...] = out

    _for_part(pl.program_id(0), n_blocks, body)


def _combine(tab, pos, x1, xn, sh_gate, sh_up, sh_down, final_g, ys, part_rows, *, tb):
    n, d = x1.shape
    f = sh_gate.shape[1]
    n_blocks = tuple(r // tb for r in part_rows)
    const = lambda i, tab: (0, 0)
    row = lambda i, tab: (i, 0)
    return pl.pallas_call(
        functools.partial(_combine_kernel, n_blocks=n_blocks),
        grid_spec=pltpu.PrefetchScalarGridSpec(
            num_scalar_prefetch=1,
            grid=(n // tb,),
            in_specs=[
                pl.BlockSpec((tb, TOP_K), row),
                pl.BlockSpec((tb, d), row),
                pl.BlockSpec((tb, d), row),
                pl.BlockSpec((d, f), const, pipeline_mode=pl.Buffered(1)),
                pl.BlockSpec((d, f), const, pipeline_mode=pl.Buffered(1)),
                pl.BlockSpec((f, d), const, pipeline_mode=pl.Buffered(1)),
                pl.BlockSpec((1, d), const),
                pl.BlockSpec(memory_space=pl.ANY),
            ],
            out_specs=[
                pl.BlockSpec((tb, d), _part_map(start, nb, 0))
                for start, nb in zip(_part_starts(n_blocks), n_blocks)
            ],
            scratch_shapes=[
                pltpu.VMEM((2, LOCAL_ROWS, d), BF16),
                pltpu.SMEM((2 * len(CHUNK_ROWS),), jnp.int32),
                pltpu.SemaphoreType.DMA((2,)),
            ],
        ),
        out_shape=[jax.ShapeDtypeStruct((r, d), F32) for r in part_rows],
        compiler_params=_params(("arbitrary",)),
        name="combine",
    )(tab, pos, x1, xn, sh_gate, sh_up, sh_down, final_g, ys)


def _segment_tables(block_counts, n_rows, *, tm):
    padded = (block_counts + SEG_ROWS - 1) // SEG_ROWS * SEG_ROWS
    local_start = jnp.cumsum(padded, axis=1) - padded
    per_expert = jnp.sum(padded, axis=0)
    region = (per_expert + tm - 1) // tm * tm
    region_start = jnp.concatenate([jnp.zeros((1,), jnp.int32), jnp.cumsum(region).astype(jnp.int32)])
    global_start = region_start[None, :N_EXPERTS] + jnp.cumsum(padded, axis=0) - padded
    counts, lists = [], []
    done = jnp.zeros_like(padded)
    for rows, cap in zip(CHUNK_ROWS, CHUNK_MAX):
        per_seg = (padded - done) // rows
        seg_end = jnp.cumsum(per_seg, axis=1)
        j = jnp.arange(cap, dtype=jnp.int32)
        seg_of_chunk = jnp.sum((seg_end[:, None, :] <= j[None, :, None]).astype(jnp.int32), axis=2)
        in_seg = seg_of_chunk[:, :, None] == jnp.arange(N_EXPERTS, dtype=jnp.int32)
        first = done - (seg_end - per_seg) * rows
        for start in (local_start, global_start):
            lists.append(jnp.sum(jnp.where(in_seg, (start + first)[:, None, :], 0), axis=2) + j[None, :] * rows)
        counts.append(seg_end[:, N_EXPERTS - 1:])
        done = done + per_seg * rows
    tab = jnp.concatenate(counts + lists, axis=1).astype(jnp.int32).reshape(-1)

    region_end = region_start[:N_EXPERTS] + per_expert
    gap_start = jnp.concatenate([region_end, region_start[N_EXPERTS:]])
    gap_end = jnp.concatenate([region_start[1:], jnp.full((1,), n_rows, jnp.int32)])
    sub_start = jnp.minimum((gap_start + EXPERT_SUB - 1) // EXPERT_SUB * EXPERT_SUB, gap_end)
    gaps = jnp.concatenate(
        [gap_start, (sub_start - gap_start) // SEG_ROWS, sub_start, (gap_end - sub_start) // EXPERT_SUB]
    ).astype(jnp.int32)

    tile_start = jnp.arange(n_rows // tm, dtype=jnp.int32) * tm
    expert = jnp.clip(
        jnp.sum((region_start[None, :] <= tile_start[:, None]).astype(jnp.int32), axis=1) - 1, 0, N_EXPERTS - 1
    )
    tile_end = jnp.sum(jnp.where(expert[:, None] == jnp.arange(N_EXPERTS), region_end[None, :], 0), axis=1)
    used = jnp.clip((tile_end - tile_start + EXPERT_SUB - 1) // EXPERT_SUB, 0, tm // EXPERT_SUB).astype(jnp.int32)
    fresh = jnp.concatenate([jnp.ones((1,), jnp.int32), (expert[1:] != expert[:-1]).astype(jnp.int32)])
    experts = jnp.arange(N_EXPERTS, dtype=jnp.int32)
    later = (experts[None, :] > expert[:, None]) & (per_expert[None, :] > 0)
    nxt = jnp.min(jnp.where(later, experts[None, :], N_EXPERTS), axis=1)
    nxt = jnp.where(nxt < N_EXPERTS, nxt, -1).astype(jnp.int32)
    tile_id = jnp.arange(n_rows // tm, dtype=jnp.int32)
    src = jnp.minimum(tile_id, jnp.max(jnp.where(used > 0, tile_id, 0)))
    return tab, gaps, (expert, used, fresh, nxt, src), local_start.astype(jnp.int32)


ROW_TM = 512
OUT_PROJ_TM = 256
IN_PROJ_TN = 2048
PROMPT_T = 256
SAMPLE_SEQS = 16
EXPERT_TM = 512
EXPERT_SUB = 256


def kernel(x_prompt, x_sample, state_lru_conv, state_lru_h, state_conf_conv, meta_tokens, norm1_g, w_in, lru_conv_w, lru_conv_b, lru_wa, lru_ba, lru_wi, lru_bi, lru_lambda, conf_conv_w, conf_conv_b, conf_ln_g, conf_ln_b, out_norm_a, out_norm_b, w_out, norm2_g, router_w, router_bias, exp_w_gate, exp_w_up, exp_w_down, sh_w_gate, sh_w_up, sh_w_down, final_norm_g):
    b_p, seq, d = x_prompt.shape
    b_s, t_s, _ = x_sample.shape
    n_p = b_p * seq
    n_s = b_s * t_s
    n = n_p + n_s
    x_parts = (x_prompt.reshape(n_p, d), x_sample.reshape(n_s, d))

    row = lambda v: v.reshape(1, -1)
    mixer_w = (
        lru_conv_w[0], row(lru_conv_b[0]),
        lru_wa[0], row(lru_ba[0]),
        lru_wi[0], row(lru_bi[0]),
        row(lru_lambda[0]),
        conf_conv_w[0], row(conf_conv_b[0]), row(conf_ln_g[0]), row(conf_ln_b[0]),
        row(out_norm_a[0]), row(out_norm_b[0]),
    )

    proj, proj_m = _in_proj(x_parts, meta_tokens, row(norm1_g[0]), w_in[0], tm=ROW_TM, tn=IN_PROJ_TN)
    _, m_lc, m_h, m_cc = _mixer(
        proj_m, 0,
        jnp.zeros((1, 1, LRU_CONV - 1, W_A), F32), jnp.zeros((1, 1, W_A), F32),
        jnp.zeros((1, 1, CONF_KERNEL - 1, W_B), F32),
        mixer_w, n_seq=1, n_t=N_META, n_tiles=1,
    )

    y_p, p_lc, p_h, p_cc = _mixer(
        proj, 0,
        jnp.broadcast_to(m_lc, (1, b_p) + m_lc.shape[2:]), jnp.broadcast_to(m_h, (b_p,) + m_h.shape[1:]),
        jnp.broadcast_to(m_cc, (1, b_p) + m_cc.shape[2:]),
        mixer_w, n_seq=1, n_t=PROMPT_T, n_tiles=seq // PROMPT_T,
    )
    y_s, s_lc, s_h, s_cc = _mixer(
        proj, n_p // (SAMPLE_SEQS * t_s),
        state_lru_conv, state_lru_h[0].reshape(b_s, 1, W_A), state_conf_conv,
        mixer_w, n_seq=SAMPLE_SEQS, n_t=t_s, n_tiles=1,
    )

    x1, xn2, logits = _out_proj(
        (y_p, y_s), x_parts, w_out[0], row(norm2_g[0]), router_w[0].astype(BF16), tm=OUT_PROJ_TM
    )

    eidx_t, gate_t, rank_t, block_counts = _route(logits.T, router_bias[0].reshape(N_EXPERTS, 1), tl=TOKEN_BLOCK)
    n_blocks = n // TOKEN_BLOCK
    n_sorted = n_blocks * (TOKEN_BLOCK * TOP_K + N_EXPERTS * (SEG_ROWS - 1)) + N_EXPERTS * (EXPERT_TM - 1)
    n_sorted = -(-n_sorted // EXPERT_TM) * EXPERT_TM
    tab, gaps, tiles, local_start = _segment_tables(block_counts[:, :, 0], n_sorted, tm=EXPERT_TM)
    expert_one_hot = eidx_t[:, :, None] == jnp.arange(N_EXPERTS, dtype=jnp.int32)
    token_start = jnp.repeat(local_start, TOKEN_BLOCK, axis=0)
    pos_t = jnp.sum(jnp.where(expert_one_hot, token_start[None], 0), axis=2) + rank_t

    xs = _dispatch(tab, gaps, pos_t, gate_t, xn2, n_sorted, tb=TOKEN_BLOCK, zero_rows=EXPERT_SUB)
    ys = _experts(tiles, xs, exp_w_gate[0], exp_w_up[0], exp_w_down[0], tm=EXPERT_TM)
    out_p, out_s = _combine(
        tab, pos_t.T, x1, xn2,
        sh_w_gate[0].astype(BF16), sh_w_up[0].astype(BF16), sh_w_down[0].astype(BF16),
        row(final_norm_g), ys, (n_p, n_s), tb=TOKEN_BLOCK,
    )

    return (
        out_p.reshape(b_p, seq, d), out_s.reshape(b_s, t_s, d),
        p_lc, p_h.reshape(1, b_p, W_A), p_cc,
        s_lc, s_h.reshape(1, b_s, W_A), s_cc,
    )
```

```python
import functools

import jax
import jax.numpy as jnp
from jax import lax
from jax.experimental import pallas as pl
from jax.experimental.pallas import tpu as pltpu

D_MODEL = 2048
N_META = 16
W_A = 1024
W_B = 1024
LRU_CONV = 4
LRU_C = 8.0
CONF_KERNEL = 31
N_EXPERTS = 64
N_EXPERT_GROUPS = 8
EXPERTS_PER_GROUP = 8
TOPK_GROUPS = 4
TOP_K = 8
ROUTED_SCALE = 2.5
EPS = 1e-6

SUBLANES = 8
LANES = 128
TOKEN_BLOCK = 256
SEG_ROWS = SUBLANES
LOCAL_ROWS = 2560
SORT_CHUNK = 512
HOT_PIECE = 16
CHUNK_ROWS = (SEG_ROWS,)
CHUNK_MAX = (LOCAL_ROWS // SEG_ROWS,)
ISSUE_UNROLL = 8
WAIT_GROUP = 32
assert LOCAL_ROWS >= TOKEN_BLOCK * TOP_K + N_EXPERTS * (SEG_ROWS - 1) and LOCAL_ROWS % SORT_CHUNK == 0
LRU_HIST = SUBLANES
CONF_HIST = 32
VMEM_LIMIT = 56 * 1024 * 1024

F32 = jnp.float32
BF16 = jnp.bfloat16


def _params(semantics):
    return pltpu.CompilerParams(dimension_semantics=semantics, vmem_limit_bytes=VMEM_LIMIT)


def _rms_norm(x, g):
    return x * lax.rsqrt(jnp.mean(x * x, axis=-1, keepdims=True) + EPS) * g


def _part_starts(n_blocks):
    starts, s = [], 0
    for nb in n_blocks:
        starts.append(s)
        s += nb
    return starts


def _part_map(start, nb, grid_axis):
    def index_map(*ids):
        return (jnp.clip(ids[grid_axis] - start, 0, nb - 1), 0)

    return index_map


def _for_part(i, n_blocks, body):
    for p, (start, nb) in enumerate(zip(_part_starts(n_blocks), n_blocks)):
        pl.when((i >= start) & (i < start + nb))(functools.partial(body, p))


def _in_proj_kernel(*refs, n_blocks):
    x_refs = refs[:len(n_blocks)]
    meta_ref, g_ref, w_ref, o_ref, meta_o_ref, w16 = refs[len(n_blocks):]

    def project(x):
        return jnp.dot(_rms_norm(x, g_ref[...]).astype(BF16), w16[...], preferred_element_type=F32)

    @pl.when(pl.program_id(1) == 0)
    def _():
        w16[...] = w_ref[...].astype(BF16)
        meta_o_ref[...] = project(meta_ref[...])

    def body(p):
        o_ref[...] = project(x_refs[p][...])

    _for_part(pl.program_id(1), n_blocks, body)


def _in_proj(x_parts, meta, g, w, *, tm, tn):
    d, f = w.shape
    n_meta = meta.shape[0]
    n_blocks = tuple(x.shape[0] // tm for x in x_parts)
    x_specs = [
        pl.BlockSpec((tm, d), _part_map(start, nb, 1)) for start, nb in zip(_part_starts(n_blocks), n_blocks)
    ]
    return pl.pallas_call(
        functools.partial(_in_proj_kernel, n_blocks=n_blocks),
        grid=(f // tn, sum(n_blocks)),
        in_specs=x_specs + [
            pl.BlockSpec((n_meta, d), lambda j, i: (0, 0)),
            pl.BlockSpec((1, d), lambda j, i: (0, 0)),
            pl.BlockSpec((d, tn), lambda j, i: (0, j), pipeline_mode=pl.Buffered(1)),
        ],
        out_specs=[pl.BlockSpec((tm, tn), lambda j, i: (i, j)), pl.BlockSpec((n_meta, tn), lambda j, i: (0, j))],
        out_shape=[
            jax.ShapeDtypeStruct((sum(n_blocks) * tm, f), F32),
            jax.ShapeDtypeStruct((n_meta, f), F32),
        ],
        scratch_shapes=[pltpu.VMEM((d, tn), BF16)],
        compiler_params=_params(("arbitrary", "arbitrary")),
        name="in_proj",
    )(*x_parts, meta, g, w)


def _causal_conv(ext, w_ref, n_t, hist):
    taps = w_ref.shape[0]
    length = ext.shape[1]
    shifted = {0: ext}
    acc = None
    for j in range(taps):
        q, r = divmod(hist - (taps - 1) + j, SUBLANES)
        if r not in shifted:
            shifted[r] = pltpu.roll(ext, length - r, 1)
        term = w_ref[pl.ds(j, 1), :][None] * shifted[r][:, q * SUBLANES:q * SUBLANES + n_t, :]
        acc = term if acc is None else acc + term
    return acc


def _mixer_kernel(
    proj_ref, lc_ref, h0_ref, cc_ref,
    wca_ref, bca_ref, wa_ref, ba_ref, wi_ref, bi_ref, lam_ref,
    wcb_ref, bcb_ref, lng_ref, lnb_ref, nag_ref, nbg_ref,
    y_ref, lc_out, h_out, cc_out,
    ua_ext, glu_ext, hst, a_scr, u_scr, h_scr, wa_bd, wi_bd,
    *, n_seq, n_t,
):
    t = pl.program_id(1)
    rows = n_seq * n_t
    c = W_A

    @pl.when((pl.program_id(0) == 0) & (t == 0))
    def _():
        n_heads, head_dim, _ = wa_ref.shape
        for w_ref, dense in ((wa_ref, wa_bd), (wi_ref, wi_bd)):
            dense[...] = jnp.zeros(dense.shape, dense.dtype)
            for h in range(n_heads):
                block = pl.ds(h * head_dim, head_dim)
                dense[block, block] = w_ref[h].astype(dense.dtype)

    @pl.when(t == 0)
    def _():
        ua_ext[:, pl.ds(0, LRU_HIST), :] = jnp.zeros((n_seq, LRU_HIST, c), F32)
        ua_ext[:, pl.ds(LRU_HIST - (LRU_CONV - 1), LRU_CONV - 1), :] = lc_ref[...]
        glu_ext[:, pl.ds(0, CONF_HIST), :] = jnp.zeros((n_seq, CONF_HIST, c), F32)
        glu_ext[:, pl.ds(CONF_HIST - (CONF_KERNEL - 1), CONF_KERNEL - 1), :] = cc_ref[...]
        hst[...] = h0_ref[...]

    ua_ext[:, pl.ds(LRU_HIST, n_t), :] = proj_ref[:, pl.ds(0, c)].reshape(n_seq, n_t, c)
    c_a = _causal_conv(ua_ext[...], wca_ref, n_t, LRU_HIST)
    c_a = (c_a + bca_ref[...][None]).reshape(rows, c)
    c_a16 = c_a.astype(BF16)
    r = jax.nn.sigmoid(jnp.dot(c_a16, wa_bd[...], preferred_element_type=F32) + ba_ref[...])
    i = jax.nn.sigmoid(jnp.dot(c_a16, wi_bd[...], preferred_element_type=F32) + bi_ref[...])
    neg_lam = -lam_ref[...]
    softplus = jnp.maximum(neg_lam, 0.0) + jnp.log1p(jnp.exp(-jnp.abs(neg_lam)))
    log_a = (-LRU_C * r) * softplus
    a = jnp.exp(log_a)
    a_scr[...] = a
    u_scr[...] = jnp.sqrt(-jnp.tanh(log_a) * (a * a + 1.0)) * (i * c_a)

    row8 = lax.broadcasted_iota(jnp.int32, (SUBLANES, c), 0)
    groups = n_t // SUBLANES
    for s in range(n_seq):
        def scan_group(g, carry, s=s):
            off = pl.multiple_of(s * n_t + g * SUBLANES, SUBLANES)
            a = a_scr[pl.ds(off, SUBLANES), :]
            u = u_scr[pl.ds(off, SUBLANES), :]
            for d in (1, 2, 4):
                keep = row8 >= d
                a_sh = jnp.where(keep, pltpu.roll(a, d, 0), 1.0)
                u_sh = jnp.where(keep, pltpu.roll(u, d, 0), 0.0)
                u = a * u_sh + u
                a = a * a_sh
            h = a * carry + u
            h_scr[pl.ds(off, SUBLANES), :] = h
            return jnp.broadcast_to(h[SUBLANES - 1:SUBLANES, :], (SUBLANES, c))

        carry = lax.fori_loop(0, groups, scan_group, jnp.broadcast_to(hst[s], (SUBLANES, c)))
        hst[s] = carry[0:1, :]

    y_a = jax.nn.gelu(proj_ref[:, pl.ds(c, c)]) * h_scr[...]
    y_ref[:, pl.ds(0, c)] = _rms_norm(y_a, nag_ref[...]).astype(y_ref.dtype)

    glu = proj_ref[:, pl.ds(2 * c, c)] * jax.nn.sigmoid(proj_ref[:, pl.ds(3 * c, c)])
    glu_ext[:, pl.ds(CONF_HIST, n_t), :] = glu.reshape(n_seq, n_t, c)
    c_b = _causal_conv(glu_ext[...], wcb_ref, n_t, CONF_HIST)
    c_b = (c_b + bcb_ref[...][None]).reshape(rows, c)
    mu = jnp.mean(c_b, axis=-1, keepdims=True)
    cen = c_b - mu
    var = jnp.mean(cen * cen, axis=-1, keepdims=True)
    ln = cen * lax.rsqrt(var + EPS) * lng_ref[...] + lnb_ref[...]
    y_b = ln * jax.nn.sigmoid(ln)
    y_ref[:, pl.ds(c, c)] = _rms_norm(y_b, nbg_ref[...]).astype(y_ref.dtype)

    lc_out[...] = ua_ext[:, pl.ds(LRU_HIST + n_t - (LRU_CONV - 1), LRU_CONV - 1), :]
    cc_out[...] = glu_ext[:, pl.ds(CONF_HIST + n_t - (CONF_KERNEL - 1), CONF_KERNEL - 1), :]
    h_out[...] = hst[...]
    ua_ext[:, pl.ds(0, LRU_HIST), :] = ua_ext[:, pl.ds(n_t, LRU_HIST), :]
    glu_ext[:, pl.ds(0, CONF_HIST), :] = glu_ext[:, pl.ds(n_t, CONF_HIST), :]


def _mixer(proj, row_block0, lc, h0, cc, weights, *, n_seq, n_t, n_tiles):
    b = lc.shape[1]
    c = W_A
    rows = n_seq * n_t
    n_sb = b // n_seq

    def row_map(sb, t):
        return (row_block0 + sb * n_tiles + t, 0)

    def seq_map(sb, t):
        return (sb, 0, 0)

    def depth_seq_map(sb, t):
        return (0, sb, 0, 0)

    w_specs = [pl.BlockSpec(w.shape, (lambda sb, t, nd=w.ndim: (0,) * nd)) for w in weights]
    lc_spec = pl.BlockSpec((None, n_seq, LRU_CONV - 1, c), depth_seq_map)
    cc_spec = pl.BlockSpec((None, n_seq, CONF_KERNEL - 1, c), depth_seq_map)
    return pl.pallas_call(
        functools.partial(_mixer_kernel, n_seq=n_seq, n_t=n_t),
        grid=(n_sb, n_tiles),
        in_specs=[
            pl.BlockSpec((rows, 4 * c), row_map),
            lc_spec,
            pl.BlockSpec((n_seq, 1, c), seq_map),
            cc_spec,
        ] + w_specs,
        out_specs=[
            pl.BlockSpec((rows, 2 * c), lambda sb, t: (sb * n_tiles + t, 0)),
            lc_spec,
            pl.BlockSpec((n_seq, 1, c), seq_map),
            cc_spec,
        ],
        out_shape=[
            jax.ShapeDtypeStruct((b * n_tiles * n_t, 2 * c), BF16),
            jax.ShapeDtypeStruct((1, b, LRU_CONV - 1, c), F32),
            jax.ShapeDtypeStruct((b, 1, c), F32),
            jax.ShapeDtypeStruct((1, b, CONF_KERNEL - 1, c), F32),
        ],
        scratch_shapes=[
            pltpu.VMEM((n_seq, LRU_HIST + n_t, c), F32),
            pltpu.VMEM((n_seq, CONF_HIST + n_t, c), F32),
            pltpu.VMEM((n_seq, 1, c), F32),
            pltpu.VMEM((rows, c), F32),
            pltpu.VMEM((rows, c), F32),
            pltpu.VMEM((rows, c), F32),
            pltpu.VMEM((c, c), BF16),
            pltpu.VMEM((c, c), BF16),
        ],
        compiler_params=_params(("arbitrary", "arbitrary")),
        name="mixer",
    )(proj, lc, h0, cc, *weights)


def _out_proj_kernel(*refs, n_blocks):
    k = len(n_blocks)
    y_refs, x_refs = refs[:k], refs[k:2 * k]
    w_ref, g_ref, rw_ref, x1_ref, xn_ref, logit_ref, w16 = refs[2 * k:]

    @pl.when(pl.program_id(0) == 0)
    def _():
        w16[...] = w_ref[...].astype(BF16)

    def body(p):
        x1 = x_refs[p][...] + jnp.dot(y_refs[p][...], w16[...], preferred_element_type=F32)
        x1_ref[...] = x1
        xn = _rms_norm(x1, g_ref[...]).astype(BF16)
        xn_ref[...] = xn
        logit_ref[...] = jnp.dot(xn, rw_ref[...], preferred_element_type=F32)

    _for_part(pl.program_id(0), n_blocks, body)


def _out_proj(y_parts, x_parts, w_out, g2, router_w, *, tm):
    d = w_out.shape[0]
    e = router_w.shape[1]
    n_blocks = tuple(x.shape[0] // tm for x in x_parts)
    part_specs = [
        pl.BlockSpec((tm, d), _part_map(start, nb, 0)) for start, nb in zip(_part_starts(n_blocks), n_blocks)
    ]
    n = sum(n_blocks) * tm
    const = lambda i: (0, 0)
    row = lambda i: (i, 0)
    return pl.pallas_call(
        functools.partial(_out_proj_kernel, n_blocks=n_blocks),
        grid=(sum(n_blocks),),
        in_specs=part_specs + part_specs + [
            pl.BlockSpec((d, d), const, pipeline_mode=pl.Buffered(1)),
            pl.BlockSpec((1, d), const),
            pl.BlockSpec((d, e), const),
        ],
        out_specs=[pl.BlockSpec((tm, d), row), pl.BlockSpec((tm, d), row), pl.BlockSpec((tm, e), row)],
        out_shape=[
            jax.ShapeDtypeStruct((n, d), F32),
            jax.ShapeDtypeStruct((n, d), BF16),
            jax.ShapeDtypeStruct((n, e), F32),
        ],
        scratch_shapes=[pltpu.VMEM((d, d), BF16)],
        compiler_params=_params(("arbitrary",)),
        name="out_proj",
    )(*y_parts, *x_parts, w_out, g2, router_w)


def _first_argmax(work, index, sentinel):
    m = jnp.max(work, axis=0, keepdims=True)
    return jnp.min(jnp.where(work == m, index, sentinel), axis=0, keepdims=True)


def _route_kernel(logit_ref, bias_ref, eidx_ref, gate_ref, rank_ref, count_ref):
    n_tok = logit_ref.shape[1]

    scores = jax.nn.sigmoid(logit_ref[...])
    biased = scores + bias_ref[...]
    grouped = biased.reshape(N_EXPERT_GROUPS, EXPERTS_PER_GROUP, n_tok)
    in_group = lax.broadcasted_iota(jnp.int32, grouped.shape, 1)
    top1 = jnp.max(grouped, axis=1, keepdims=True)
    first1 = jnp.min(jnp.where(grouped == top1, in_group, EXPERTS_PER_GROUP), axis=1, keepdims=True)
    top2 = jnp.max(jnp.where(in_group == first1, -jnp.inf, grouped), axis=1, keepdims=True)
    group_scores = (top1 + top2).reshape(N_EXPERT_GROUPS, n_tok)

    group_id = lax.broadcasted_iota(jnp.int32, group_scores.shape, 0)
    group_on = jnp.zeros(group_scores.shape, F32)
    work = group_scores
    for _ in range(TOPK_GROUPS):
        pick = group_id == _first_argmax(work, group_id, N_EXPERT_GROUPS)
        group_on = jnp.where(pick, 1.0, group_on)
        work = jnp.where(pick, -jnp.inf, work)

    masked = jnp.where(group_on.reshape(N_EXPERT_GROUPS, 1, n_tok) > 0.0, grouped, -jnp.inf)
    work = masked.reshape(N_EXPERTS, n_tok)
    expert_id = lax.broadcasted_iota(jnp.int32, work.shape, 0)
    ids, sel = [], []
    chosen = jnp.zeros(work.shape, F32)
    for _ in range(TOP_K):
        first = _first_argmax(work, expert_id, N_EXPERTS)
        pick = expert_id == first
        ids.append(first)
        sel.append(jnp.sum(jnp.where(pick, scores, 0.0), axis=0, keepdims=True))
        chosen = jnp.where(pick, 1.0, chosen)
        work = jnp.where(pick, -jnp.inf, work)
    sel = jnp.concatenate(sel, axis=0)
    eidx_ref[...] = jnp.concatenate(ids, axis=0)
    gate_ref[...] = sel / (jnp.sum(sel, axis=0, keepdims=True) + 1e-20) * ROUTED_SCALE

    earlier = lax.broadcasted_iota(jnp.int32, (n_tok, n_tok), 0) < lax.broadcasted_iota(jnp.int32, (n_tok, n_tok), 1)
    before = jnp.dot(chosen.astype(BF16), earlier.astype(BF16), preferred_element_type=F32)
    rank_ref[...] = jnp.concatenate(
        [jnp.sum(jnp.where(expert_id == ids[k], before, 0.0), axis=0, keepdims=True) for k in range(TOP_K)], axis=0
    ).astype(jnp.int32)
    count_ref[0] = jnp.sum(chosen, axis=1, keepdims=True).astype(jnp.int32)


def _route(logits_t, bias, *, tl):
    e, n = logits_t.shape
    pick_spec = pl.BlockSpec((TOP_K, tl), lambda i: (0, i))
    return pl.pallas_call(
        _route_kernel,
        grid=(n // tl,),
        in_specs=[pl.BlockSpec((e, tl), lambda i: (0, i)), pl.BlockSpec((e, 1), lambda i: (0, 0))],
        out_specs=[pick_spec, pick_spec, pick_spec, pl.BlockSpec((1, e, 1), lambda i: (i, 0, 0))],
        out_shape=[
            jax.ShapeDtypeStruct((TOP_K, n), jnp.int32),
            jax.ShapeDtypeStruct((TOP_K, n), F32),
            jax.ShapeDtypeStruct((TOP_K, n), jnp.int32),
            jax.ShapeDtypeStruct((n // tl, e, 1), jnp.int32),
        ],
        compiler_params=_params(("arbitrary",)),
        name="route",
    )(logits_t, bias)


def _gate_lanes(gate):
    g1 = gate.astype(BF16).astype(F32)
    rest = gate - g1
    g2 = rest.astype(BF16).astype(F32)
    g3 = (rest - g2).astype(BF16).astype(F32)
    lane = lax.broadcasted_iota(jnp.int32, (gate.shape[0], LANES), 1)
    return jnp.where(lane == 0, g1, jnp.where(lane == 1, g2, jnp.where(lane == 2, g3, 0.0))).astype(BF16)


def _tab_offsets():
    offsets, col = [], len(CHUNK_ROWS)
    for cap in CHUNK_MAX:
        offsets.append((col, col + cap))
        col += 2 * cap
    return offsets, col


def _segment_copies(tab_ref, block, make_copy):
    offsets, width = _tab_offsets()
    base = block * width
    counts = []
    for k, (rows, (local_col, global_col)) in enumerate(zip(CHUNK_ROWS, offsets)):
        n_chunks = tab_ref[base + k]

        def start_one(j, rows=rows, local_col=local_col, global_col=global_col):
            make_copy(
                pl.multiple_of(tab_ref[base + local_col + j], SEG_ROWS),
                pl.multiple_of(tab_ref[base + global_col + j], SEG_ROWS),
                rows,
            ).start()

        def per_group(g, carry):
            for u in range(ISSUE_UNROLL):
                start_one(g * ISSUE_UNROLL + u)
            return carry

        def per_chunk(j, carry):
            start_one(j)
            return carry

        n_groups = n_chunks // ISSUE_UNROLL
        lax.fori_loop(0, n_groups, per_group, 0)
        lax.fori_loop(n_groups * ISSUE_UNROLL, n_chunks, per_chunk, 0)
        counts.append(n_chunks)
    return counts


def _wait_copies(counts, make_copy):
    for rows, count in zip(CHUNK_ROWS, counts):
        def wait_group(c, carry, rows=rows):
            make_copy(0, 0, rows * WAIT_GROUP).wait()
            return carry

        def wait_one(c, carry, rows=rows):
            make_copy(0, 0, rows).wait()
            return carry

        n_groups = count // WAIT_GROUP
        lax.fori_loop(0, n_groups, wait_group, 0)
        lax.fori_loop(n_groups * WAIT_GROUP, count, wait_one, 0)


def _dispatch_kernel(
    tab_ref, gap_ref, pos_ref, gate_ref, x_ref, xs_hbm, sorted_scr, zero_scr, pending, fill_counts, sem, zero_sem
):
    step = pl.program_id(0)
    slot = step % 2
    d = x_ref.shape[1]
    n_tok = x_ref.shape[0]
    tile_rows = zero_scr.shape[0]

    def zero_chunk(row):
        return pltpu.make_async_copy(
            zero_scr.at[pl.ds(0, SEG_ROWS), :],
            xs_hbm.at[pl.ds(pl.multiple_of(row, SEG_ROWS), SEG_ROWS), :],
            zero_sem,
        )

    def zero_tile(row):
        return pltpu.make_async_copy(
            zero_scr, xs_hbm.at[pl.ds(pl.multiple_of(row, tile_rows), tile_rows), :], zero_sem
        )

    @pl.when(step == 0)
    def _():
        zero_scr[...] = jnp.zeros(zero_scr.shape, BF16)
        n_gaps = N_EXPERTS + 1

        def per_gap(g, totals):
            chunk_start, chunks = gap_ref[g], gap_ref[n_gaps + g]
            tile_start, tiles = gap_ref[2 * n_gaps + g], gap_ref[3 * n_gaps + g]

            def per_chunk(c, carry):
                zero_chunk(chunk_start + c * SEG_ROWS).start()
                return carry

            def per_tile(c, carry):
                zero_tile(tile_start + c * tile_rows).start()
                return carry

            lax.fori_loop(0, chunks, per_chunk, 0)
            lax.fori_loop(0, tiles, per_tile, 0)
            return totals[0] + chunks, totals[1] + tiles

        n_fill, n_tail = lax.fori_loop(0, n_gaps, per_gap, (0, 0))
        fill_counts[0] = n_fill
        fill_counts[1] = n_tail

    pos = pos_ref[...]
    gate = gate_ref[...]
    x = x_ref[...]
    piece_id = lax.broadcasted_iota(jnp.int32, (HOT_PIECE, n_tok), 0)
    pos_k = [jnp.broadcast_to(pos[k:k + 1, :], (HOT_PIECE, n_tok)) for k in range(TOP_K)]
    gate_k = [jnp.broadcast_to(gate[k:k + 1, :], (HOT_PIECE, n_tok)) for k in range(TOP_K)]
    for c in range(LOCAL_ROWS // SORT_CHUNK):
        one_hot, row_gate = [], []
        for p in range(SORT_CHUNK // HOT_PIECE):
            row_id = piece_id + (c * SORT_CHUNK + p * HOT_PIECE)
            hot = jnp.zeros((HOT_PIECE, n_tok), F32)
            gate_hit = jnp.zeros((HOT_PIECE, n_tok), F32)
            for k in range(TOP_K):
                hit = row_id == pos_k[k]
                hot = jnp.where(hit, 1.0, hot)
                gate_hit = jnp.where(hit, gate_k[k], gate_hit)
            one_hot.append(hot.astype(BF16))
            row_gate.append(jnp.sum(gate_hit, axis=1, keepdims=True))
        rows = jnp.dot(jnp.concatenate(one_hot, axis=0), x, preferred_element_type=F32)
        out_rows = pl.ds(c * SORT_CHUNK, SORT_CHUNK)
        sorted_scr[slot, out_rows, pl.ds(0, d)] = rows.astype(BF16)
        sorted_scr[slot, out_rows, pl.ds(d, LANES)] = _gate_lanes(jnp.concatenate(row_gate, axis=0))

    def copy_chunk_from(src_slot):
        def copy_chunk(local_row, global_row, rows):
            return pltpu.make_async_copy(
                sorted_scr.at[src_slot, pl.ds(local_row, rows), :], xs_hbm.at[pl.ds(global_row, rows), :], sem
            )

        return copy_chunk

    copy_chunk = copy_chunk_from(0)

    @pl.when(step > 0)
    def _():
        _wait_copies([pending[k] for k in range(len(CHUNK_ROWS))], copy_chunk)

    for s in range(2):
        @pl.when(slot == s)
        def _(s=s):
            for k, count in enumerate(_segment_copies(tab_ref, step, copy_chunk_from(s))):
                pending[k] = count

    @pl.when(step == pl.num_programs(0) - 1)
    def _():
        def wait_tile(c, carry):
            zero_tile(0).wait()
            return carry

        def wait_chunk(c, carry):
            zero_chunk(0).wait()
            return carry

        _wait_copies([pending[k] for k in range(len(CHUNK_ROWS))], copy_chunk)
        lax.fori_loop(0, fill_counts[0], wait_chunk, 0)
        lax.fori_loop(0, fill_counts[1], wait_tile, 0)


def _dispatch(tab, gaps, pos_t, gate_t, xn, n_sorted, *, tb, zero_rows):
    n, d = xn.shape
    width = d + LANES
    pick_spec = pl.BlockSpec((TOP_K, tb), lambda i, tab, gaps: (0, i))
    return pl.pallas_call(
        _dispatch_kernel,
        grid_spec=pltpu.PrefetchScalarGridSpec(
            num_scalar_prefetch=2,
            grid=(n // tb,),
            in_specs=[pick_spec, pick_spec, pl.BlockSpec((tb, d), lambda i, tab, gaps: (i, 0))],
            out_specs=pl.BlockSpec(memory_space=pl.ANY),
            scratch_shapes=[
                pltpu.VMEM((2, LOCAL_ROWS, width), BF16),
                pltpu.VMEM((zero_rows, width), BF16),
                pltpu.SMEM((len(CHUNK_ROWS),), jnp.int32),
                pltpu.SMEM((2,), jnp.int32),
                pltpu.SemaphoreType.DMA,
                pltpu.SemaphoreType.DMA,
            ],
        ),
        out_shape=jax.ShapeDtypeStruct((n_sorted, width), BF16),
        compiler_params=_params(("arbitrary",)),
        name="dispatch",
    )(tab, gaps, pos_t, gate_t, xn)


def _experts_kernel(
    expert_ref, used_ref, fresh_ref, next_ref, src_ref, xs_ref, wg_hbm, wu_hbm, wd_hbm, ys_ref,
    wg32, wu32, wd32, wg16, wu16, wd16, sems,
):
    it = pl.program_id(0)
    d = ys_ref.shape[1]

    def weight_copies(e):
        return (
            pltpu.make_async_copy(wg_hbm.at[e], wg32, sems.at[0]),
            pltpu.make_async_copy(wu_hbm.at[e], wu32, sems.at[1]),
            pltpu.make_async_copy(wd_hbm.at[e], wd32, sems.at[2]),
        )

    @pl.when(it == 0)
    def _():
        for copy in weight_copies(expert_ref[0]):
            copy.start()

    @pl.when(used_ref[it] == 0)
    def _():
        ys_ref[...] = jnp.zeros(ys_ref.shape, ys_ref.dtype)

    @pl.when((used_ref[it] > 0) & (fresh_ref[it] == 1))
    def _():
        for copy in weight_copies(expert_ref[it]):
            copy.wait()
        wg16[...] = wg32[...].astype(BF16)
        wu16[...] = wu32[...].astype(BF16)
        wd16[...] = wd32[...].astype(BF16)

        @pl.when(next_ref[it] >= 0)
        def _():
            for copy in weight_copies(next_ref[it]):
                copy.start()

    tm = ys_ref.shape[0]
    for n_sub in range(1, tm // EXPERT_SUB + 1):
        rows = n_sub * EXPERT_SUB

        @pl.when(used_ref[it] == n_sub)
        def _(rows=rows):
            x = xs_ref[pl.ds(0, rows), pl.ds(0, d)]
            gate = jnp.sum(xs_ref[pl.ds(0, rows), pl.ds(d, LANES)].astype(F32), axis=1, keepdims=True)
            g = jnp.dot(x, wg16[...], preferred_element_type=F32)
            u = jnp.dot(x, wu16[...], preferred_element_type=F32)
            h = g * jax.nn.sigmoid(g) * u * gate
            y = jnp.dot(h.astype(BF16), wd16[...], preferred_element_type=F32)
            ys_ref[pl.ds(0, rows), :] = y.astype(ys_ref.dtype)
            if rows < tm:
                ys_ref[pl.ds(rows, tm - rows), :] = jnp.zeros((tm - rows, d), ys_ref.dtype)


def _experts(tiles, xs, w_gate, w_up, w_down, *, tm):
    p = xs.shape[0]
    d, f = w_gate.shape[1:]
    row_map = lambda i, expert, used, fresh, nxt, src: (i, 0)
    src_map = lambda i, expert, used, fresh, nxt, src: (src[i], 0)
    hbm = pl.BlockSpec(memory_space=pl.ANY)
    return pl.pallas_call(
        _experts_kernel,
        grid_spec=pltpu.PrefetchScalarGridSpec(
            num_scalar_prefetch=5,
            grid=(p // tm,),
            in_specs=[pl.BlockSpec((tm, xs.shape[1]), src_map), hbm, hbm, hbm],
            out_specs=pl.BlockSpec((tm, d), row_map),
            scratch_shapes=[
                pltpu.VMEM((d, f), F32), pltpu.VMEM((d, f), F32), pltpu.VMEM((f, d), F32),
                pltpu.VMEM((d, f), BF16), pltpu.VMEM((d, f), BF16), pltpu.VMEM((f, d), BF16),
                pltpu.SemaphoreType.DMA((3,)),
            ],
        ),
        out_shape=jax.ShapeDtypeStruct((p, d), BF16),
        compiler_params=_params(("arbitrary",)),
        name="experts",
    )(*tiles, xs, w_gate, w_up, w_down)


def _combine_kernel(tab_ref, pos_ref, x1_ref, xn_ref, sg_ref, su_ref, sd_ref, fg_ref, ys_hbm, *rest, n_blocks):
    o_refs = rest[:len(n_blocks)]
    local_scr, pending, sems = rest[len(n_blocks):]
    n_tok = x1_ref.shape[0]
    step = pl.program_id(0)
    slot = step % 2

    n_sizes = len(CHUNK_ROWS)

    def copy_chunk_into(dst_slot):
        def copy_chunk(local_row, global_row, rows):
            return pltpu.make_async_copy(
                ys_hbm.at[pl.ds(global_row, rows), :],
                local_scr.at[dst_slot, pl.ds(local_row, rows), :],
                sems.at[dst_slot],
            )

        return copy_chunk

    def fetch(block, dst_slot):
        for k, count in enumerate(_segment_copies(tab_ref, block, copy_chunk_into(dst_slot))):
            pending[dst_slot * n_sizes + k] = count

    @pl.when(step == 0)
    def _():
        local_scr[...] = jnp.zeros(local_scr.shape, local_scr.dtype)
        fetch(0, 0)

    for s in range(2):
        @pl.when((step + 1 < pl.num_programs(0)) & (slot == s))
        def _(s=s):
            fetch(step + 1, 1 - s)

    xn = xn_ref[...]
    sg = jnp.dot(xn, sg_ref[...], preferred_element_type=F32)
    su = jnp.dot(xn, su_ref[...], preferred_element_type=F32)
    acc = x1_ref[...] + jnp.dot(
        (sg * jax.nn.sigmoid(sg) * su).astype(BF16), sd_ref[...], preferred_element_type=F32
    )

    piece_id = lax.broadcasted_iota(jnp.int32, (HOT_PIECE, LANES), 1)
    one_hot = []
    for r in range(n_tok // HOT_PIECE):
        pos = pos_ref[pl.ds(r * HOT_PIECE, HOT_PIECE), :]
        pos_k = [jnp.broadcast_to(pos[:, k:k + 1], (HOT_PIECE, LANES)) for k in range(TOP_K)]
        cols = []
        for c in range(LOCAL_ROWS // LANES):
            col_id = piece_id + c * LANES
            hot = jnp.zeros((HOT_PIECE, LANES), F32)
            for k in range(TOP_K):
                hot = jnp.where(col_id == pos_k[k], 1.0, hot)
            cols.append(hot.astype(BF16))
        one_hot.append(jnp.concatenate(cols, axis=1))
    one_hot = jnp.concatenate(one_hot, axis=0)

    _wait_copies([pending[slot * n_sizes + k] for k in range(n_sizes)], copy_chunk_into(slot))
    routed = jnp.dot(one_hot, local_scr[slot], preferred_element_type=F32)
    out = _rms_norm(acc + routed, fg_ref[...])

    def body(p):
        o_refs[p][...] = out

    _for_part(pl.program_id(0), n_blocks, body)


def _combine(tab, pos, x1, xn, sh_gate, sh_up, sh_down, final_g, ys, part_rows, *, tb):
    n, d = x1.shape
    f = sh_gate.shape[1]
    n_blocks = tuple(r // tb for r in part_rows)
    const = lambda i, tab: (0, 0)
    row = lambda i, tab: (i, 0)
    return pl.pallas_call(
        functools.partial(_combine_kernel, n_blocks=n_blocks),
        grid_spec=pltpu.PrefetchScalarGridSpec(
            num_scalar_prefetch=1,
            grid=(n // tb,),
            in_specs=[
                pl.BlockSpec((tb, TOP_K), row),
                pl.BlockSpec((tb, d), row),
                pl.BlockSpec((tb, d), row),
                pl.BlockSpec((d, f), const, pipeline_mode=pl.Buffered(1)),
                pl.BlockSpec((d, f), const, pipeline_mode=pl.Buffered(1)),
                pl.BlockSpec((f, d), const, pipeline_mode=pl.Buffered(1)),
                pl.BlockSpec((1, d), const),
                pl.BlockSpec(memory_space=pl.ANY),
            ],
            out_specs=[
                pl.BlockSpec((tb, d), _part_map(start, nb, 0))
                for start, nb in zip(_part_starts(n_blocks), n_blocks)
            ],
            scratch_shapes=[
                pltpu.VMEM((2, LOCAL_ROWS, d), BF16),
                pltpu.SMEM((2 * len(CHUNK_ROWS),), jnp.int32),
                pltpu.SemaphoreType.DMA((2,)),
            ],
        ),
        out_shape=[jax.ShapeDtypeStruct((r, d), F32) for r in part_rows],
        compiler_params=_params(("arbitrary",)),
        name="combine",
    )(tab, pos, x1, xn, sh_gate, sh_up, sh_down, final_g, ys)


def _segment_tables(block_counts, n_rows, *, tm):
    padded = (block_counts + SEG_ROWS - 1) // SEG_ROWS * SEG_ROWS
    local_start = jnp.cumsum(padded, axis=1) - padded
    per_expert = jnp.sum(padded, axis=0)
    region = (per_expert + tm - 1) // tm * tm
    region_start = jnp.concatenate([jnp.zeros((1,), jnp.int32), jnp.cumsum(region).astype(jnp.int32)])
    global_start = region_start[None, :N_EXPERTS] + jnp.cumsum(padded, axis=0) - padded
    counts, lists = [], []
    done = jnp.zeros_like(padded)
    for rows, cap in zip(CHUNK_ROWS, CHUNK_MAX):
        per_seg = (padded - done) // rows
        seg_end = jnp.cumsum(per_seg, axis=1)
        j = jnp.arange(cap, dtype=jnp.int32)
        seg_of_chunk = jnp.sum((seg_end[:, None, :] <= j[None, :, None]).astype(jnp.int32), axis=2)
        in_seg = seg_of_chunk[:, :, None] == jnp.arange(N_EXPERTS, dtype=jnp.int32)
        first = done - (seg_end - per_seg) * rows
        for start in (local_start, global_start):
            lists.append(jnp.sum(jnp.where(in_seg, (start + first)[:, None, :], 0), axis=2) + j[None, :] * rows)
        counts.append(seg_end[:, N_EXPERTS - 1:])
        done = done + per_seg * rows
    tab = jnp.concatenate(counts + lists, axis=1).astype(jnp.int32).reshape(-1)

    region_end = region_start[:N_EXPERTS] + per_expert
    gap_start = jnp.concatenate([region_end, region_start[N_EXPERTS:]])
    gap_end = jnp.concatenate([region_start[1:], jnp.full((1,), n_rows, jnp.int32)])
    sub_start = jnp.minimum((gap_start + EXPERT_SUB - 1) // EXPERT_SUB * EXPERT_SUB, gap_end)
    gaps = jnp.concatenate(
        [gap_start, (sub_start - gap_start) // SEG_ROWS, sub_start, (gap_end - sub_start) // EXPERT_SUB]
    ).astype(jnp.int32)

    tile_start = jnp.arange(n_rows // tm, dtype=jnp.int32) * tm
    expert = jnp.clip(
        jnp.sum((region_start[None, :] <= tile_start[:, None]).astype(jnp.int32), axis=1) - 1, 0, N_EXPERTS - 1
    )
    tile_end = jnp.sum(jnp.where(expert[:, None] == jnp.arange(N_EXPERTS), region_end[None, :], 0), axis=1)
    used = jnp.clip((tile_end - tile_start + EXPERT_SUB - 1) // EXPERT_SUB, 0, tm // EXPERT_SUB).astype(jnp.int32)
    fresh = jnp.concatenate([jnp.ones((1,), jnp.int32), (expert[1:] != expert[:-1]).astype(jnp.int32)])
    experts = jnp.arange(N_EXPERTS, dtype=jnp.int32)
    later = (experts[None, :] > expert[:, None]) & (per_expert[None, :] > 0)
    nxt = jnp.min(jnp.where(later, experts[None, :], N_EXPERTS), axis=1)
    nxt = jnp.where(nxt < N_EXPERTS, nxt, -1).astype(jnp.int32)
    tile_id = jnp.arange(n_rows // tm, dtype=jnp.int32)
    src = jnp.minimum(tile_id, jnp.max(jnp.where(used > 0, tile_id, 0)))
    return tab, gaps, (expert, used, fresh, nxt, src), local_start.astype(jnp.int32)


ROW_TM = 512
OUT_PROJ_TM = 256
IN_PROJ_TN = 2048
PROMPT_T = 256
SAMPLE_SEQS = 16
EXPERT_TM = 512
EXPERT_SUB = 128


def kernel(x_prompt, x_sample, state_lru_conv, state_lru_h, state_conf_conv, meta_tokens, norm1_g, w_in, lru_conv_w, lru_conv_b, lru_wa, lru_ba, lru_wi, lru_bi, lru_lambda, conf_conv_w, conf_conv_b, conf_ln_g, conf_ln_b, out_norm_a, out_norm_b, w_out, norm2_g, router_w, router_bias, exp_w_gate, exp_w_up, exp_w_down, sh_w_gate, sh_w_up, sh_w_down, final_norm_g):
    b_p, seq, d = x_prompt.shape
    b_s, t_s, _ = x_sample.shape
    n_p = b_p * seq
    n_s = b_s * t_s
    n = n_p + n_s
    x_parts = (x_prompt.reshape(n_p, d), x_sample.reshape(n_s, d))

    row = lambda v: v.reshape(1, -1)
    mixer_w = (
        lru_conv_w[0], row(lru_conv_b[0]),
        lru_wa[0], row(lru_ba[0]),
        lru_wi[0], row(lru_bi[0]),
        row(lru_lambda[0]),
        conf_conv_w[0], row(conf_conv_b[0]), row(conf_ln_g[0]), row(conf_ln_b[0]),
        row(out_norm_a[0]), row(out_norm_b[0]),
    )

    proj, proj_m = _in_proj(x_parts, meta_tokens, row(norm1_g[0]), w_in[0], tm=ROW_TM, tn=IN_PROJ_TN)
    _, m_lc, m_h, m_cc = _mixer(
        proj_m, 0,
        jnp.zeros((1, 1, LRU_CONV - 1, W_A), F32), jnp.zeros((1, 1, W_A), F32),
        jnp.zeros((1, 1, CONF_KERNEL - 1, W_B), F32),
        mixer_w, n_seq=1, n_t=N_META, n_tiles=1,
    )

    y_p, p_lc, p_h, p_cc = _mixer(
        proj, 0,
        jnp.broadcast_to(m_lc, (1, b_p) + m_lc.shape[2:]), jnp.broadcast_to(m_h, (b_p,) + m_h.shape[1:]),
        jnp.broadcast_to(m_cc, (1, b_p) + m_cc.shape[2:]),
        mixer_w, n_seq=1, n_t=PROMPT_T, n_tiles=seq // PROMPT_T,
    )
    y_s, s_lc, s_h, s_cc = _mixer(
        proj, n_p // (SAMPLE_SEQS * t_s),
        state_lru_conv, state_lru_h[0].reshape(b_s, 1, W_A), state_conf_conv,
        mixer_w, n_seq=SAMPLE_SEQS, n_t=t_s, n_tiles=1,
    )

    x1, xn2, logits = _out_proj(
        (y_p, y_s), x_parts, w_out[0], row(norm2_g[0]), router_w[0].astype(BF16), tm=OUT_PROJ_TM
    )

    eidx_t, gate_t, rank_t, block_counts = _route(logits.T, router_bias[0].reshape(N_EXPERTS, 1), tl=TOKEN_BLOCK)
    n_blocks = n // TOKEN_BLOCK
    n_sorted = n_blocks * (TOKEN_BLOCK * TOP_K + N_EXPERTS * (SEG_ROWS - 1)) + N_EXPERTS * (EXPERT_TM - 1)
    n_sorted = -(-n_sorted // EXPERT_TM) * EXPERT_TM
    tab, gaps, tiles, local_start = _segment_tables(block_counts[:, :, 0], n_sorted, tm=EXPERT_TM)
    expert_one_hot = eidx_t[:, :, None] == jnp.arange(N_EXPERTS, dtype=jnp.int32)
    token_start = jnp.repeat(local_start, TOKEN_BLOCK, axis=0)
    pos_t = jnp.sum(jnp.where(expert_one_hot, token_start[None], 0), axis=2) + rank_t

    xs = _dispatch(tab, gaps, pos_t, gate_t, xn2, n_sorted, tb=TOKEN_BLOCK, zero_rows=EXPERT_SUB)
    ys = _experts(tiles, xs, exp_w_gate[0], exp_w_up[0], exp_w_down[0], tm=EXPERT_TM)
    out_p, out_s = _combine(
        tab, pos_t.T, x1, xn2,
        sh_w_gate[0].astype(BF16), sh_w_up[0].astype(BF16), sh_w_down[0].astype(BF16),
        row(final_norm_g), ys, (n_p, n_s), tb=TOKEN_BLOCK,
    )

    return (
        out_p.reshape(b_p, seq, d), out_s.reshape(b_s, t_s, d),
        p_lc, p_h.reshape(1, b_p, W_A), p_cc,
        s_lc, s_h.reshape(1, b_s, W_A), s_cc,
    )
```

```python
import functools

import jax
import jax.numpy as jnp
from jax import lax
from jax.experimental import pallas as pl
from jax.experimental.pallas import tpu as pltpu

D_MODEL = 2048
N_META = 16
W_A = 1024
W_B = 1024
LRU_CONV = 4
LRU_C = 8.0
CONF_KERNEL = 31
N_EXPERTS = 64
N_EXPERT_GROUPS = 8
EXPERTS_PER_GROUP = 8
TOPK_GROUPS = 4
TOP_K = 8
ROUTED_SCALE = 2.5
EPS = 1e-6

SUBLANES = 8
LANES = 128
TOKEN_BLOCK = 256
SEG_ROWS = SUBLANES
LOCAL_ROWS = 2560
SORT_CHUNK = 128
HOT_PIECE = 16
CHUNK_ROWS = (SEG_ROWS,)
CHUNK_MAX = (LOCAL_ROWS // SEG_ROWS,)
ISSUE_UNROLL = 16
WAIT_GROUP = 32
assert LOCAL_ROWS >= TOKEN_BLOCK * TOP_K + N_EXPERTS * (SEG_ROWS - 1) and LOCAL_ROWS % SORT_CHUNK == 0
LRU_HIST = SUBLANES
CONF_HIST = 32
VMEM_LIMIT = 56 * 1024 * 1024

F32 = jnp.float32
BF16 = jnp.bfloat16


def _params(semantics):
    return pltpu.CompilerParams(dimension_semantics=semantics, vmem_limit_bytes=VMEM_LIMIT)


def _rms_norm(x, g):
    return x * lax.rsqrt(jnp.mean(x * x, axis=-1, keepdims=True) + EPS) * g


def _part_starts(n_blocks):
    starts, s = [], 0
    for nb in n_blocks:
        starts.append(s)
        s += nb
    return starts


def _part_map(start, nb, grid_axis):
    def index_map(*ids):
        return (jnp.clip(ids[grid_axis] - start, 0, nb - 1), 0)

    return index_map


def _for_part(i, n_blocks, body):
    for p, (start, nb) in enumerate(zip(_part_starts(n_blocks), n_blocks)):
        pl.when((i >= start) & (i < start + nb))(functools.partial(body, p))


def _in_proj_kernel(*refs, n_blocks):
    x_refs = refs[:len(n_blocks)]
    meta_ref, g_ref, w_ref, o_ref, meta_o_ref, w16 = refs[len(n_blocks):]

    def project(x):
        return jnp.dot(_rms_norm(x, g_ref[...]).astype(BF16), w16[...], preferred_element_type=F32)

    @pl.when(pl.program_id(1) == 0)
    def _():
        w16[...] = w_ref[...].astype(BF16)
        meta_o_ref[...] = project(meta_ref[...])

    def body(p):
        o_ref[...] = project(x_refs[p][...])

    _for_part(pl.program_id(1), n_blocks, body)


def _in_proj(x_parts, meta, g, w, *, tm, tn):
    d, f = w.shape
    n_meta = meta.shape[0]
    n_blocks = tuple(x.shape[0] // tm for x in x_parts)
    x_specs = [
        pl.BlockSpec((tm, d), _part_map(start, nb, 1)) for start, nb in zip(_part_starts(n_blocks), n_blocks)
    ]
    return pl.pallas_call(
        functools.partial(_in_proj_kernel, n_blocks=n_blocks),
        grid=(f // tn, sum(n_blocks)),
        in_specs=x_specs + [
            pl.BlockSpec((n_meta, d), lambda j, i: (0, 0)),
            pl.BlockSpec((1, d), lambda j, i: (0, 0)),
            pl.BlockSpec((d, tn), lambda j, i: (0, j), pipeline_mode=pl.Buffered(1)),
        ],
        out_specs=[pl.BlockSpec((tm, tn), lambda j, i: (i, j)), pl.BlockSpec((n_meta, tn), lambda j, i: (0, j))],
        out_shape=[
            jax.ShapeDtypeStruct((sum(n_blocks) * tm, f), F32),
            jax.ShapeDtypeStruct((n_meta, f), F32),
        ],
        scratch_shapes=[pltpu.VMEM((d, tn), BF16)],
        compiler_params=_params(("arbitrary", "arbitrary")),
        name="in_proj",
    )(*x_parts, meta, g, w)


def _causal_conv(ext, w_ref, n_t, hist):
    taps = w_ref.shape[0]
    length = ext.shape[1]
    shifted = {0: ext}
    acc = None
    for j in range(taps):
        q, r = divmod(hist - (taps - 1) + j, SUBLANES)
        if r not in shifted:
            shifted[r] = pltpu.roll(ext, length - r, 1)
        term = w_ref[pl.ds(j, 1), :][None] * shifted[r][:, q * SUBLANES:q * SUBLANES + n_t, :]
        acc = term if acc is None else acc + term
    return acc


def _mixer_kernel(
    proj_ref, lc_ref, h0_ref, cc_ref,
    wca_ref, bca_ref, wa_ref, ba_ref, wi_ref, bi_ref, lam_ref,
    wcb_ref, bcb_ref, lng_ref, lnb_ref, nag_ref, nbg_ref,
    y_ref, lc_out, h_out, cc_out,
    ua_ext, glu_ext, hst, a_scr, u_scr, h_scr, wa_bd, wi_bd,
    *, n_seq, n_t,
):
    t = pl.program_id(1)
    rows = n_seq * n_t
    c = W_A

    @pl.when((pl.program_id(0) == 0) & (t == 0))
    def _():
        n_heads, head_dim, _ = wa_ref.shape
        for w_ref, dense in ((wa_ref, wa_bd), (wi_ref, wi_bd)):
            dense[...] = jnp.zeros(dense.shape, dense.dtype)
            for h in range(n_heads):
                block = pl.ds(h * head_dim, head_dim)
                dense[block, block] = w_ref[h].astype(dense.dtype)

    @pl.when(t == 0)
    def _():
        ua_ext[:, pl.ds(0, LRU_HIST), :] = jnp.zeros((n_seq, LRU_HIST, c), F32)
        ua_ext[:, pl.ds(LRU_HIST - (LRU_CONV - 1), LRU_CONV - 1), :] = lc_ref[...]
        glu_ext[:, pl.ds(0, CONF_HIST), :] = jnp.zeros((n_seq, CONF_HIST, c), F32)
        glu_ext[:, pl.ds(CONF_HIST - (CONF_KERNEL - 1), CONF_KERNEL - 1), :] = cc_ref[...]
        hst[...] = h0_ref[...]

    ua_ext[:, pl.ds(LRU_HIST, n_t), :] = proj_ref[:, pl.ds(0, c)].reshape(n_seq, n_t, c)
    c_a = _causal_conv(ua_ext[...], wca_ref, n_t, LRU_HIST)
    c_a = (c_a + bca_ref[...][None]).reshape(rows, c)
    c_a16 = c_a.astype(BF16)
    r = jax.nn.sigmoid(jnp.dot(c_a16, wa_bd[...], preferred_element_type=F32) + ba_ref[...])
    i = jax.nn.sigmoid(jnp.dot(c_a16, wi_bd[...], preferred_element_type=F32) + bi_ref[...])
    neg_lam = -lam_ref[...]
    softplus = jnp.maximum(neg_lam, 0.0) + jnp.log1p(jnp.exp(-jnp.abs(neg_lam)))
    log_a = (-LRU_C * r) * softplus
    a = jnp.exp(log_a)
    a_scr[...] = a
    u_scr[...] = jnp.sqrt(-jnp.tanh(log_a) * (a * a + 1.0)) * (i * c_a)

    row8 = lax.broadcasted_iota(jnp.int32, (SUBLANES, c), 0)
    groups = n_t // SUBLANES
    for s in range(n_seq):
        def scan_group(g, carry, s=s):
            off = pl.multiple_of(s * n_t + g * SUBLANES, SUBLANES)
            a = a_scr[pl.ds(off, SUBLANES), :]
            u = u_scr[pl.ds(off, SUBLANES), :]
            for d in (1, 2, 4):
                keep = row8 >= d
                a_sh = jnp.where(keep, pltpu.roll(a, d, 0), 1.0)
                u_sh = jnp.where(keep, pltpu.roll(u, d, 0), 0.0)
                u = a * u_sh + u
                a = a * a_sh
            h = a * carry + u
            h_scr[pl.ds(off, SUBLANES), :] = h
            return jnp.broadcast_to(h[SUBLANES - 1:SUBLANES, :], (SUBLANES, c))

        carry = lax.fori_loop(0, groups, scan_group, jnp.broadcast_to(hst[s], (SUBLANES, c)))
        hst[s] = carry[0:1, :]

    y_a = jax.nn.gelu(proj_ref[:, pl.ds(c, c)]) * h_scr[...]
    y_ref[:, pl.ds(0, c)] = _rms_norm(y_a, nag_ref[...]).astype(y_ref.dtype)

    glu = proj_ref[:, pl.ds(2 * c, c)] * jax.nn.sigmoid(proj_ref[:, pl.ds(3 * c, c)])
    glu_ext[:, pl.ds(CONF_HIST, n_t), :] = glu.reshape(n_seq, n_t, c)
    c_b = _causal_conv(glu_ext[...], wcb_ref, n_t, CONF_HIST)
    c_b = (c_b + bcb_ref[...][None]).reshape(rows, c)
    mu = jnp.mean(c_b, axis=-1, keepdims=True)
    cen = c_b - mu
    var = jnp.mean(cen * cen, axis=-1, keepdims=True)
    ln = cen * lax.rsqrt(var + EPS) * lng_ref[...] + lnb_ref[...]
    y_b = ln * jax.nn.sigmoid(ln)
    y_ref[:, pl.ds(c, c)] = _rms_norm(y_b, nbg_ref[...]).astype(y_ref.dtype)

    lc_out[...] = ua_ext[:, pl.ds(LRU_HIST + n_t - (LRU_CONV - 1), LRU_CONV - 1), :]
    cc_out[...] = glu_ext[:, pl.ds(CONF_HIST + n_t - (CONF_KERNEL - 1), CONF_KERNEL - 1), :]
    h_out[...] = hst[...]
    ua_ext[:, pl.ds(0, LRU_HIST), :] = ua_ext[:, pl.ds(n_t, LRU_HIST), :]
    glu_ext[:, pl.ds(0, CONF_HIST), :] = glu_ext[:, pl.ds(n_t, CONF_HIST), :]


def _mixer(proj, row_block0, lc, h0, cc, weights, *, n_seq, n_t, n_tiles):
    b = lc.shape[1]
    c = W_A
    rows = n_seq * n_t
    n_sb = b // n_seq

    def row_map(sb, t):
        return (row_block0 + sb * n_tiles + t, 0)

    def seq_map(sb, t):
        return (sb, 0, 0)

    def depth_seq_map(sb, t):
        return (0, sb, 0, 0)

    w_specs = [pl.BlockSpec(w.shape, (lambda sb, t, nd=w.ndim: (0,) * nd)) for w in weights]
    lc_spec = pl.BlockSpec((None, n_seq, LRU_CONV - 1, c), depth_seq_map)
    cc_spec = pl.BlockSpec((None, n_seq, CONF_KERNEL - 1, c), depth_seq_map)
    return pl.pallas_call(
        functools.partial(_mixer_kernel, n_seq=n_seq, n_t=n_t),
        grid=(n_sb, n_tiles),
        in_specs=[
            pl.BlockSpec((rows, 4 * c), row_map),
            lc_spec,
            pl.BlockSpec((n_seq, 1, c), seq_map),
            cc_spec,
        ] + w_specs,
        out_specs=[
            pl.BlockSpec((rows, 2 * c), lambda sb, t: (sb * n_tiles + t, 0)),
            lc_spec,
            pl.BlockSpec((n_seq, 1, c), seq_map),
            cc_spec,
        ],
        out_shape=[
            jax.ShapeDtypeStruct((b * n_tiles * n_t, 2 * c), BF16),
            jax.ShapeDtypeStruct((1, b, LRU_CONV - 1, c), F32),
            jax.ShapeDtypeStruct((b, 1, c), F32),
            jax.ShapeDtypeStruct((1, b, CONF_KERNEL - 1, c), F32),
        ],
        scratch_shapes=[
            pltpu.VMEM((n_seq, LRU_HIST + n_t, c), F32),
            pltpu.VMEM((n_seq, CONF_HIST + n_t, c), F32),
            pltpu.VMEM((n_seq, 1, c), F32),
            pltpu.VMEM((rows, c), F32),
            pltpu.VMEM((rows, c), F32),
            pltpu.VMEM((rows, c), F32),
            pltpu.VMEM((c, c), BF16),
            pltpu.VMEM((c, c), BF16),
        ],
        compiler_params=_params(("arbitrary", "arbitrary")),
        name="mixer",
    )(proj, lc, h0, cc, *weights)


def _out_proj_kernel(*refs, n_blocks):
    k = len(n_blocks)
    y_refs, x_refs = refs[:k], refs[k:2 * k]
    w_ref, g_ref, rw_ref, x1_ref, xn_ref, logit_ref, w16 = refs[2 * k:]

    @pl.when(pl.program_id(0) == 0)
    def _():
        w16[...] = w_ref[...].astype(BF16)

    def body(p):
        x1 = x_refs[p][...] + jnp.dot(y_refs[p][...], w16[...], preferred_element_type=F32)
        x1_ref[...] = x1
        xn = _rms_norm(x1, g_ref[...]).astype(BF16)
        xn_ref[...] = xn
        logit_ref[...] = jnp.dot(xn, rw_ref[...], preferred_element_type=F32)

    _for_part(pl.program_id(0), n_blocks, body)


def _out_proj(y_parts, x_parts, w_out, g2, router_w, *, tm):
    d = w_out.shape[0]
    e = router_w.shape[1]
    n_blocks = tuple(x.shape[0] // tm for x in x_parts)
    part_specs = [
        pl.BlockSpec((tm, d), _part_map(start, nb, 0)) for start, nb in zip(_part_starts(n_blocks), n_blocks)
    ]
    n = sum(n_blocks) * tm
    const = lambda i: (0, 0)
    row = lambda i: (i, 0)
    return pl.pallas_call(
        functools.partial(_out_proj_kernel, n_blocks=n_blocks),
        grid=(sum(n_blocks),),
        in_specs=part_specs + part_specs + [
            pl.BlockSpec((d, d), const, pipeline_mode=pl.Buffered(1)),
            pl.BlockSpec((1, d), const),
            pl.BlockSpec((d, e), const),
        ],
        out_specs=[pl.BlockSpec((tm, d), row), pl.BlockSpec((tm, d), row), pl.BlockSpec((tm, e), row)],
        out_shape=[
            jax.ShapeDtypeStruct((n, d), F32),
            jax.ShapeDtypeStruct((n, d), BF16),
            jax.ShapeDtypeStruct((n, e), F32),
        ],
        scratch_shapes=[pltpu.VMEM((d, d), BF16)],
        compiler_params=_params(("arbitrary",)),
        name="out_proj",
    )(*y_parts, *x_parts, w_out, g2, router_w)


def _first_argmax(work, index, sentinel):
    m = jnp.max(work, axis=0, keepdims=True)
    return jnp.min(jnp.where(work == m, index, sentinel), axis=0, keepdims=True)


def _route_kernel(logit_ref, bias_ref, eidx_ref, gate_ref, rank_ref, count_ref):
    n_tok = logit_ref.shape[1]

    scores = jax.nn.sigmoid(logit_ref[...])
    biased = scores + bias_ref[...]
    grouped = biased.reshape(N_EXPERT_GROUPS, EXPERTS_PER_GROUP, n_tok)
    in_group = lax.broadcasted_iota(jnp.int32, grouped.shape, 1)
    top1 = jnp.max(grouped, axis=1, keepdims=True)
    first1 = jnp.min(jnp.where(grouped == top1, in_group, EXPERTS_PER_GROUP), axis=1, keepdims=True)
    top2 = jnp.max(jnp.where(in_group == first1, -jnp.inf, grouped), axis=1, keepdims=True)
    group_scores = (top1 + top2).reshape(N_EXPERT_GROUPS, n_tok)

    group_id = lax.broadcasted_iota(jnp.int32, group_scores.shape, 0)
    group_on = jnp.zeros(group_scores.shape, F32)
    work = group_scores
    for _ in range(TOPK_GROUPS):
        pick = group_id == _first_argmax(work, group_id, N_EXPERT_GROUPS)
        group_on = jnp.where(pick, 1.0, group_on)
        work = jnp.where(pick, -jnp.inf, work)

    masked = jnp.where(group_on.reshape(N_EXPERT_GROUPS, 1, n_tok) > 0.0, grouped, -jnp.inf)
    work = masked.reshape(N_EXPERTS, n_tok)
    expert_id = lax.broadcasted_iota(jnp.int32, work.shape, 0)
    ids, sel = [], []
    chosen = jnp.zeros(work.shape, F32)
    for _ in range(TOP_K):
        first = _first_argmax(work, expert_id, N_EXPERTS)
        pick = expert_id == first
        ids.append(first)
        sel.append(jnp.sum(jnp.where(pick, scores, 0.0), axis=0, keepdims=True))
        chosen = jnp.where(pick, 1.0, chosen)
        work = jnp.where(pick, -jnp.inf, work)
    sel = jnp.concatenate(sel, axis=0)
    eidx_ref[...] = jnp.concatenate(ids, axis=0)
    gate_ref[...] = sel / (jnp.sum(sel, axis=0, keepdims=True) + 1e-20) * ROUTED_SCALE

    earlier = lax.broadcasted_iota(jnp.int32, (n_tok, n_tok), 0) < lax.broadcasted_iota(jnp.int32, (n_tok, n_tok), 1)
    before = jnp.dot(chosen.astype(BF16), earlier.astype(BF16), preferred_element_type=F32)
    rank_ref[...] = jnp.concatenate(
        [jnp.sum(jnp.where(expert_id == ids[k], before, 0.0), axis=0, keepdims=True) for k in range(TOP_K)], axis=0
    ).astype(jnp.int32)
    count_ref[0] = jnp.sum(chosen, axis=1, keepdims=True).astype(jnp.int32)


def _route(logits_t, bias, *, tl):
    e, n = logits_t.shape
    pick_spec = pl.BlockSpec((TOP_K, tl), lambda i: (0, i))
    return pl.pallas_call(
        _route_kernel,
        grid=(n // tl,),
        in_specs=[pl.BlockSpec((e, tl), lambda i: (0, i)), pl.BlockSpec((e, 1), lambda i: (0, 0))],
        out_specs=[pick_spec, pick_spec, pick_spec, pl.BlockSpec((1, e, 1), lambda i: (i, 0, 0))],
        out_shape=[
            jax.ShapeDtypeStruct((TOP_K, n), jnp.int32),
            jax.ShapeDtypeStruct((TOP_K, n), F32),
            jax.ShapeDtypeStruct((TOP_K, n), jnp.int32),
            jax.ShapeDtypeStruct((n // tl, e, 1), jnp.int32),
        ],
        compiler_params=_params(("arbitrary",)),
        name="route",
    )(logits_t, bias)


def _gate_lanes(gate):
    g1 = gate.astype(BF16).astype(F32)
    rest = gate - g1
    g2 = rest.astype(BF16).astype(F32)
    g3 = (rest - g2).astype(BF16).astype(F32)
    lane = lax.broadcasted_iota(jnp.int32, (gate.shape[0], LANES), 1)
    return jnp.where(lane == 0, g1, jnp.where(lane == 1, g2, jnp.where(lane == 2, g3, 0.0))).astype(BF16)


def _tab_offsets():
    offsets, col = [], len(CHUNK_ROWS)
    for cap in CHUNK_MAX:
        offsets.append((col, col + cap))
        col += 2 * cap
    return offsets, col


def _segment_copies(tab_ref, block, make_copy):
    offsets, width = _tab_offsets()
    base = block * width
    counts = []
    for k, (rows, (local_col, global_col)) in enumerate(zip(CHUNK_ROWS, offsets)):
        n_chunks = tab_ref[base + k]

        def start_one(j, rows=rows, local_col=local_col, global_col=global_col):
            make_copy(
                pl.multiple_of(tab_ref[base + local_col + j], SEG_ROWS),
                pl.multiple_of(tab_ref[base + global_col + j], SEG_ROWS),
                rows,
            ).start()

        def per_group(g, carry):
            for u in range(ISSUE_UNROLL):
                start_one(g * ISSUE_UNROLL + u)
            return carry

        def per_chunk(j, carry):
            start_one(j)
            return carry

        n_groups = n_chunks // ISSUE_UNROLL
        lax.fori_loop(0, n_groups, per_group, 0)
        lax.fori_loop(n_groups * ISSUE_UNROLL, n_chunks, per_chunk, 0)
        counts.append(n_chunks)
    return counts


def _wait_copies(counts, make_copy):
    for rows, count in zip(CHUNK_ROWS, counts):
        def wait_group(c, carry, rows=rows):
            make_copy(0, 0, rows * WAIT_GROUP).wait()
            return carry

        def wait_one(c, carry, rows=rows):
            make_copy(0, 0, rows).wait()
            return carry

        n_groups = count // WAIT_GROUP
        lax.fori_loop(0, n_groups, wait_group, 0)
        lax.fori_loop(n_groups * WAIT_GROUP, count, wait_one, 0)


def _dispatch_kernel(
    tab_ref, gap_ref, pos_ref, gate_ref, x_ref, xs_hbm, sorted_scr, zero_scr, pending, fill_counts, sem, zero_sem
):
    step = pl.program_id(0)
    slot = step % 2
    d = x_ref.shape[1]
    n_tok = x_ref.shape[0]
    tile_rows = zero_scr.shape[0]

    def zero_chunk(row):
        return pltpu.make_async_copy(
            zero_scr.at[pl.ds(0, SEG_ROWS), :],
            xs_hbm.at[pl.ds(pl.multiple_of(row, SEG_ROWS), SEG_ROWS), :],
            zero_sem,
        )

    def zero_tile(row):
        return pltpu.make_async_copy(
            zero_scr, xs_hbm.at[pl.ds(pl.multiple_of(row, tile_rows), tile_rows), :], zero_sem
        )

    @pl.when(step == 0)
    def _():
        zero_scr[...] = jnp.zeros(zero_scr.shape, BF16)
        n_gaps = N_EXPERTS + 1

        def per_gap(g, totals):
            chunk_start, chunks = gap_ref[g], gap_ref[n_gaps + g]
            tile_start, tiles = gap_ref[2 * n_gaps + g], gap_ref[3 * n_gaps + g]

            def per_chunk(c, carry):
                zero_chunk(chunk_start + c * SEG_ROWS).start()
                return carry

            def per_tile(c, carry):
                zero_tile(tile_start + c * tile_rows).start()
                return carry

            lax.fori_loop(0, chunks, per_chunk, 0)
            lax.fori_loop(0, tiles, per_tile, 0)
            return totals[0] + chunks, totals[1] + tiles

        n_fill, n_tail = lax.fori_loop(0, n_gaps, per_gap, (0, 0))
        fill_counts[0] = n_fill
        fill_counts[1] = n_tail

    pos = pos_ref[...]
    gate = gate_ref[...]
    x = x_ref[...]
    piece_id = lax.broadcasted_iota(jnp.int32, (HOT_PIECE, n_tok), 0)
    pos_k = [jnp.broadcast_to(pos[k:k + 1, :], (HOT_PIECE, n_tok)) for k in range(TOP_K)]
    gate_k = [jnp.broadcast_to(gate[k:k + 1, :], (HOT_PIECE, n_tok)) for k in range(TOP_K)]
    for c in range(LOCAL_ROWS // SORT_CHUNK):
        one_hot, row_gate = [], []
        for p in range(SORT_CHUNK // HOT_PIECE):
            row_id = piece_id + (c * SORT_CHUNK + p * HOT_PIECE)
            hot = jnp.zeros((HOT_PIECE, n_tok), F32)
            gate_hit = jnp.zeros((HOT_PIECE, n_tok), F32)
            for k in range(TOP_K):
                hit = row_id == pos_k[k]
                hot = jnp.where(hit, 1.0, hot)
                gate_hit = jnp.where(hit, gate_k[k], gate_hit)
            one_hot.append(hot.astype(BF16))
            row_gate.append(jnp.sum(gate_hit, axis=1, keepdims=True))
        rows = jnp.dot(jnp.concatenate(one_hot, axis=0), x, preferred_element_type=F32)
        out_rows = pl.ds(c * SORT_CHUNK, SORT_CHUNK)
        sorted_scr[slot, out_rows, pl.ds(0, d)] = rows.astype(BF16)
        sorted_scr[slot, out_rows, pl.ds(d, LANES)] = _gate_lanes(jnp.concatenate(row_gate, axis=0))

    def copy_chunk_from(src_slot):
        def copy_chunk(local_row, global_row, rows):
            return pltpu.make_async_copy(
                sorted_scr.at[src_slot, pl.ds(local_row, rows), :], xs_hbm.at[pl.ds(global_row, rows), :], sem
            )

        return copy_chunk

    copy_chunk = copy_chunk_from(0)

    @pl.when(step > 0)
    def _():
        _wait_copies([pending[k] for k in range(len(CHUNK_ROWS))], copy_chunk)

    for s in range(2):
        @pl.when(slot == s)
        def _(s=s):
            for k, count in enumerate(_segment_copies(tab_ref, step, copy_chunk_from(s))):
                pending[k] = count

    @pl.when(step == pl.num_programs(0) - 1)
    def _():
        def wait_tile(c, carry):
            zero_tile(0).wait()
            return carry

        def wait_chunk(c, carry):
            zero_chunk(0).wait()
            return carry

        _wait_copies([pending[k] for k in range(len(CHUNK_ROWS))], copy_chunk)
        lax.fori_loop(0, fill_counts[0], wait_chunk, 0)
        lax.fori_loop(0, fill_counts[1], wait_tile, 0)


def _dispatch(tab, gaps, pos_t, gate_t, xn, n_sorted, *, tb, zero_rows):
    n, d = xn.shape
    width = d + LANES
    pick_spec = pl.BlockSpec((TOP_K, tb), lambda i, tab, gaps: (0, i))
    return pl.pallas_call(
        _dispatch_kernel,
        grid_spec=pltpu.PrefetchScalarGridSpec(
            num_scalar_prefetch=2,
            grid=(n // tb,),
            in_specs=[pick_spec, pick_spec, pl.BlockSpec((tb, d), lambda i, tab, gaps: (i, 0))],
            out_specs=pl.BlockSpec(memory_space=pl.ANY),
            scratch_shapes=[
                pltpu.VMEM((2, LOCAL_ROWS, width), BF16),
                pltpu.VMEM((zero_rows, width), BF16),
                pltpu.SMEM((len(CHUNK_ROWS),), jnp.int32),
                pltpu.SMEM((2,), jnp.int32),
                pltpu.SemaphoreType.DMA,
                pltpu.SemaphoreType.DMA,
            ],
        ),
        out_shape=jax.ShapeDtypeStruct((n_sorted, width), BF16),
        compiler_params=_params(("arbitrary",)),
        name="dispatch",
    )(tab, gaps, pos_t, gate_t, xn)


def _experts_kernel(
    expert_ref, used_ref, fresh_ref, next_ref, src_ref, xs_ref, wg_hbm, wu_hbm, wd_hbm, ys_ref,
    wg32, wu32, wd32, wg16, wu16, wd16, sems,
):
    it = pl.program_id(0)
    d = ys_ref.shape[1]

    def weight_copies(e):
        return (
            pltpu.make_async_copy(wg_hbm.at[e], wg32, sems.at[0]),
            pltpu.make_async_copy(wu_hbm.at[e], wu32, sems.at[1]),
            pltpu.make_async_copy(wd_hbm.at[e], wd32, sems.at[2]),
        )

    @pl.when(it == 0)
    def _():
        for copy in weight_copies(expert_ref[0]):
            copy.start()

    @pl.when(used_ref[it] == 0)
    def _():
        ys_ref[...] = jnp.zeros(ys_ref.shape, ys_ref.dtype)

    @pl.when((used_ref[it] > 0) & (fresh_ref[it] == 1))
    def _():
        for copy in weight_copies(expert_ref[it]):
            copy.wait()
        wg16[...] = wg32[...].astype(BF16)
        wu16[...] = wu32[...].astype(BF16)
        wd16[...] = wd32[...].astype(BF16)

        @pl.when(next_ref[it] >= 0)
        def _():
            for copy in weight_copies(next_ref[it]):
                copy.start()

    tm = ys_ref.shape[0]
    for n_sub in range(1, tm // EXPERT_SUB + 1):
        rows = n_sub * EXPERT_SUB

        @pl.when(used_ref[it] == n_sub)
        def _(rows=rows):
            x = xs_ref[pl.ds(0, rows), pl.ds(0, d)]
            gate = jnp.sum(xs_ref[pl.ds(0, rows), pl.ds(d, LANES)].astype(F32), axis=1, keepdims=True)
            g = jnp.dot(x, wg16[...], preferred_element_type=F32)
            u = jnp.dot(x, wu16[...], preferred_element_type=F32)
            h = g * jax.nn.sigmoid(g) * u * gate
            y = jnp.dot(h.astype(BF16), wd16[...], preferred_element_type=F32)
            ys_ref[pl.ds(0, rows), :] = y.astype(ys_ref.dtype)
            if rows < tm:
                ys_ref[pl.ds(rows, tm - rows), :] = jnp.zeros((tm - rows, d), ys_ref.dtype)


def _experts(tiles, xs, w_gate, w_up, w_down, *, tm):
    p = xs.shape[0]
    d, f = w_gate.shape[1:]
    row_map = lambda i, expert, used, fresh, nxt, src: (i, 0)
    src_map = lambda i, expert, used, fresh, nxt, src: (src[i], 0)
    hbm = pl.BlockSpec(memory_space=pl.ANY)
    return pl.pallas_call(
        _experts_kernel,
        grid_spec=pltpu.PrefetchScalarGridSpec(
            num_scalar_prefetch=5,
            grid=(p // tm,),
            in_specs=[pl.BlockSpec((tm, xs.shape[1]), src_map), hbm, hbm, hbm],
            out_specs=pl.BlockSpec((tm, d), row_map),
            scratch_shapes=[
                pltpu.VMEM((d, f), F32), pltpu.VMEM((d, f), F32), pltpu.VMEM((f, d), F32),
                pltpu.VMEM((d, f), BF16), pltpu.VMEM((d, f), BF16), pltpu.VMEM((f, d), BF16),
                pltpu.SemaphoreType.DMA((3,)),
            ],
        ),
        out_shape=jax.ShapeDtypeStruct((p, d), BF16),
        compiler_params=_params(("arbitrary",)),
        name="experts",
    )(*tiles, xs, w_gate, w_up, w_down)


def _combine_kernel(tab_ref, pos_ref, x1_ref, xn_ref, sg_ref, su_ref, sd_ref, fg_ref, ys_hbm, *rest, n_blocks):
    o_refs = rest[:len(n_blocks)]
    local_scr, pending, sems = rest[len(n_blocks):]
    n_tok = x1_ref.shape[0]
    step = pl.program_id(0)
    slot = step % 2

    n_sizes = len(CHUNK_ROWS)

    def copy_chunk_into(dst_slot):
        def copy_chunk(local_row, global_row, rows):
            return pltpu.make_async_copy(
                ys_hbm.at[pl.ds(global_row, rows), :],
                local_scr.at[dst_slot, pl.ds(local_row, rows), :],
                sems.at[dst_slot],
            )

        return copy_chunk

    def fetch(block, dst_slot):
        for k, count in enumerate(_segment_copies(tab_ref, block, copy_chunk_into(dst_slot))):
            pending[dst_slot * n_sizes + k] = count

    @pl.when(step == 0)
    def _():
        local_scr[...] = jnp.zeros(local_scr.shape, local_scr.dtype)
        fetch(0, 0)

    for s in range(2):
        @pl.when((step + 1 < pl.num_programs(0)) & (slot == s))
        def _(s=s):
            fetch(step + 1, 1 - s)

    xn = xn_ref[...]
    sg = jnp.dot(xn, sg_ref[...], preferred_element_type=F32)
    su = jnp.dot(xn, su_ref[...], preferred_element_type=F32)
    acc = x1_ref[...] + jnp.dot(
        (sg * jax.nn.sigmoid(sg) * su).astype(BF16), sd_ref[...], preferred_element_type=F32
    )

    piece_id = lax.broadcasted_iota(jnp.int32, (HOT_PIECE, LANES), 1)
    one_hot = []
    for r in range(n_tok // HOT_PIECE):
        pos = pos_ref[pl.ds(r * HOT_PIECE, HOT_PIECE), :]
        pos_k = [jnp.broadcast_to(pos[:, k:k + 1], (HOT_PIECE, LANES)) for k in range(TOP_K)]
        cols = []
        for c in range(LOCAL_ROWS // LANES):
            col_id = piece_id + c * LANES
            hot = jnp.zeros((HOT_PIECE, LANES), F32)
            for k in range(TOP_K):
                hot = jnp.where(col_id == pos_k[k], 1.0, hot)
            cols.append(hot.astype(BF16))
        one_hot.append(jnp.concatenate(cols, axis=1))
    one_hot = jnp.concatenate(one_hot, axis=0)

    _wait_copies([pending[slot * n_sizes + k] for k in range(n_sizes)], copy_chunk_into(slot))
    routed = jnp.dot(one_hot, local_scr[slot], preferred_element_type=F32)
    out = _rms_norm(acc + routed, fg_ref[...])

    def body(p):
        o_refs[p][...] = out

    _for_part(pl.program_id(0), n_blocks, body)


def _combine(tab, pos, x1, xn, sh_gate, sh_up, sh_down, final_g, ys, part_rows, *, tb):
    n, d = x1.shape
    f = sh_gate.shape[1]
    n_blocks = tuple(r // tb for r in part_rows)
    const = lambda i, tab: (0, 0)
    row = lambda i, tab: (i, 0)
    return pl.pallas_call(
        functools.partial(_combine_kernel, n_blocks=n_blocks),
        grid_spec=pltpu.PrefetchScalarGridSpec(
            num_scalar_prefetch=1,
            grid=(n // tb,),
            in_specs=[
                pl.BlockSpec((tb, TOP_K), row),
                pl.BlockSpec((tb, d), row),
                pl.BlockSpec((tb, d), row),
                pl.BlockSpec((d, f), const, pipeline_mode=pl.Buffered(1)),
                pl.BlockSpec((d, f), const, pipeline_mode=pl.Buffered(1)),
                pl.BlockSpec((f, d), const, pipeline_mode=pl.Buffered(1)),
                pl.BlockSpec((1, d), const),
                pl.BlockSpec(memory_space=pl.ANY),
            ],
            out_specs=[
                pl.BlockSpec((tb, d), _part_map(start, nb, 0))
                for start, nb in zip(_part_starts(n_blocks), n_blocks)
            ],
            scratch_shapes=[
                pltpu.VMEM((2, LOCAL_ROWS, d), BF16),
                pltpu.SMEM((2 * len(CHUNK_ROWS),), jnp.int32),
                pltpu.SemaphoreType.DMA((2,)),
            ],
        ),
        out_shape=[jax.ShapeDtypeStruct((r, d), F32) for r in part_rows],
        compiler_params=_params(("arbitrary",)),
        name="combine",
    )(tab, pos, x1, xn, sh_gate, sh_up, sh_down, final_g, ys)


def _segment_tables(block_counts, n_rows, *, tm):
    padded = (block_counts + SEG_ROWS - 1) // SEG_ROWS * SEG_ROWS
    local_start = jnp.cumsum(padded, axis=1) - padded
    per_expert = jnp.sum(padded, axis=0)
    region = (per_expert + tm - 1) // tm * tm
    region_start = jnp.concatenate([jnp.zeros((1,), jnp.int32), jnp.cumsum(region).astype(jnp.int32)])
    global_start = region_start[None, :N_EXPERTS] + jnp.cumsum(padded, axis=0) - padded
    counts, lists = [], []
    done = jnp.zeros_like(padded)
    for rows, cap in zip(CHUNK_ROWS, CHUNK_MAX):
        per_seg = (padded - done) // rows
        seg_end = jnp.cumsum(per_seg, axis=1)
        j = jnp.arange(cap, dtype=jnp.int32)
        seg_of_chunk = jnp.sum((seg_end[:, None, :] <= j[None, :, None]).astype(jnp.int32), axis=2)
        in_seg = seg_of_chunk[:, :, None] == jnp.arange(N_EXPERTS, dtype=jnp.int32)
        first = done - (seg_end - per_seg) * rows
        for start in (local_start, global_start):
            lists.append(jnp.sum(jnp.where(in_seg, (start + first)[:, None, :], 0), axis=2) + j[None, :] * rows)
        counts.append(seg_end[:, N_EXPERTS - 1:])
        done = done + per_seg * rows
    tab = jnp.concatenate(counts + lists, axis=1).astype(jnp.int32).reshape(-1)

    region_end = region_start[:N_EXPERTS] + per_expert
    gap_start = jnp.concatenate([region_end, region_start[N_EXPERTS:]])
    gap_end = jnp.concatenate([region_start[1:], jnp.full((1,), n_rows, jnp.int32)])
    sub_start = jnp.minimum((gap_start + EXPERT_SUB - 1) // EXPERT_SUB * EXPERT_SUB, gap_end)
    gaps = jnp.concatenate(
        [gap_start, (sub_start - gap_start) // SEG_ROWS, sub_start, (gap_end - sub_start) // EXPERT_SUB]
    ).astype(jnp.int32)

    tile_start = jnp.arange(n_rows // tm, dtype=jnp.int32) * tm
    expert = jnp.clip(
        jnp.sum((region_start[None, :] <= tile_start[:, None]).astype(jnp.int32), axis=1) - 1, 0, N_EXPERTS - 1
    )
    tile_end = jnp.sum(jnp.where(expert[:, None] == jnp.arange(N_EXPERTS), region_end[None, :], 0), axis=1)
    used = jnp.clip((tile_end - tile_start + EXPERT_SUB - 1) // EXPERT_SUB, 0, tm // EXPERT_SUB).astype(jnp.int32)
    fresh = jnp.concatenate([jnp.ones((1,), jnp.int32), (expert[1:] != expert[:-1]).astype(jnp.int32)])
    experts = jnp.arange(N_EXPERTS, dtype=jnp.int32)
    later = (experts[None, :] > expert[:, None]) & (per_expert[None, :] > 0)
    nxt = jnp.min(jnp.where(later, experts[None, :], N_EXPERTS), axis=1)
    nxt = jnp.where(nxt < N_EXPERTS, nxt, -1).astype(jnp.int32)
    tile_id = jnp.arange(n_rows // tm, dtype=jnp.int32)
    src = jnp.minimum(tile_id, jnp.max(jnp.where(used > 0, tile_id, 0)))
    return tab, gaps, (expert, used, fresh, nxt, src), local_start.astype(jnp.int32)


ROW_TM = 512
OUT_PROJ_TM = 256
IN_PROJ_TN = 2048
PROMPT_T = 256
SAMPLE_SEQS = 16
EXPERT_TM = 512
EXPERT_SUB = 128


def kernel(x_prompt, x_sample, state_lru_conv, state_lru_h, state_conf_conv, meta_tokens, norm1_g, w_in, lru_conv_w, lru_conv_b, lru_wa, lru_ba, lru_wi, lru_bi, lru_lambda, conf_conv_w, conf_conv_b, conf_ln_g, conf_ln_b, out_norm_a, out_norm_b, w_out, norm2_g, router_w, router_bias, exp_w_gate, exp_w_up, exp_w_down, sh_w_gate, sh_w_up, sh_w_down, final_norm_g):
    b_p, seq, d = x_prompt.shape
    b_s, t_s, _ = x_sample.shape
    n_p = b_p * seq
    n_s = b_s * t_s
    n = n_p + n_s
    x_parts = (x_prompt.reshape(n_p, d), x_sample.reshape(n_s, d))

    row = lambda v: v.reshape(1, -1)
    mixer_w = (
        lru_conv_w[0], row(lru_conv_b[0]),
        lru_wa[0], row(lru_ba[0]),
        lru_wi[0], row(lru_bi[0]),
        row(lru_lambda[0]),
        conf_conv_w[0], row(conf_conv_b[0]), row(conf_ln_g[0]), row(conf_ln_b[0]),
        row(out_norm_a[0]), row(out_norm_b[0]),
    )

    proj, proj_m = _in_proj(x_parts, meta_tokens, row(norm1_g[0]), w_in[0], tm=ROW_TM, tn=IN_PROJ_TN)
    _, m_lc, m_h, m_cc = _mixer(
        proj_m, 0,
        jnp.zeros((1, 1, LRU_CONV - 1, W_A), F32), jnp.zeros((1, 1, W_A), F32),
        jnp.zeros((1, 1, CONF_KERNEL - 1, W_B), F32),
        mixer_w, n_seq=1, n_t=N_META, n_tiles=1,
    )

    y_p, p_lc, p_h, p_cc = _mixer(
        proj, 0,
        jnp.broadcast_to(m_lc, (1, b_p) + m_lc.shape[2:]), jnp.broadcast_to(m_h, (b_p,) + m_h.shape[1:]),
        jnp.broadcast_to(m_cc, (1, b_p) + m_cc.shape[2:]),
        mixer_w, n_seq=1, n_t=PROMPT_T, n_tiles=seq // PROMPT_T,
    )
    y_s, s_lc, s_h, s_cc = _mixer(
        proj, n_p // (SAMPLE_SEQS * t_s),
        state_lru_conv, state_lru_h[0].reshape(b_s, 1, W_A), state_conf_conv,
        mixer_w, n_seq=SAMPLE_SEQS, n_t=t_s, n_tiles=1,
    )

    x1, xn2, logits = _out_proj(
        (y_p, y_s), x_parts, w_out[0], row(norm2_g[0]), router_w[0].astype(BF16), tm=OUT_PROJ_TM
    )

    eidx_t, gate_t, rank_t, block_counts = _route(logits.T, router_bias[0].reshape(N_EXPERTS, 1), tl=TOKEN_BLOCK)
    n_blocks = n // TOKEN_BLOCK
    n_sorted = n_blocks * (TOKEN_BLOCK * TOP_K + N_EXPERTS * (SEG_ROWS - 1)) + N_EXPERTS * (EXPERT_TM - 1)
    n_sorted = -(-n_sorted // EXPERT_TM) * EXPERT_TM
    tab, gaps, tiles, local_start = _segment_tables(block_counts[:, :, 0], n_sorted, tm=EXPERT_TM)
    expert_one_hot = eidx_t[:, :, None] == jnp.arange(N_EXPERTS, dtype=jnp.int32)
    token_start = jnp.repeat(local_start, TOKEN_BLOCK, axis=0)
    pos_t = jnp.sum(jnp.where(expert_one_hot, token_start[None], 0), axis=2) + rank_t

    xs = _dispatch(tab, gaps, pos_t, gate_t, xn2, n_sorted, tb=TOKEN_BLOCK, zero_rows=EXPERT_SUB)
    ys = _experts(tiles, xs, exp_w_gate[0], exp_w_up[0], exp_w_down[0], tm=EXPERT_TM)
    out_p, out_s = _combine(
        tab, pos_t.T, x1, xn2,
        sh_w_gate[0].astype(BF16), sh_w_up[0].astype(BF16), sh_w_down[0].astype(BF16),
        row(final_norm_g), ys, (n_p, n_s), tb=TOKEN_BLOCK,
    )

    return (
        out_p.reshape(b_p, seq, d), out_s.reshape(b_s, t_s, d),
        p_lc, p_h.reshape(1, b_p, W_A), p_cc,
        s_lc, s_h.reshape(1, b_s, W_A), s_cc,
    )
```

```python
import functools

import jax
import jax.numpy as jnp
from jax import lax
from jax.experimental import pallas as pl
from jax.experimental.pallas import tpu as pltpu

D_MODEL = 2048
N_META = 16
W_A = 1024
W_B = 1024
LRU_CONV = 4
LRU_C = 8.0
CONF_KERNEL = 31
N_EXPERTS = 64
N_EXPERT_GROUPS = 8
EXPERTS_PER_GROUP = 8
TOPK_GROUPS = 4
TOP_K = 8
ROUTED_SCALE = 2.5
EPS = 1e-6

SUBLANES = 8
LANES = 128
TOKEN_BLOCK = 256
SEG_ROWS = SUBLANES
LOCAL_ROWS = 2560
SORT_CHUNK = 128
HOT_PIECE = 16
CHUNK_ROWS = (SEG_ROWS,)
CHUNK_MAX = (LOCAL_ROWS // SEG_ROWS,)
ISSUE_UNROLL = 16
WAIT_GROUP = 32
assert LOCAL_ROWS >= TOKEN_BLOCK * TOP_K + N_EXPERTS * (SEG_ROWS - 1) and LOCAL_ROWS % SORT_CHUNK == 0
LRU_HIST = SUBLANES
CONF_HIST = 32
VMEM_LIMIT = 56 * 1024 * 1024

F32 = jnp.float32
BF16 = jnp.bfloat16


def _params(semantics):
    return pltpu.CompilerParams(dimension_semantics=semantics, vmem_limit_bytes=VMEM_LIMIT)


def _rms_norm(x, g):
    return x * lax.rsqrt(jnp.mean(x * x, axis=-1, keepdims=True) + EPS) * g


def _part_starts(n_blocks):
    starts, s = [], 0
    for nb in n_blocks:
        starts.append(s)
        s += nb
    return starts


def _part_map(start, nb, grid_axis):
    def index_map(*ids):
        return (jnp.clip(ids[grid_axis] - start, 0, nb - 1), 0)

    return index_map


def _for_part(i, n_blocks, body):
    for p, (start, nb) in enumerate(zip(_part_starts(n_blocks), n_blocks)):
        pl.when((i >= start) & (i < start + nb))(functools.partial(body, p))


def _in_proj_kernel(*refs, n_blocks):
    x_refs = refs[:len(n_blocks)]
    meta_ref, g_ref, w_ref, o_ref, meta_o_ref, w16 = refs[len(n_blocks):]

    def project(x):
        return jnp.dot(_rms_norm(x, g_ref[...]).astype(BF16), w16[...], preferred_element_type=F32)

    @pl.when(pl.program_id(1) == 0)
    def _():
        w16[...] = w_ref[...].astype(BF16)
        meta_o_ref[...] = project(meta_ref[...])

    def body(p):
        o_ref[...] = project(x_refs[p][...])

    _for_part(pl.program_id(1), n_blocks, body)


def _in_proj(x_parts, meta, g, w, *, tm, tn):
    d, f = w.shape
    n_meta = meta.shape[0]
    n_blocks = tuple(x.shape[0] // tm for x in x_parts)
    x_specs = [
        pl.BlockSpec((tm, d), _part_map(start, nb, 1)) for start, nb in zip(_part_starts(n_blocks), n_blocks)
    ]
    return pl.pallas_call(
        functools.partial(_in_proj_kernel, n_blocks=n_blocks),
        grid=(f // tn, sum(n_blocks)),
        in_specs=x_specs + [
            pl.BlockSpec((n_meta, d), lambda j, i: (0, 0)),
            pl.BlockSpec((1, d), lambda j, i: (0, 0)),
            pl.BlockSpec((d, tn), lambda j, i: (0, j), pipeline_mode=pl.Buffered(1)),
        ],
        out_specs=[pl.BlockSpec((tm, tn), lambda j, i: (i, j)), pl.BlockSpec((n_meta, tn), lambda j, i: (0, j))],
        out_shape=[
            jax.ShapeDtypeStruct((sum(n_blocks) * tm, f), F32),
            jax.ShapeDtypeStruct((n_meta, f), F32),
        ],
        scratch_shapes=[pltpu.VMEM((d, tn), BF16)],
        compiler_params=_params(("arbitrary", "arbitrary")),
        name="in_proj",
    )(*x_parts, meta, g, w)


def _causal_conv(ext, w_ref, n_t, hist):
    taps = w_ref.shape[0]
    length = ext.shape[1]
    shifted = {0: ext}
    acc = None
    for j in range(taps):
        q, r = divmod(hist - (taps - 1) + j, SUBLANES)
        if r not in shifted:
            shifted[r] = pltpu.roll(ext, length - r, 1)
        term = w_ref[pl.ds(j, 1), :][None] * shifted[r][:, q * SUBLANES:q * SUBLANES + n_t, :]
        acc = term if acc is None else acc + term
    return acc


def _mixer_kernel(
    proj_ref, lc_ref, h0_ref, cc_ref,
    wca_ref, bca_ref, wa_ref, ba_ref, wi_ref, bi_ref, lam_ref,
    wcb_ref, bcb_ref, lng_ref, lnb_ref, nag_ref, nbg_ref,
    y_ref, lc_out, h_out, cc_out,
    ua_ext, glu_ext, hst, a_scr, u_scr, h_scr, wa_bd, wi_bd,
    *, n_seq, n_t,
):
    t = pl.program_id(1)
    rows = n_seq * n_t
    c = W_A

    @pl.when((pl.program_id(0) == 0) & (t == 0))
    def _():
        n_heads, head_dim, _ = wa_ref.shape
        for w_ref, dense in ((wa_ref, wa_bd), (wi_ref, wi_bd)):
            dense[...] = jnp.zeros(dense.shape, dense.dtype)
            for h in range(n_heads):
                block = pl.ds(h * head_dim, head_dim)
                dense[block, block] = w_ref[h].astype(dense.dtype)

    @pl.when(t == 0)
    def _():
        ua_ext[:, pl.ds(0, LRU_HIST), :] = jnp.zeros((n_seq, LRU_HIST, c), F32)
        ua_ext[:, pl.ds(LRU_HIST - (LRU_CONV - 1), LRU_CONV - 1), :] = lc_ref[...]
        glu_ext[:, pl.ds(0, CONF_HIST), :] = jnp.zeros((n_seq, CONF_HIST, c), F32)
        glu_ext[:, pl.ds(CONF_HIST - (CONF_KERNEL - 1), CONF_KERNEL - 1), :] = cc_ref[...]
        hst[...] = h0_ref[...]

    ua_ext[:, pl.ds(LRU_HIST, n_t), :] = proj_ref[:, pl.ds(0, c)].reshape(n_seq, n_t, c)
    c_a = _causal_conv(ua_ext[...], wca_ref, n_t, LRU_HIST)
    c_a = (c_a + bca_ref[...][None]).reshape(rows, c)
    c_a16 = c_a.astype(BF16)
    r = jax.nn.sigmoid(jnp.dot(c_a16, wa_bd[...], preferred_element_type=F32) + ba_ref[...])
    i = jax.nn.sigmoid(jnp.dot(c_a16, wi_bd[...], preferred_element_type=F32) + bi_ref[...])
    neg_lam = -lam_ref[...]
    softplus = jnp.maximum(neg_lam, 0.0) + jnp.log1p(jnp.exp(-jnp.abs(neg_lam)))
    log_a = (-LRU_C * r) * softplus
    a = jnp.exp(log_a)
    a_scr[...] = a
    u_scr[...] = jnp.sqrt(-jnp.tanh(log_a) * (a * a + 1.0)) * (i * c_a)

    row8 = lax.broadcasted_iota(jnp.int32, (SUBLANES, c), 0)
    groups = n_t // SUBLANES
    for s in range(n_seq):
        def scan_group(g, carry, s=s):
            off = pl.multiple_of(s * n_t + g * SUBLANES, SUBLANES)
            a = a_scr[pl.ds(off, SUBLANES), :]
            u = u_scr[pl.ds(off, SUBLANES), :]
            for d in (1, 2, 4):
                keep = row8 >= d
                a_sh = jnp.where(keep, pltpu.roll(a, d, 0), 1.0)
                u_sh = jnp.where(keep, pltpu.roll(u, d, 0), 0.0)
                u = a * u_sh + u
                a = a * a_sh
            h = a * carry + u
            h_scr[pl.ds(off, SUBLANES), :] = h
            return jnp.broadcast_to(h[SUBLANES - 1:SUBLANES, :], (SUBLANES, c))

        carry = lax.fori_loop(0, groups, scan_group, jnp.broadcast_to(hst[s], (SUBLANES, c)))
        hst[s] = carry[0:1, :]

    y_a = jax.nn.gelu(proj_ref[:, pl.ds(c, c)]) * h_scr[...]
    y_ref[:, pl.ds(0, c)] = _rms_norm(y_a, nag_ref[...]).astype(y_ref.dtype)

    glu = proj_ref[:, pl.ds(2 * c, c)] * jax.nn.sigmoid(proj_ref[:, pl.ds(3 * c, c)])
    glu_ext[:, pl.ds(CONF_HIST, n_t), :] = glu.reshape(n_seq, n_t, c)
    c_b = _causal_conv(glu_ext[...], wcb_ref, n_t, CONF_HIST)
    c_b = (c_b + bcb_ref[...][None]).reshape(rows, c)
    mu = jnp.mean(c_b, axis=-1, keepdims=True)
    cen = c_b - mu
    var = jnp.mean(cen * cen, axis=-1, keepdims=True)
    ln = cen * lax.rsqrt(var + EPS) * lng_ref[...] + lnb_ref[...]
    y_b = ln * jax.nn.sigmoid(ln)
    y_ref[:, pl.ds(c, c)] = _rms_norm(y_b, nbg_ref[...]).astype(y_ref.dtype)

    lc_out[...] = ua_ext[:, pl.ds(LRU_HIST + n_t - (LRU_CONV - 1), LRU_CONV - 1), :]
    cc_out[...] = glu_ext[:, pl.ds(CONF_HIST + n_t - (CONF_KERNEL - 1), CONF_KERNEL - 1), :]
    h_out[...] = hst[...]
    ua_ext[:, pl.ds(0, LRU_HIST), :] = ua_ext[:, pl.ds(n_t, LRU_HIST), :]
    glu_ext[:, pl.ds(0, CONF_HIST), :] = glu_ext[:, pl.ds(n_t, CONF_HIST), :]


def _mixer(proj, row_block0, lc, h0, cc, weights, *, n_seq, n_t, n_tiles):
    b = lc.shape[1]
    c = W_A
    rows = n_seq * n_t
    n_sb = b // n_seq

    def row_map(sb, t):
        return (row_block0 + sb * n_tiles + t, 0)

    def seq_map(sb, t):
        return (sb, 0, 0)

    def depth_seq_map(sb, t):
        return (0, sb, 0, 0)

    w_specs = [pl.BlockSpec(w.shape, (lambda sb, t, nd=w.ndim: (0,) * nd)) for w in weights]
    lc_spec = pl.BlockSpec((None, n_seq, LRU_CONV - 1, c), depth_seq_map)
    cc_spec = pl.BlockSpec((None, n_seq, CONF_KERNEL - 1, c), depth_seq_map)
    return pl.pallas_call(
        functools.partial(_mixer_kernel, n_seq=n_seq, n_t=n_t),
        grid=(n_sb, n_tiles),
        in_specs=[
            pl.BlockSpec((rows, 4 * c), row_map),
            lc_spec,
            pl.BlockSpec((n_seq, 1, c), seq_map),
            cc_spec,
        ] + w_specs,
        out_specs=[
            pl.BlockSpec((rows, 2 * c), lambda sb, t: (sb * n_tiles + t, 0)),
            lc_spec,
            pl.BlockSpec((n_seq, 1, c), seq_map),
            cc_spec,
        ],
        out_shape=[
            jax.ShapeDtypeStruct((b * n_tiles * n_t, 2 * c), BF16),
            jax.ShapeDtypeStruct((1, b, LRU_CONV - 1, c), F32),
            jax.ShapeDtypeStruct((b, 1, c), F32),
            jax.ShapeDtypeStruct((1, b, CONF_KERNEL - 1, c), F32),
        ],
        scratch_shapes=[
            pltpu.VMEM((n_seq, LRU_HIST + n_t, c), F32),
            pltpu.VMEM((n_seq, CONF_HIST + n_t, c), F32),
            pltpu.VMEM((n_seq, 1, c), F32),
            pltpu.VMEM((rows, c), F32),
            pltpu.VMEM((rows, c), F32),
            pltpu.VMEM((rows, c), F32),
            pltpu.VMEM((c, c), BF16),
            pltpu.VMEM((c, c), BF16),
        ],
        compiler_params=_params(("arbitrary", "arbitrary")),
        name="mixer",
    )(proj, lc, h0, cc, *weights)


def _out_proj_kernel(*refs, n_blocks):
    k = len(n_blocks)
    y_refs, x_refs = refs[:k], refs[k:2 * k]
    w_ref, g_ref, rw_ref, x1_ref, xn_ref, logit_ref, w16 = refs[2 * k:]

    @pl.when(pl.program_id(0) == 0)
    def _():
        w16[...] = w_ref[...].astype(BF16)

    def body(p):
        x1 = x_refs[p][...] + jnp.dot(y_refs[p][...], w16[...], preferred_element_type=F32)
        x1_ref[...] = x1
        xn = _rms_norm(x1, g_ref[...]).astype(BF16)
        xn_ref[...] = xn
        logit_ref[...] = jnp.dot(xn, rw_ref[...], preferred_element_type=F32)

    _for_part(pl.program_id(0), n_blocks, body)


def _out_proj(y_parts, x_parts, w_out, g2, router_w, *, tm):
    d = w_out.shape[0]
    e = router_w.shape[1]
    n_blocks = tuple(x.shape[0] // tm for x in x_parts)
    part_specs = [
        pl.BlockSpec((tm, d), _part_map(start, nb, 0)) for start, nb in zip(_part_starts(n_blocks), n_blocks)
    ]
    n = sum(n_blocks) * tm
    const = lambda i: (0, 0)
    row = lambda i: (i, 0)
    return pl.pallas_call(
        functools.partial(_out_proj_kernel, n_blocks=n_blocks),
        grid=(sum(n_blocks),),
        in_specs=part_specs + part_specs + [
            pl.BlockSpec((d, d), const, pipeline_mode=pl.Buffered(1)),
            pl.BlockSpec((1, d), const),
            pl.BlockSpec((d, e), const),
        ],
        out_specs=[pl.BlockSpec((tm, d), row), pl.BlockSpec((tm, d), row), pl.BlockSpec((tm, e), row)],
        out_shape=[
            jax.ShapeDtypeStruct((n, d), F32),
            jax.ShapeDtypeStruct((n, d), BF16),
            jax.ShapeDtypeStruct((n, e), F32),
        ],
        scratch_shapes=[pltpu.VMEM((d, d), BF16)],
        compiler_params=_params(("arbitrary",)),
        name="out_proj",
    )(*y_parts, *x_parts, w_out, g2, router_w)


def _first_argmax(work, index, sentinel):
    m = jnp.max(work, axis=0, keepdims=True)
    return jnp.min(jnp.where(work == m, index, sentinel), axis=0, keepdims=True)


def _route_kernel(logit_ref, bias_ref, eidx_ref, gate_ref, rank_ref, count_ref):
    n_tok = logit_ref.shape[1]

    scores = jax.nn.sigmoid(logit_ref[...])
    biased = scores + bias_ref[...]
    grouped = biased.reshape(N_EXPERT_GROUPS, EXPERTS_PER_GROUP, n_tok)
    in_group = lax.broadcasted_iota(jnp.int32, grouped.shape, 1)
    top1 = jnp.max(grouped, axis=1, keepdims=True)
    first1 = jnp.min(jnp.where(grouped == top1, in_group, EXPERTS_PER_GROUP), axis=1, keepdims=True)
    top2 = jnp.max(jnp.where(in_group == first1, -jnp.inf, grouped), axis=1, keepdims=True)
    group_scores = (top1 + top2).reshape(N_EXPERT_GROUPS, n_tok)

    group_id = lax.broadcasted_iota(jnp.int32, group_scores.shape, 0)
    group_on = jnp.zeros(group_scores.shape, F32)
    work = group_scores
    for _ in range(TOPK_GROUPS):
        pick = group_id == _first_argmax(work, group_id, N_EXPERT_GROUPS)
        group_on = jnp.where(pick, 1.0, group_on)
        work = jnp.where(pick, -jnp.inf, work)

    masked = jnp.where(group_on.reshape(N_EXPERT_GROUPS, 1, n_tok) > 0.0, grouped, -jnp.inf)
    work = masked.reshape(N_EXPERTS, n_tok)
    expert_id = lax.broadcasted_iota(jnp.int32, work.shape, 0)
    ids, sel = [], []
    chosen = jnp.zeros(work.shape, F32)
    for _ in range(TOP_K):
        first = _first_argmax(work, expert_id, N_EXPERTS)
        pick = expert_id == first
        ids.append(first)
        sel.append(jnp.sum(jnp.where(pick, scores, 0.0), axis=0, keepdims=True))
        chosen = jnp.where(pick, 1.0, chosen)
        work = jnp.where(pick, -jnp.inf, work)
    sel = jnp.concatenate(sel, axis=0)
    eidx_ref[...] = jnp.concatenate(ids, axis=0)
    gate_ref[...] = sel / (jnp.sum(sel, axis=0, keepdims=True) + 1e-20) * ROUTED_SCALE

    earlier = lax.broadcasted_iota(jnp.int32, (n_tok, n_tok), 0) < lax.broadcasted_iota(jnp.int32, (n_tok, n_tok), 1)
    before = jnp.dot(chosen.astype(BF16), earlier.astype(BF16), preferred_element_type=F32)
    rank_ref[...] = jnp.concatenate(
        [jnp.sum(jnp.where(expert_id == ids[k], before, 0.0), axis=0, keepdims=True) for k in range(TOP_K)], axis=0
    ).astype(jnp.int32)
    count_ref[0] = jnp.sum(chosen, axis=1, keepdims=True).astype(jnp.int32)


def _route(logits_t, bias, *, tl):
    e, n = logits_t.shape
    pick_spec = pl.BlockSpec((TOP_K, tl), lambda i: (0, i))
    return pl.pallas_call(
        _route_kernel,
        grid=(n // tl,),
        in_specs=[pl.BlockSpec((e, tl), lambda i: (0, i)), pl.BlockSpec((e, 1), lambda i: (0, 0))],
        out_specs=[pick_spec, pick_spec, pick_spec, pl.BlockSpec((1, e, 1), lambda i: (i, 0, 0))],
        out_shape=[
            jax.ShapeDtypeStruct((TOP_K, n), jnp.int32),
            jax.ShapeDtypeStruct((TOP_K, n), F32),
            jax.ShapeDtypeStruct((TOP_K, n), jnp.int32),
            jax.ShapeDtypeStruct((n // tl, e, 1), jnp.int32),
        ],
        compiler_params=_params(("arbitrary",)),
        name="route",
    )(logits_t, bias)


def _gate_lanes(gate):
    g1 = gate.astype(BF16).astype(F32)
    rest = gate - g1
    g2 = rest.astype(BF16).astype(F32)
    g3 = (rest - g2).astype(BF16).astype(F32)
    lane = lax.broadcasted_iota(jnp.int32, (gate.shape[0], LANES), 1)
    return jnp.where(lane == 0, g1, jnp.where(lane == 1, g2, jnp.where(lane == 2, g3, 0.0))).astype(BF16)


def _tab_offsets():
    offsets, col = [], len(CHUNK_ROWS)
    for cap in CHUNK_MAX:
        offsets.append((col, col + cap))
        col += 2 * cap
    return offsets, col


def _segment_copies(tab_ref, block, make_copy):
    offsets, width = _tab_offsets()
    base = block * width
    counts = []
    for k, (rows, (local_col, global_col)) in enumerate(zip(CHUNK_ROWS, offsets)):
        n_chunks = tab_ref[base + k]

        def start_one(j, priority, rows=rows, local_col=local_col, global_col=global_col):
            make_copy(
                pl.multiple_of(tab_ref[base + local_col + j], SEG_ROWS),
                pl.multiple_of(tab_ref[base + global_col + j], SEG_ROWS),
                rows,
            ).start(priority=priority)

        def per_group(g, carry):
            for u in range(ISSUE_UNROLL):
                start_one(g * ISSUE_UNROLL + u, u % 2)
            return carry

        def per_chunk(j, carry):
            start_one(j, 0)
            return carry

        n_groups = n_chunks // ISSUE_UNROLL
        lax.fori_loop(0, n_groups, per_group, 0)
        lax.fori_loop(n_groups * ISSUE_UNROLL, n_chunks, per_chunk, 0)
        counts.append(n_chunks)
    return counts


def _wait_copies(counts, make_copy):
    for rows, count in zip(CHUNK_ROWS, counts):
        def wait_group(c, carry, rows=rows):
            make_copy(0, 0, rows * WAIT_GROUP).wait()
            return carry

        def wait_one(c, carry, rows=rows):
            make_copy(0, 0, rows).wait()
            return carry

        n_groups = count // WAIT_GROUP
        lax.fori_loop(0, n_groups, wait_group, 0)
        lax.fori_loop(n_groups * WAIT_GROUP, count, wait_one, 0)


def _dispatch_kernel(
    tab_ref, gap_ref, pos_ref, gate_ref, x_ref, xs_hbm, sorted_scr, zero_scr, pending, fill_counts, sem, zero_sem
):
    step = pl.program_id(0)
    slot = step % 2
    d = x_ref.shape[1]
    n_tok = x_ref.shape[0]
    tile_rows = zero_scr.shape[0]

    def zero_chunk(row):
        return pltpu.make_async_copy(
            zero_scr.at[pl.ds(0, SEG_ROWS), :],
            xs_hbm.at[pl.ds(pl.multiple_of(row, SEG_ROWS), SEG_ROWS), :],
            zero_sem,
        )

    def zero_tile(row):
        return pltpu.make_async_copy(
            zero_scr, xs_hbm.at[pl.ds(pl.multiple_of(row, tile_rows), tile_rows), :], zero_sem
        )

    @pl.when(step == 0)
    def _():
        zero_scr[...] = jnp.zeros(zero_scr.shape, BF16)
        n_gaps = N_EXPERTS + 1

        def per_gap(g, totals):
            chunk_start, chunks = gap_ref[g], gap_ref[n_gaps + g]
            tile_start, tiles = gap_ref[2 * n_gaps + g], gap_ref[3 * n_gaps + g]

            def per_chunk(c, carry):
                zero_chunk(chunk_start + c * SEG_ROWS).start()
                return carry

            def per_tile(c, carry):
                zero_tile(tile_start + c * tile_rows).start()
                return carry

            lax.fori_loop(0, chunks, per_chunk, 0)
            lax.fori_loop(0, tiles, per_tile, 0)
            return totals[0] + chunks, totals[1] + tiles

        n_fill, n_tail = lax.fori_loop(0, n_gaps, per_gap, (0, 0))
        fill_counts[0] = n_fill
        fill_counts[1] = n_tail

    pos = pos_ref[...]
    gate = gate_ref[...]
    x = x_ref[...]
    piece_id = lax.broadcasted_iota(jnp.int32, (HOT_PIECE, n_tok), 0)
    pos_k = [jnp.broadcast_to(pos[k:k + 1, :], (HOT_PIECE, n_tok)) for k in range(TOP_K)]
    gate_k = [jnp.broadcast_to(gate[k:k + 1, :], (HOT_PIECE, n_tok)) for k in range(TOP_K)]
    for c in range(LOCAL_ROWS // SORT_CHUNK):
        one_hot, row_gate = [], []
        for p in range(SORT_CHUNK // HOT_PIECE):
            row_id = piece_id + (c * SORT_CHUNK + p * HOT_PIECE)
            hot = jnp.zeros((HOT_PIECE, n_tok), F32)
            gate_hit = jnp.zeros((HOT_PIECE, n_tok), F32)
            for k in range(TOP_K):
                hit = row_id == pos_k[k]
                hot = jnp.where(hit, 1.0, hot)
                gate_hit = jnp.where(hit, gate_k[k], gate_hit)
            one_hot.append(hot.astype(BF16))
            row_gate.append(jnp.sum(gate_hit, axis=1, keepdims=True))
        rows = jnp.dot(jnp.concatenate(one_hot, axis=0), x, preferred_element_type=F32)
        out_rows = pl.ds(c * SORT_CHUNK, SORT_CHUNK)
        sorted_scr[slot, out_rows, pl.ds(0, d)] = rows.astype(BF16)
        sorted_scr[slot, out_rows, pl.ds(d, LANES)] = _gate_lanes(jnp.concatenate(row_gate, axis=0))

    def copy_chunk_from(src_slot):
        def copy_chunk(local_row, global_row, rows):
            return pltpu.make_async_copy(
                sorted_scr.at[src_slot, pl.ds(local_row, rows), :], xs_hbm.at[pl.ds(global_row, rows), :], sem
            )

        return copy_chunk

    copy_chunk = copy_chunk_from(0)

    @pl.when(step > 0)
    def _():
        _wait_copies([pending[k] for k in range(len(CHUNK_ROWS))], copy_chunk)

    for s in range(2):
        @pl.when(slot == s)
        def _(s=s):
            for k, count in enumerate(_segment_copies(tab_ref, step, copy_chunk_from(s))):
                pending[k] = count

    @pl.when(step == pl.num_programs(0) - 1)
    def _():
        def wait_tile(c, carry):
            zero_tile(0).wait()
            return carry

        def wait_chunk(c, carry):
            zero_chunk(0).wait()
            return carry

        _wait_copies([pending[k] for k in range(len(CHUNK_ROWS))], copy_chunk)
        lax.fori_loop(0, fill_counts[0], wait_chunk, 0)
        lax.fori_loop(0, fill_counts[1], wait_tile, 0)


def _dispatch(tab, gaps, pos_t, gate_t, xn, n_sorted, *, tb, zero_rows):
    n, d = xn.shape
    width = d + LANES
    pick_spec = pl.BlockSpec((TOP_K, tb), lambda i, tab, gaps: (0, i))
    return pl.pallas_call(
        _dispatch_kernel,
        grid_spec=pltpu.PrefetchScalarGridSpec(
            num_scalar_prefetch=2,
            grid=(n // tb,),
            in_specs=[pick_spec, pick_spec, pl.BlockSpec((tb, d), lambda i, tab, gaps: (i, 0))],
            out_specs=pl.BlockSpec(memory_space=pl.ANY),
            scratch_shapes=[
                pltpu.VMEM((2, LOCAL_ROWS, width), BF16),
                pltpu.VMEM((zero_rows, width), BF16),
                pltpu.SMEM((len(CHUNK_ROWS),), jnp.int32),
                pltpu.SMEM((2,), jnp.int32),
                pltpu.SemaphoreType.DMA,
                pltpu.SemaphoreType.DMA,
            ],
        ),
        out_shape=jax.ShapeDtypeStruct((n_sorted, width), BF16),
        compiler_params=_params(("arbitrary",)),
        name="dispatch",
    )(tab, gaps, pos_t, gate_t, xn)


def _experts_kernel(
    expert_ref, used_ref, fresh_ref, next_ref, src_ref, xs_ref, wg_hbm, wu_hbm, wd_hbm, ys_ref,
    wg32, wu32, wd32, wg16, wu16, wd16, sems,
):
    it = pl.program_id(0)
    d = ys_ref.shape[1]

    def weight_copies(e):
        return (
            pltpu.make_async_copy(wg_hbm.at[e], wg32, sems.at[0]),
            pltpu.make_async_copy(wu_hbm.at[e], wu32, sems.at[1]),
            pltpu.make_async_copy(wd_hbm.at[e], wd32, sems.at[2]),
        )

    @pl.when(it == 0)
    def _():
        for copy in weight_copies(expert_ref[0]):
            copy.start()

    @pl.when(used_ref[it] == 0)
    def _():
        ys_ref[...] = jnp.zeros(ys_ref.shape, ys_ref.dtype)

    @pl.when((used_ref[it] > 0) & (fresh_ref[it] == 1))
    def _():
        for copy in weight_copies(expert_ref[it]):
            copy.wait()
        wg16[...] = wg32[...].astype(BF16)
        wu16[...] = wu32[...].astype(BF16)
        wd16[...] = wd32[...].astype(BF16)

        @pl.when(next_ref[it] >= 0)
        def _():
            for copy in weight_copies(next_ref[it]):
                copy.start()

    tm = ys_ref.shape[0]
    for n_sub in range(1, tm // EXPERT_SUB + 1):
        rows = n_sub * EXPERT_SUB

        @pl.when(used_ref[it] == n_sub)
        def _(rows=rows):
            x = xs_ref[pl.ds(0, rows), pl.ds(0, d)]
            gate = jnp.sum(xs_ref[pl.ds(0, rows), pl.ds(d, LANES)].astype(F32), axis=1, keepdims=True)
            g = jnp.dot(x, wg16[...], preferred_element_type=F32)
            u = jnp.dot(x, wu16[...], preferred_element_type=F32)
            h = g * jax.nn.sigmoid(g) * u * gate
            y = jnp.dot(h.astype(BF16), wd16[...], preferred_element_type=F32)
            ys_ref[pl.ds(0, rows), :] = y.astype(ys_ref.dtype)
            if rows < tm:
                ys_ref[pl.ds(rows, tm - rows), :] = jnp.zeros((tm - rows, d), ys_ref.dtype)


def _experts(tiles, xs, w_gate, w_up, w_down, *, tm):
    p = xs.shape[0]
    d, f = w_gate.shape[1:]
    row_map = lambda i, expert, used, fresh, nxt, src: (i, 0)
    src_map = lambda i, expert, used, fresh, nxt, src: (src[i], 0)
    hbm = pl.BlockSpec(memory_space=pl.ANY)
    return pl.pallas_call(
        _experts_kernel,
        grid_spec=pltpu.PrefetchScalarGridSpec(
            num_scalar_prefetch=5,
            grid=(p // tm,),
            in_specs=[pl.BlockSpec((tm, xs.shape[1]), src_map), hbm, hbm, hbm],
            out_specs=pl.BlockSpec((tm, d), row_map),
            scratch_shapes=[
                pltpu.VMEM((d, f), F32), pltpu.VMEM((d, f), F32), pltpu.VMEM((f, d), F32),
                pltpu.VMEM((d, f), BF16), pltpu.VMEM((d, f), BF16), pltpu.VMEM((f, d), BF16),
                pltpu.SemaphoreType.DMA((3,)),
            ],
        ),
        out_shape=jax.ShapeDtypeStruct((p, d), BF16),
        compiler_params=_params(("arbitrary",)),
        name="experts",
    )(*tiles, xs, w_gate, w_up, w_down)


def _combine_kernel(tab_ref, pos_ref, x1_ref, xn_ref, sg_ref, su_ref, sd_ref, fg_ref, ys_hbm, *rest, n_blocks):
    o_refs = rest[:len(n_blocks)]
    local_scr, pending, sems = rest[len(n_blocks):]
    n_tok = x1_ref.shape[0]
    step = pl.program_id(0)
    slot = step % 2

    n_sizes = len(CHUNK_ROWS)

    def copy_chunk_into(dst_slot):
        def copy_chunk(local_row, global_row, rows):
            return pltpu.make_async_copy(
                ys_hbm.at[pl.ds(global_row, rows), :],
                local_scr.at[dst_slot, pl.ds(local_row, rows), :],
                sems.at[dst_slot],
            )

        return copy_chunk

    def fetch(block, dst_slot):
        for k, count in enumerate(_segment_copies(tab_ref, block, copy_chunk_into(dst_slot))):
            pending[dst_slot * n_sizes + k] = count

    @pl.when(step == 0)
    def _():
        local_scr[...] = jnp.zeros(local_scr.shape, local_scr.dtype)
        fetch(0, 0)

    for s in range(2):
        @pl.when((step + 1 < pl.num_programs(0)) & (slot == s))
        def _(s=s):
            fetch(step + 1, 1 - s)

    xn = xn_ref[...]
    sg = jnp.dot(xn, sg_ref[...], preferred_element_type=F32)
    su = jnp.dot(xn, su_ref[...], preferred_element_type=F32)
    acc = x1_ref[...] + jnp.dot(
        (sg * jax.nn.sigmoid(sg) * su).astype(BF16), sd_ref[...], preferred_element_type=F32
    )

    piece_id = lax.broadcasted_iota(jnp.int32, (HOT_PIECE, LANES), 1)
    one_hot = []
    for r in range(n_tok // HOT_PIECE):
        pos = pos_ref[pl.ds(r * HOT_PIECE, HOT_PIECE), :]
        pos_k = [jnp.broadcast_to(pos[:, k:k + 1], (HOT_PIECE, LANES)) for k in range(TOP_K)]
        cols = []
        for c in range(LOCAL_ROWS // LANES):
            col_id = piece_id + c * LANES
            hot = jnp.zeros((HOT_PIECE, LANES), F32)
            for k in range(TOP_K):
                hot = jnp.where(col_id == pos_k[k], 1.0, hot)
            cols.append(hot.astype(BF16))
        one_hot.append(jnp.concatenate(cols, axis=1))
    one_hot = jnp.concatenate(one_hot, axis=0)

    _wait_copies([pending[slot * n_sizes + k] for k in range(n_sizes)], copy_chunk_into(slot))
    routed = jnp.dot(one_hot, local_scr[slot], preferred_element_type=F32)
    out = _rms_norm(acc + routed, fg_ref[...])

    def body(p):
        o_refs[p][...] = out

    _for_part(pl.program_id(0), n_blocks, body)


def _combine(tab, pos, x1, xn, sh_gate, sh_up, sh_down, final_g, ys, part_rows, *, tb):
    n, d = x1.shape
    f = sh_gate.shape[1]
    n_blocks = tuple(r // tb for r in part_rows)
    const = lambda i, tab: (0, 0)
    row = lambda i, tab: (i, 0)
    return pl.pallas_call(
        functools.partial(_combine_kernel, n_blocks=n_blocks),
        grid_spec=pltpu.PrefetchScalarGridSpec(
            num_scalar_prefetch=1,
            grid=(n // tb,),
            in_specs=[
                pl.BlockSpec((tb, TOP_K), row),
                pl.BlockSpec((tb, d), row),
                pl.BlockSpec((tb, d), row),
                pl.BlockSpec((d, f), const, pipeline_mode=pl.Buffered(1)),
                pl.BlockSpec((d, f), const, pipeline_mode=pl.Buffered(1)),
                pl.BlockSpec((f, d), const, pipeline_mode=pl.Buffered(1)),
                pl.BlockSpec((1, d), const),
                pl.BlockSpec(memory_space=pl.ANY),
            ],
            out_specs=[
                pl.BlockSpec((tb, d), _part_map(start, nb, 0))
                for start, nb in zip(_part_starts(n_blocks), n_blocks)
            ],
            scratch_shapes=[
                pltpu.VMEM((2, LOCAL_ROWS, d), BF16),
                pltpu.SMEM((2 * len(CHUNK_ROWS),), jnp.int32),
                pltpu.SemaphoreType.DMA((2,)),
            ],
        ),
        out_shape=[jax.ShapeDtypeStruct((r, d), F32) for r in part_rows],
        compiler_params=_params(("arbitrary",)),
        name="combine",
    )(tab, pos, x1, xn, sh_gate, sh_up, sh_down, final_g, ys)


def _segment_tables(block_counts, n_rows, *, tm):
    padded = (block_counts + SEG_ROWS - 1) // SEG_ROWS * SEG_ROWS
    local_start = jnp.cumsum(padded, axis=1) - padded
    per_expert = jnp.sum(padded, axis=0)
    region = (per_expert + tm - 1) // tm * tm
    region_start = jnp.concatenate([jnp.zeros((1,), jnp.int32), jnp.cumsum(region).astype(jnp.int32)])
    global_start = region_start[None, :N_EXPERTS] + jnp.cumsum(padded, axis=0) - padded
    counts, lists = [], []
    done = jnp.zeros_like(padded)
    for rows, cap in zip(CHUNK_ROWS, CHUNK_MAX):
        per_seg = (padded - done) // rows
        seg_end = jnp.cumsum(per_seg, axis=1)
        j = jnp.arange(cap, dtype=jnp.int32)
        seg_of_chunk = jnp.sum((seg_end[:, None, :] <= j[None, :, None]).astype(jnp.int32), axis=2)
        in_seg = seg_of_chunk[:, :, None] == jnp.arange(N_EXPERTS, dtype=jnp.int32)
        first = done - (seg_end - per_seg) * rows
        for start in (local_start, global_start):
            lists.append(jnp.sum(jnp.where(in_seg, (start + first)[:, None, :], 0), axis=2) + j[None, :] * rows)
        counts.append(seg_end[:, N_EXPERTS - 1:])
        done = done + per_seg * rows
    tab = jnp.concatenate(counts + lists, axis=1).astype(jnp.int32).reshape(-1)

    region_end = region_start[:N_EXPERTS] + per_expert
    gap_start = jnp.concatenate([region_end, region_start[N_EXPERTS:]])
    gap_end = jnp.concatenate([region_start[1:], jnp.full((1,), n_rows, jnp.int32)])
    sub_start = jnp.minimum((gap_start + EXPERT_SUB - 1) // EXPERT_SUB * EXPERT_SUB, gap_end)
    gaps = jnp.concatenate(
        [gap_start, (sub_start - gap_start) // SEG_ROWS, sub_start, (gap_end - sub_start) // EXPERT_SUB]
    ).astype(jnp.int32)

    tile_start = jnp.arange(n_rows // tm, dtype=jnp.int32) * tm
    expert = jnp.clip(
        jnp.sum((region_start[None, :] <= tile_start[:, None]).astype(jnp.int32), axis=1) - 1, 0, N_EXPERTS - 1
    )
    tile_end = jnp.sum(jnp.where(expert[:, None] == jnp.arange(N_EXPERTS), region_end[None, :], 0), axis=1)
    used = jnp.clip((tile_end - tile_start + EXPERT_SUB - 1) // EXPERT_SUB, 0, tm // EXPERT_SUB).astype(jnp.int32)
    fresh = jnp.concatenate([jnp.ones((1,), jnp.int32), (expert[1:] != expert[:-1]).astype(jnp.int32)])
    experts = jnp.arange(N_EXPERTS, dtype=jnp.int32)
    later = (experts[None, :] > expert[:, None]) & (per_expert[None, :] > 0)
    nxt = jnp.min(jnp.where(later, experts[None, :], N_EXPERTS), axis=1)
    nxt = jnp.where(nxt < N_EXPERTS, nxt, -1).astype(jnp.int32)
    tile_id = jnp.arange(n_rows // tm, dtype=jnp.int32)
    src = jnp.minimum(tile_id, jnp.max(jnp.where(used > 0, tile_id, 0)))
    return tab, gaps, (expert, used, fresh, nxt, src), local_start.astype(jnp.int32)


ROW_TM = 512
OUT_PROJ_TM = 256
IN_PROJ_TN = 2048
PROMPT_T = 256
SAMPLE_SEQS = 16
EXPERT_TM = 512
EXPERT_SUB = 128


def kernel(x_prompt, x_sample, state_lru_conv, state_lru_h, state_conf_conv, meta_tokens, norm1_g, w_in, lru_conv_w, lru_conv_b, lru_wa, lru_ba, lru_wi, lru_bi, lru_lambda, conf_conv_w, conf_conv_b, conf_ln_g, conf_ln_b, out_norm_a, out_norm_b, w_out, norm2_g, router_w, router_bias, exp_w_gate, exp_w_up, exp_w_down, sh_w_gate, sh_w_up, sh_w_down, final_norm_g):
    b_p, seq, d = x_prompt.shape
    b_s, t_s, _ = x_sample.shape
    n_p = b_p * seq
    n_s = b_s * t_s
    n = n_p + n_s
    x_parts = (x_prompt.reshape(n_p, d), x_sample.reshape(n_s, d))

    row = lambda v: v.reshape(1, -1)
    mixer_w = (
        lru_conv_w[0], row(lru_conv_b[0]),
        lru_wa[0], row(lru_ba[0]),
        lru_wi[0], row(lru_bi[0]),
        row(lru_lambda[0]),
        conf_conv_w[0], row(conf_conv_b[0]), row(conf_ln_g[0]), row(conf_ln_b[0]),
        row(out_norm_a[0]), row(out_norm_b[0]),
    )

    proj, proj_m = _in_proj(x_parts, meta_tokens, row(norm1_g[0]), w_in[0], tm=ROW_TM, tn=IN_PROJ_TN)
    _, m_lc, m_h, m_cc = _mixer(
        proj_m, 0,
        jnp.zeros((1, 1, LRU_CONV - 1, W_A), F32), jnp.zeros((1, 1, W_A), F32),
        jnp.zeros((1, 1, CONF_KERNEL - 1, W_B), F32),
        mixer_w, n_seq=1, n_t=N_META, n_tiles=1,
    )

    y_p, p_lc, p_h, p_cc = _mixer(
        proj, 0,
        jnp.broadcast_to(m_lc, (1, b_p) + m_lc.shape[2:]), jnp.broadcast_to(m_h, (b_p,) + m_h.shape[1:]),
        jnp.broadcast_to(m_cc, (1, b_p) + m_cc.shape[2:]),
        mixer_w, n_seq=1, n_t=PROMPT_T, n_tiles=seq // PROMPT_T,
    )
    y_s, s_lc, s_h, s_cc = _mixer(
        proj, n_p // (SAMPLE_SEQS * t_s),
        state_lru_conv, state_lru_h[0].reshape(b_s, 1, W_A), state_conf_conv,
        mixer_w, n_seq=SAMPLE_SEQS, n_t=t_s, n_tiles=1,
    )

    x1, xn2, logits = _out_proj(
        (y_p, y_s), x_parts, w_out[0], row(norm2_g[0]), router_w[0].astype(BF16), tm=OUT_PROJ_TM
    )

    eidx_t, gate_t, rank_t, block_counts = _route(logits.T, router_bias[0].reshape(N_EXPERTS, 1), tl=TOKEN_BLOCK)
    n_blocks = n // TOKEN_BLOCK
    n_sorted = n_blocks * (TOKEN_BLOCK * TOP_K + N_EXPERTS * (SEG_ROWS - 1)) + N_EXPERTS * (EXPERT_TM - 1)
    n_sorted = -(-n_sorted // EXPERT_TM) * EXPERT_TM
    tab, gaps, tiles, local_start = _segment_tables(block_counts[:, :, 0], n_sorted, tm=EXPERT_TM)
    expert_one_hot = eidx_t[:, :, None] == jnp.arange(N_EXPERTS, dtype=jnp.int32)
    token_start = jnp.repeat(local_start, TOKEN_BLOCK, axis=0)
    pos_t = jnp.sum(jnp.where(expert_one_hot, token_start[None], 0), axis=2) + rank_t

    xs = _dispatch(tab, gaps, pos_t, gate_t, xn2, n_sorted, tb=TOKEN_BLOCK, zero_rows=EXPERT_SUB)
    ys = _experts(tiles, xs, exp_w_gate[0], exp_w_up[0], exp_w_down[0], tm=EXPERT_TM)
    out_p, out_s = _combine(
        tab, pos_t.T, x1, xn2,
        sh_w_gate[0].astype(BF16), sh_w_up[0].astype(BF16), sh_w_down[0].astype(BF16),
        row(final_norm_g), ys, (n_p, n_s), tb=TOKEN_BLOCK,
    )

    return (
        out_p.reshape(b_p, seq, d), out_s.reshape(b_s, t_s, d),
        p_lc, p_h.reshape(1, b_p, W_A), p_cc,
        s_lc, s_h.reshape(1, b_s, W_A), s_cc,
    )
```
